```python
import math
import jax, jax.numpy as jnp
from jax import lax
import numpy as np

D_MODEL = 1024
BATCH = 8
SEQ = 4096
DEPTH = 4

SGU_WIDTH = 512
SGU_GROUPS = 4
SGU_GROUP_DIM = SGU_WIDTH // SGU_GROUPS
SGU_CHUNK = 128
HY_WIDTH = 512
HY_SHORT_CONV = 3
HY_EMB_DIM = 33
HY_BANDS = (HY_EMB_DIM - 1) // 2
HY_FILTER_HIDDEN = 64
HY_FAST_DECAY_PCT = 0.3
HY_SLOW_DECAY_PCT = 1.5
HY_DECAY_TARGET = 1e-2
HY_FILTER_INIT_SCALE = 0.1
XA_HEADS = 4
XA_HEAD_DIM = 128
XA_WIDTH = XA_HEADS * XA_HEAD_DIM
MEM_LEN = 256
N_BRANCH = 3
BRANCH_WIDTH = 512
IN_COLS = 2 * SGU_WIDTH + 3 * HY_WIDTH + XA_WIDTH + N_BRANCH * D_MODEL
N_EXPERTS = 16
EC_CAPACITY = 2
EXPERT_FF = 1024
NORM_EPS = 1e-6
LN_EPS = 1e-5

kernel_name = 'hybrid_sgu_hyena_memxattn_ecmoe_encoder'


def rms_norm(x, g):
    xf = x.astype(jnp.float32)
    y = xf * lax.rsqrt(jnp.mean(xf * xf, axis=-1, keepdims=True) + NORM_EPS)
    return (y * g.astype(jnp.float32)).astype(x.dtype)


def layer_norm(x, g, b):
    xf = x.astype(jnp.float32)
    mu = jnp.mean(xf, axis=-1, keepdims=True)
    var = jnp.mean(jnp.square(xf - mu), axis=-1, keepdims=True)
    y = (xf - mu) * lax.rsqrt(var + LN_EPS)
    return (y * g.astype(jnp.float32) + b.astype(jnp.float32)).astype(x.dtype)


def spatial_gating(z, ln_g, ln_b, w_s, b_s):
    B, L, _ = z.shape
    u, v = jnp.split(z, 2, axis=-1)
    v = layer_norm(v, ln_g, ln_b)
    v = v.reshape(B, L // SGU_CHUNK, SGU_CHUNK, SGU_GROUPS, SGU_GROUP_DIM)
    v = jnp.einsum('gpq,bnqgc->bnpgc', w_s, v) + b_s.T[:, :, None]
    return u * v.reshape(B, L, SGU_WIDTH)


def hyena_filters(L, w1, b1, w2, b2, w3, b3, w4, freq):
    f32 = jnp.float32
    t = jnp.linspace(0.0, 1.0, L, dtype=f32)[:, None]
    w = 2.0 * math.pi * jnp.arange(L, dtype=f32)[:, None] / L
    f = jnp.linspace(1e-4, HY_BANDS - 1, HY_BANDS, dtype=f32)[None, :]
    z = jnp.concatenate([t, jnp.cos(f * w), -jnp.sin(f * w)], axis=-1)
    fr = freq.astype(f32)
    h = jnp.sin(fr * (z @ w1.astype(f32) + b1.astype(f32)))
    h = jnp.sin(fr * (h @ w2.astype(f32) + b2.astype(f32)))
    h = jnp.sin(fr * (h @ w3.astype(f32) + b3.astype(f32)))
    h = h @ w4.astype(f32)
    max_decay = math.log(HY_DECAY_TARGET) / HY_FAST_DECAY_PCT
    min_decay = math.log(HY_DECAY_TARGET) / HY_SLOW_DECAY_PCT
    deltas = jnp.abs(jnp.linspace(min_decay, max_decay, HY_WIDTH, dtype=f32))
    decay = jnp.exp(-t * deltas)
    h_fwd = h[:, :HY_WIDTH] * decay
    h_bwd = h[:, HY_WIDTH:] * decay
    zero = jnp.zeros((1, HY_WIDTH), f32)
    return jnp.concatenate([h_fwd, zero, h_bwd[:0:-1]], axis=0)


def hyena_branch(p, conv_w, conv_b, filt_k, bias):
    B, L, _ = p.shape
    pad = HY_SHORT_CONV // 2
    pp = jnp.pad(p, ((0, 0), (pad, pad), (0, 0)))
    p = sum(conv_w[k] * pp[:, k:k + L] for k in range(HY_SHORT_CONV)) + conv_b
    x0, x1, v = jnp.split(p, 3, axis=-1)
    u = (x1 * v).astype(jnp.float32)
    n = 2 * L
    uf = jnp.fft.rfft(u, n=n, axis=1)
    kf = jnp.fft.rfft(filt_k, n=n, axis=0)
    y = jnp.fft.irfft(uf * kf[None], n=n, axis=1)[:, :L]
    y = y + u * bias.astype(jnp.float32)
    return x0 * y.astype(p.dtype)


def memory_attention(q, mem_n, w_kv):
    B, L, _ = q.shape
    M = mem_n.shape[1]
    k, v = jnp.split(mem_n @ w_kv, 2, axis=-1)
    q = q.reshape(B, L, XA_HEADS, XA_HEAD_DIM)
    k = k.reshape(B, M, XA_HEADS, XA_HEAD_DIM)
    v = v.reshape(B, M, XA_HEADS, XA_HEAD_DIM)
    s = jnp.einsum('bshd,bmhd->bhsm', q.astype(jnp.float32), k.astype(jnp.float32)) / math.sqrt(XA_HEAD_DIM)
    pr = jax.nn.softmax(s, axis=-1)
    o = jnp.einsum('bhsm,bmhd->bshd', pr.astype(v.dtype), v)
    return o.reshape(B, L, XA_WIDTH)


def expert_choice_moe(h, w_router, w1, w3, w2):
    B, L, D = h.shape
    cap = EC_CAPACITY * L // N_EXPERTS
    aff = jax.nn.softmax((h @ w_router).astype(jnp.float32), axis=-1)
    gates, idx = lax.top_k(jnp.swapaxes(aff, 1, 2), cap)
    xin = jax.vmap(lambda hb, ib: hb[ib])(h, idx)
    hid = jax.nn.silu(jnp.einsum('becd,edf->becf', xin, w1)) * jnp.einsum('becd,edf->becf', xin, w3)
    y = jnp.einsum('becf,efd->becd', hid, w2) * gates[..., None].astype(h.dtype)
    return jax.vmap(lambda yb, ib: jnp.zeros((L, D), yb.dtype).at[ib.reshape(-1)].add(yb.reshape(-1, D)))(y, idx)


def setup_inputs(seed: int = 0) -> dict:
    key = jax.random.key(seed)
    ks = iter(jax.random.split(key, 40))

    def nrm(shape, scale):
        return jax.random.normal(next(ks), shape, jnp.float32) * scale

    D, Ld, H = D_MODEL, DEPTH, HY_FILTER_HIDDEN
    return {
        'x': nrm((BATCH, SEQ, D), 1.0),
        'mem': nrm((BATCH, MEM_LEN, D), 1.0),
        'norm1_g': 1.0 + nrm((Ld, D), 0.02),
        'w_in': nrm((Ld, D, IN_COLS), D ** -0.5),
        'gate_b': nrm((Ld, N_BRANCH * D), 0.02),
        'sgu_ln_g': 1.0 + nrm((Ld, SGU_WIDTH), 0.02),
        'sgu_ln_b': nrm((Ld, SGU_WIDTH), 0.02),
        'sgu_w': nrm((Ld, SGU_GROUPS, SGU_CHUNK, SGU_CHUNK), SGU_CHUNK ** -0.5),
        'sgu_b': 1.0 + nrm((Ld, SGU_GROUPS, SGU_CHUNK), 0.02),
        'hy_conv_w': nrm((Ld, HY_SHORT_CONV, 3 * HY_WIDTH), HY_SHORT_CONV ** -0.5),
        'hy_conv_b': nrm((Ld, 3 * HY_WIDTH), 0.02),
        'hy_f_w1': nrm((Ld, HY_EMB_DIM, H), HY_EMB_DIM ** -0.5),
        'hy_f_b1': nrm((Ld, H), 0.1),
        'hy_f_w2': nrm((Ld, H, H), H ** -0.5),
        'hy_f_b2': nrm((Ld, H), 0.1),
        'hy_f_w3': nrm((Ld, H, H), H ** -0.5),
        'hy_f_b3': nrm((Ld, H), 0.1),
        'hy_f_w4': nrm((Ld, H, 2 * HY_WIDTH), HY_FILTER_INIT_SCALE * H ** -0.5),
        'hy_f_freq': 1.0 + nrm((Ld, H), 0.02),
        'hy_bias': nrm((Ld, HY_WIDTH), 1.0),
        'mem_norm_g': 1.0 + nrm((Ld, D), 0.02),
        'w_kv': nrm((Ld, D, 2 * XA_WIDTH), D ** -0.5),
        'w_branch': nrm((Ld, N_BRANCH, BRANCH_WIDTH, D), BRANCH_WIDTH ** -0.5),
        'w_out': nrm((Ld, D, D), D ** -0.5),
        'norm2_g': 1.0 + nrm((Ld, D), 0.02),
        'w_router': nrm((Ld, D, N_EXPERTS), D ** -0.5),
        'w1': nrm((Ld, N_EXPERTS, D, EXPERT_FF), D ** -0.5),
        'w3': nrm((Ld, N_EXPERTS, D, EXPERT_FF), D ** -0.5),
        'w2': nrm((Ld, N_EXPERTS, EXPERT_FF, D), EXPERT_FF ** -0.5),
        'final_g': 1.0 + nrm((D,), 0.02),
    }


def reference(x, mem, norm1_g, w_in, gate_b, sgu_ln_g, sgu_ln_b, sgu_w, sgu_b,
              hy_conv_w, hy_conv_b, hy_f_w1, hy_f_b1, hy_f_w2, hy_f_b2, hy_f_w3, hy_f_b3,
              hy_f_w4, hy_f_freq, hy_bias, mem_norm_g, w_kv, w_branch, w_out,
              norm2_g, w_router, w1, w3, w2, final_g):
    B, L, D = x.shape
    c0 = 2 * SGU_WIDTH
    c1 = c0 + 3 * HY_WIDTH
    c2 = c1 + XA_WIDTH
    for l in range(DEPTH):
        h = rms_norm(x, norm1_g[l])
        cols = h @ w_in[l]
        a = spatial_gating(jax.nn.gelu(cols[..., :c0], approximate=False),
                           sgu_ln_g[l], sgu_ln_b[l], sgu_w[l], sgu_b[l])
        filt = hyena_filters(L, hy_f_w1[l], hy_f_b1[l], hy_f_w2[l], hy_f_b2[l],
                             hy_f_w3[l], hy_f_b3[l], hy_f_w4[l], hy_f_freq[l])
        hb = hyena_branch(cols[..., c0:c1], hy_conv_w[l], hy_conv_b[l], filt, hy_bias[l])
        c = memory_attention(cols[..., c1:c2], rms_norm(mem, mem_norm_g[l]), w_kv[l])
        gates = jax.nn.sigmoid((cols[..., c2:] + gate_b[l]).astype(jnp.float32))
        gates = gates.astype(x.dtype).reshape(B, L, N_BRANCH, D)
        br = jnp.stack([a, hb, c], axis=2)
        proj = jnp.einsum('bsnc,ncd->bsnd', br, w_branch[l])
        merged = jnp.sum(gates * proj, axis=2)
        x = x + merged @ w_out[l]
        h = rms_norm(x, norm2_g[l])
        x = x + expert_choice_moe(h, w_router[l], w1[l], w3[l], w2[l])
    return rms_norm(x, final_g)
```

```python
import functools
import math

import numpy as np
import jax
import jax.numpy as jnp
from jax import lax
from jax.experimental import pallas as pl
from jax.experimental.pallas import tpu as pltpu

F32 = jnp.float32
BF16 = jnp.bfloat16

D_MODEL = 1024
SGU_WIDTH = 512
SGU_GROUPS = 4
SGU_CHUNK = 128
HY_WIDTH = 512
HY_EMB_DIM = 33
HY_BANDS = (HY_EMB_DIM - 1) // 2
HY_FILTER_HIDDEN = 64
HY_FAST_DECAY_PCT = 0.3
HY_SLOW_DECAY_PCT = 1.5
HY_DECAY_TARGET = 1e-2
XA_HEADS = 4
XA_HEAD_DIM = 128
XA_WIDTH = XA_HEADS * XA_HEAD_DIM
N_EXPERTS = 16
EC_CAPACITY = 2
NORM_EPS = 1e-6
LN_EPS = 1e-5

C_SGU = 2 * SGU_WIDTH
C_HY = C_SGU + 3 * HY_WIDTH
C_Q = C_HY + XA_WIDTH

LANES = 128
SUBLANES = 8
VMEM_LIMIT = 56 * 1024 * 1024

FFT_N2 = 128
FFT_ROWS = 256


def _cparams(semantics):
    return pltpu.CompilerParams(dimension_semantics=semantics, vmem_limit_bytes=VMEM_LIMIT)


def _rms(x, g):
    return x * lax.rsqrt(jnp.mean(x * x, axis=-1, keepdims=True) + NORM_EPS) * g


def _dot(a, b):
    return jnp.dot(a, b, preferred_element_type=F32)


def _dot_nt(a, b):
    return lax.dot_general(a, b, (((1,), (1,)), ((), ())), preferred_element_type=F32)


def _split_bf16(x):
    hi = x.astype(BF16)
    lo = (x - hi.astype(F32)).astype(BF16)
    return hi, lo


def _kv_body(mem_ref, g_ref, w_ref, k_ref, v_ref):
    m = _rms(mem_ref[...], g_ref[...]).astype(BF16)
    kv = _dot(m, w_ref[...])
    k_ref[...] = kv[:, :XA_WIDTH].astype(BF16)
    v_ref[...] = kv[:, XA_WIDTH:].astype(BF16)


def _kv_call(mem, mem_norm_g, w_kv_bf):
    B, M, D = mem.shape
    depth = w_kv_bf.shape[0]
    return pl.pallas_call(
        _kv_body,
        grid=(depth, B),
        in_specs=[
            pl.BlockSpec((None, M, D), lambda l, b: (b, 0, 0)),
            pl.BlockSpec((None, 1, D), lambda l, b: (l, 0, 0)),
            pl.BlockSpec((None, D, 2 * XA_WIDTH), lambda l, b: (l, 0, 0)),
        ],
        out_specs=[
            pl.BlockSpec((None, None, M, XA_WIDTH), lambda l, b: (l, b, 0, 0)),
            pl.BlockSpec((None, None, M, XA_WIDTH), lambda l, b: (l, b, 0, 0)),
        ],
        out_shape=[jax.ShapeDtypeStruct((depth, B, M, XA_WIDTH), BF16)] * 2,
        compiler_params=_cparams(("arbitrary", "arbitrary")),
        name="kv",
    )(mem, mem_norm_g.reshape(depth, 1, D), w_kv_bf)


def _filter_body(w1t_ref, w1c_ref, w1s_ref, b1_ref, w2_ref, b2_ref, w3_ref, b3_ref, w4_ref, fr_ref,
                 bands_ref, deltas_ref, o_ref, *, seq_len, rows):
    d = pl.program_id(1)
    r0 = pl.program_id(2) * rows
    ridx = r0 + lax.broadcasted_iota(jnp.int32, (rows, 1), 0)
    pos = (ridx + d * (seq_len - 2 * ridx)).astype(F32)
    t = pos * (1.0 / (seq_len - 1))
    w = pos * (2.0 * math.pi / seq_len)
    ang = w * bands_ref[...]
    hp = lax.Precision.HIGHEST
    fr = fr_ref[...]
    pre = (t * w1t_ref[...] + jnp.dot(jnp.cos(ang), w1c_ref[...], precision=hp, preferred_element_type=F32)
           - jnp.dot(jnp.sin(ang), w1s_ref[...], precision=hp, preferred_element_type=F32) + b1_ref[...])
    h = jnp.sin(fr * pre)
    h = jnp.sin(fr * (jnp.dot(h, w2_ref[...], precision=hp, preferred_element_type=F32) + b2_ref[...]))
    h = jnp.sin(fr * (jnp.dot(h, w3_ref[...], precision=hp, preferred_element_type=F32) + b3_ref[...]))
    h = jnp.dot(h, w4_ref[...], precision=hp, preferred_element_type=F32)
    h = h * jnp.exp(-t * deltas_ref[...])
    h = jnp.where((ridx == 0) & (d == 1), 0.0, h)
    for c in range(HY_WIDTH // LANES):
        o_ref[c] = h[:, c * LANES:(c + 1) * LANES]


def _filter_call(seq_len, w1, b1, w2, b2, w3, b3, w4, freq):
    depth = w1.shape[0]
    H = HY_FILTER_HIDDEN
    rows = 1024
    nct = HY_WIDTH // LANES
    bands = np.zeros((1, LANES), np.float32)
    bands[0, :HY_BANDS] = np.linspace(1e-4, HY_BANDS - 1, HY_BANDS)
    max_decay = math.log(HY_DECAY_TARGET) / HY_FAST_DECAY_PCT
    min_decay = math.log(HY_DECAY_TARGET) / HY_SLOW_DECAY_PCT
    deltas = np.abs(np.linspace(min_decay, max_decay, HY_WIDTH)).astype(np.float32).reshape(1, HY_WIDTH)
    pad = ((0, 0), (0, LANES - HY_BANDS), (0, 0))
    w1t = w1[:, 0:1, :]
    w1c = jnp.pad(w1[:, 1:1 + HY_BANDS, :], pad)
    w1s = jnp.pad(w1[:, 1 + HY_BANDS:, :], pad)
    vec = lambda n: pl.BlockSpec((None, 1, n), lambda l, d, r: (l, 0, 0))
    mat = lambda m, n: pl.BlockSpec((None, m, n), lambda l, d, r: (l, 0, 0))
    return pl.pallas_call(
        functools.partial(_filter_body, seq_len=seq_len, rows=rows),
        grid=(depth, 2, seq_len // rows),
        in_specs=[
            vec(H), mat(LANES, H), mat(LANES, H), vec(H), mat(H, H), vec(H), mat(H, H), vec(H),
            pl.BlockSpec((None, H, HY_WIDTH), lambda l, d, r: (l, 0, d)),
            vec(H),
            pl.BlockSpec((1, LANES), lambda l, d, r: (0, 0)),
            pl.BlockSpec((1, HY_WIDTH), lambda l, d, r: (0, 0)),
        ],
        out_specs=pl.BlockSpec((None, None, nct, rows, LANES), lambda l, d, r: (l, d, 0, r, 0)),
        out_shape=jax.ShapeDtypeStruct((depth, 2, nct, seq_len, LANES), F32),
        compiler_params=_cparams(("arbitrary", "arbitrary", "arbitrary")),
        name="hyena_filter",
    )(w1t, w1c, w1s, b1.reshape(depth, 1, H), w2, b2.reshape(depth, 1, H), w3, b3.reshape(depth, 1, H),
      w4, freq.reshape(depth, 1, H), jnp.asarray(bands), jnp.asarray(deltas))


def _mixer_a_body(x_ref, xp_ref, xn_ref, g1_ref, win_ref, gb_ref, lng_ref, lnb_ref, sw_ref, sbt_ref,
                  cw_ref, cb_ref, k_ref, v_ref, wba_ref, wbc_ref,
                  mp_ref, gh_ref, x0_ref, u_ref, *, tm):
    j = pl.program_id(1)
    nj = pl.num_programs(1)
    g1 = g1_ref[...]
    h = _rms(x_ref[...], g1)
    hb = h.astype(BF16)

    zs = _dot(hb, win_ref[:, 0:C_SGU])
    zs = 0.5 * zs * (1.0 + lax.erf(zs * (1.0 / math.sqrt(2.0))))
    us = zs[:, :SGU_WIDTH]
    vs = zs[:, SGU_WIDTH:]
    mu = jnp.mean(vs, axis=-1, keepdims=True)
    vc = vs - mu
    var = jnp.mean(vc * vc, axis=-1, keepdims=True)
    vn = (vc * lax.rsqrt(var + LN_EPS) * lng_ref[...] + lnb_ref[...]).astype(BF16)
    gd = SGU_WIDTH // SGU_GROUPS
    chunks = []
    for c in range(tm // SGU_CHUNK):
        groups = []
        for g in range(SGU_GROUPS):
            blk = vn[c * SGU_CHUNK:(c + 1) * SGU_CHUNK, g * gd:(g + 1) * gd]
            groups.append(_dot(sw_ref[g], blk))
        chunks.append(jnp.concatenate(groups, axis=1) + sbt_ref[...])
    a = us * jnp.concatenate(chunks, axis=0)

    q = _dot(hb, win_ref[:, C_HY:C_Q])
    scale = 1.0 / math.sqrt(XA_HEAD_DIM)
    heads = []
    for hd in range(XA_HEADS):
        sl = slice(hd * XA_HEAD_DIM, (hd + 1) * XA_HEAD_DIM)
        s = _dot_nt(q[:, sl].astype(BF16), k_ref[:, sl]) * scale
        s = s - jnp.max(s, axis=-1, keepdims=True)
        p = jnp.exp(s)
        p = p / jnp.sum(p, axis=-1, keepdims=True)
        heads.append(_dot(p.astype(BF16), v_ref[:, sl]))
    c_att = jnp.concatenate(heads, axis=1)

    gates = jax.nn.sigmoid(_dot(hb, win_ref[:, C_Q:]) + gb_ref[...])
    pa = _dot(a.astype(BF16), wba_ref[...])
    pc = _dot(c_att.astype(BF16), wbc_ref[...])
    mp_ref[...] = gates[:, 0:D_MODEL] * pa + gates[:, 2 * D_MODEL:] * pc
    gh_ref[...] = gates[:, D_MODEL:2 * D_MODEL].astype(BF16)

    hprev = _rms(xp_ref[...], g1).astype(BF16)
    hnext = _rms(xn_ref[...], g1).astype(BF16)
    hext = jnp.concatenate([hprev, hb, hnext], axis=0)
    pe = _dot(hext, win_ref[:, C_SGU:C_HY])
    rows = lax.broadcasted_iota(jnp.int32, (tm, 1), 0)
    pm1 = jnp.where((rows == 0) & (j == 0), 0.0, pe[SUBLANES - 1:SUBLANES - 1 + tm])
    pp1 = jnp.where((rows == tm - 1) & (j == nj - 1), 0.0, pe[SUBLANES + 1:SUBLANES + 1 + tm])
    p0 = pe[SUBLANES:SUBLANES + tm]
    pconv = cw_ref[0:1, :] * pm1 + cw_ref[1:2, :] * p0 + cw_ref[2:3, :] * pp1 + cb_ref[...]
    x0 = pconv[:, :HY_WIDTH]
    u = pconv[:, HY_WIDTH:2 * HY_WIDTH] * pconv[:, 2 * HY_WIDTH:]
    for c in range(HY_WIDTH // LANES):
        x0_ref[c] = x0[:, c * LANES:(c + 1) * LANES]
        u_ref[c] = u[:, c * LANES:(c + 1) * LANES]


def _mixer_a_call(l, x, norm1_g, w_in_bf, gate_b, sgu_ln_g, sgu_ln_b, sgu_w_bf, sgu_bt, conv_w, conv_b,
                  k_all, v_all, w_branch_bf):
    B, L, D = x.shape
    tm = 256
    nct = HY_WIDTH // LANES
    in_cols = w_in_bf.shape[-1]
    hb_per_tile = tm // SUBLANES
    n_hb = L // SUBLANES
    const = lambda shape: pl.BlockSpec((None,) + shape, lambda b, j: (l,) + (0,) * len(shape),
                                       pipeline_mode=pl.Buffered(1))
    return pl.pallas_call(
        functools.partial(_mixer_a_body, tm=tm),
        grid=(B, L // tm),
        in_specs=[
            pl.BlockSpec((None, tm, D), lambda b, j: (b, j, 0)),
            pl.BlockSpec((None, SUBLANES, D), lambda b, j: (b, jnp.maximum(j * hb_per_tile - 1, 0), 0)),
            pl.BlockSpec((None, SUBLANES, D), lambda b, j: (b, jnp.minimum((j + 1) * hb_per_tile, n_hb - 1), 0)),
            const((1, D)),
            const((D, in_cols)),
            const((1, 3 * D)),
            const((1, SGU_WIDTH)),
            const((1, SGU_WIDTH)),
            const((SGU_GROUPS, SGU_CHUNK, SGU_CHUNK)),
            const((SGU_CHUNK, SGU_WIDTH)),
            const((3, 3 * HY_WIDTH)),
            const((1, 3 * HY_WIDTH)),
            pl.BlockSpec((None, None, k_all.shape[2], XA_WIDTH), lambda b, j: (l, b, 0, 0)),
            pl.BlockSpec((None, None, k_all.shape[2], XA_WIDTH), lambda b, j: (l, b, 0, 0)),
            pl.BlockSpec((None, None, SGU_WIDTH, D), lambda b, j: (l, 0, 0, 0), pipeline_mode=pl.Buffered(1)),
            pl.BlockSpec((None, None, XA_WIDTH, D), lambda b, j: (l, 2, 0, 0), pipeline_mode=pl.Buffered(1)),
        ],
        out_specs=[
            pl.BlockSpec((None, tm, D), lambda b, j: (b, j, 0)),
            pl.BlockSpec((None, tm, D), lambda b, j: (b, j, 0)),
            pl.BlockSpec((None, nct, tm, LANES), lambda b, j: (b, 0, j, 0)),
            pl.BlockSpec((None, nct, tm, LANES), lambda b, j: (b, 0, j, 0)),
        ],
        out_shape=[
            jax.ShapeDtypeStruct((B, L, D), F32),
            jax.ShapeDtypeStruct((B, L, D), BF16),
            jax.ShapeDtypeStruct((B, nct, L, LANES), F32),
            jax.ShapeDtypeStruct((B, nct, L, LANES), F32),
        ],
        compiler_params=_cparams(("arbitrary", "arbitrary")),
        name="mixer_a",
    )(x, x, x, norm1_g, w_in_bf, gate_b, sgu_ln_g, sgu_ln_b, sgu_w_bf, sgu_bt, conv_w, conv_b,
      k_all, v_all, w_branch_bf, w_branch_bf)


def _fft_constants(seq_len):
    n = 2 * seq_len
    n2 = FFT_N2
    n1 = n // n2
    nz = seq_len // n2
    lo = FFT_ROWS // nz
    tiles = n2 // lo
    k1n = n1 // 2 + 1
    assert lo == SUBLANES and nz * lo == FFT_ROWS
    k1 = np.arange(k1n).reshape(k1n, 1, 1, 1, 1).astype(np.float64)
    part = np.arange(2).reshape(1, 2, 1, 1, 1)
    a = np.arange(lo).reshape(1, 1, lo, 1, 1)
    m1 = np.arange(nz).reshape(1, 1, 1, nz, 1).astype(np.float64)
    a2 = np.arange(lo).reshape(1, 1, 1, 1, lo)
    fwd = np.zeros((tiles, k1n, 2, lo, nz, lo), np.float64)
    for j in range(tiles):
        theta = 2.0 * np.pi * (m1 * k1 / n1 + (lo * j + a) * k1 / n)
        val = np.where(part == 0, np.cos(theta), -np.sin(theta))
        fwd[j] = val * (a == a2)
    fwd = fwd.reshape(tiles, k1n * 2 * lo, nz * lo)
    weight = np.where((np.arange(k1n) == 0) | (np.arange(k1n) == n1 // 2), 1.0, 2.0) / n
    inv = fwd.reshape(tiles, k1n, 2 * lo, nz * lo) * weight.reshape(1, k1n, 1, 1)
    inv = inv.reshape(tiles, k1n * 2 * lo, nz * lo).transpose(0, 2, 1)
    phi = 2.0 * np.pi * np.outer(np.arange(n2), np.arange(n2)) / n2
    c, s = np.cos(phi), np.sin(phi)
    g_fwd = np.block([[c, s], [-s, c]])
    g_inv = np.block([[c, -s], [s, c]])
    f32 = lambda v: jnp.asarray(v.astype(np.float32)).astype(BF16)
    return f32(fwd), f32(inv), f32(g_fwd), f32(g_inv), (tiles, k1n, lo, nz)


def _hyena_body(filt_ref, u_ref, x0_ref, bias_ref, fa_ref, ia_ref, gf_ref, gi_ref, o_ref, z_ref, k_ref,
                *, tiles, k1n, lo, nz):
    s = pl.program_id(1)
    half = FFT_N2
    rows = nz * lo

    def stage_a(src_ref):
        for j in range(tiles):
            tile = src_ref[:, j].reshape(rows, LANES).astype(BF16)
            z_ref[:, :, j] = _dot(fa_ref[j], tile).reshape(k1n, 2, lo, LANES)

    def spectrum(k1):
        return _dot(gf_ref[...], z_ref[k1].reshape(2 * half, LANES).astype(BF16))

    @pl.when(s == 0)
    def _():
        k_ref[...] = jnp.zeros(k_ref.shape, F32)

    @pl.when(s < 2)
    def _():
        stage_a(filt_ref)
        odd_sign = jnp.where(s == 0, 1.0, -1.0).astype(F32)
        for k1 in range(k1n):
            x = spectrum(k1)
            k_ref[k1] += x if k1 % 2 == 0 else odd_sign * x

    @pl.when(s >= 2)
    def _():
        stage_a(u_ref)
        for k1 in range(k1n):
            x = spectrum(k1)
            xr, xi = x[:half], x[half:]
            kr, ki = k_ref[k1, :half], k_ref[k1, half:]
            y = jnp.concatenate([xr * kr - xi * ki, xr * ki + xi * kr], axis=0).astype(BF16)
            z_ref[k1] = _dot(gi_ref[...], y).reshape(2, tiles, lo, LANES)
        bias = bias_ref[...]
        for j in range(tiles):
            zt = z_ref[:, :, j].reshape(k1n * 2 * lo, LANES).astype(BF16)
            y = _dot(ia_ref[j], zt)
            uu = u_ref[:, j].reshape(rows, LANES)
            x0 = x0_ref[:, j].reshape(rows, LANES)
            o_ref[:, j] = (x0 * (y + bias * uu)).reshape(nz, lo, LANES)


def _hyena_call(l, filt, u, x0, hy_bias, consts):
    fa, ia, gf, gi, (tiles, k1n, lo, nz) = consts
    B, nct, L, _ = u.shape
    view = lambda a: a.reshape(a.shape[:-2] + (nz, tiles, lo, LANES))
    seq = (nz, tiles, lo, LANES)
    bidx = lambda s: jnp.maximum(s - 2, 0)
    single = lambda shape: pl.BlockSpec(shape, lambda c, s: (0,) * len(shape), pipeline_mode=pl.Buffered(1))
    out = pl.pallas_call(
        functools.partial(_hyena_body, tiles=tiles, k1n=k1n, lo=lo, nz=nz),
        grid=(nct, B + 2),
        in_specs=[
            pl.BlockSpec((None, None, None) + seq, lambda c, s: (l, jnp.minimum(s, 1), c, 0, 0, 0, 0)),
            pl.BlockSpec((None, None) + seq, lambda c, s: (bidx(s), c, 0, 0, 0, 0)),
            pl.BlockSpec((None, None) + seq, lambda c, s: (bidx(s), c, 0, 0, 0, 0)),
            pl.BlockSpec((None, None, 1, LANES), lambda c, s: (l, c, 0, 0)),
            single(fa.shape), single(ia.shape), single(gf.shape), single(gi.shape),
        ],
        out_specs=pl.BlockSpec((None, None) + seq, lambda c, s: (bidx(s), c, 0, 0, 0, 0)),
        out_shape=jax.ShapeDtypeStruct((B, nct) + seq, F32),
        scratch_shapes=[
            pltpu.VMEM((k1n, 2, tiles, lo, LANES), F32),
            pltpu.VMEM((k1n, 2 * FFT_N2, LANES), F32),
        ],
        compiler_params=_cparams(("arbitrary", "arbitrary")),
        name="hyena_fft",
    )(view(filt), view(u), view(x0), hy_bias, fa, ia, gf, gi)
    return out.reshape(B, nct, L, LANES)


def _mixer_c_body(x_ref, mp_ref, gh_ref, hy_ref, wbh_ref, wo_ref, g2_ref, wr_ref, wrt_ref,
                  x1_ref, he_ref, afft_ref):
    hy = jnp.concatenate([hy_ref[c] for c in range(HY_WIDTH // LANES)], axis=1).astype(BF16)
    merged = mp_ref[...] + gh_ref[...].astype(F32) * _dot(hy, wbh_ref[...])
    x1 = x_ref[...] + _dot(merged.astype(BF16), wo_ref[...])
    x1_ref[...] = x1
    h2 = _rms(x1, g2_ref[...])
    he_ref[:, :D_MODEL] = h2
    hh, hl = _split_bf16(h2)
    wr = wr_ref[...]
    wh, wl = _split_bf16(wr)
    logits = _dot(hh, wh) + _dot(hl, wh) + _dot(hh, wl)
    lane = lax.broadcasted_iota(jnp.int32, logits.shape, 1)
    logits = jnp.where(lane < N_EXPERTS, logits, -jnp.inf)
    e = jnp.exp(logits - jnp.max(logits, axis=-1, keepdims=True))
    he_ref[:, D_MODEL:] = e / jnp.sum(e, axis=-1, keepdims=True)
    wrt = wrt_ref[...]
    wth, wtl = _split_bf16(wrt)
    lt = _dot_nt(wth, hh) + _dot_nt(wth, hl) + _dot_nt(wtl, hh)
    et = jnp.exp(lt - jnp.max(lt, axis=0, keepdims=True))
    afft_ref[...] = et / jnp.sum(et, axis=0, keepdims=True)


def _mixer_c_call(l, x, mp, gh, hy, w_branch_bf, w_out_bf, norm2_g, wr_pad, wr_t):
    B, L, D = x.shape
    tm = 512
    nct = HY_WIDTH // LANES
    const = lambda shape: pl.BlockSpec((None,) + shape, lambda b, j: (l,) + (0,) * len(shape),
                                       pipeline_mode=pl.Buffered(1))
    return pl.pallas_call(
        _mixer_c_body,
        grid=(B, L // tm),
        in_specs=[
            pl.BlockSpec((None, tm, D), lambda b, j: (b, j, 0)),
            pl.BlockSpec((None, tm, D), lambda b, j: (b, j, 0)),
            pl.BlockSpec((None, tm, D), lambda b, j: (b, j, 0)),
            pl.BlockSpec((None, nct, tm, LANES), lambda b, j: (b, 0, j, 0)),
            pl.BlockSpec((None, None, HY_WIDTH, D), lambda b, j: (l, 1, 0, 0), pipeline_mode=pl.Buffered(1)),
            const((D, D)),
            const((1, D)),
            const((D, LANES)),
            const((N_EXPERTS, D)),
        ],
        out_specs=[
            pl.BlockSpec((None, tm, D), lambda b, j: (b, j, 0)),
            pl.BlockSpec((None, tm, D + LANES), lambda b, j: (b, j, 0)),
            pl.BlockSpec((None, N_EXPERTS, tm), lambda b, j: (b, 0, j)),
        ],
        out_shape=[
            jax.ShapeDtypeStruct((B, L, D), F32),
            jax.ShapeDtypeStruct((B, L, D + LANES), F32),
            jax.ShapeDtypeStruct((B, N_EXPERTS, L), F32),
        ],
        compiler_params=_cparams(("arbitrary", "arbitrary")),
        name="mixer_c",
    )(x, mp, gh, hy, w_branch_bf, w_out_bf, norm2_g, wr_pad, wr_t)


def _cumsum_lanes(x, tri):
    outs = []
    carry = jnp.zeros((x.shape[0], 1), F32)
    for blk in range(x.shape[1] // LANES):
        loc = _dot(x[:, blk * LANES:(blk + 1) * LANES].astype(BF16), tri) + carry
        outs.append(loc)
        carry = loc[:, LANES - 1:LANES]
    return jnp.concatenate(outs, axis=1)


def _route_body(afft_ref, tri_ref, idx_ref, rank_ref, *, cap):
    a = afft_ref[...]
    ai = lax.bitcast_convert_type(a, jnp.int32)
    ne, seq = a.shape

    def bit_step(i, v):
        cand = v | jnp.left_shift(jnp.int32(1), 30 - i)
        cnt = jnp.sum((ai >= cand).astype(F32), axis=1, keepdims=True)
        return jnp.where(cnt >= cap, cand, v)

    thr = lax.fori_loop(0, 31, bit_step, jnp.zeros((ne, 1), jnp.int32))
    gt = ai > thr
    eq = ai == thr
    need = cap - jnp.sum(gt.astype(F32), axis=1, keepdims=True)
    tri = tri_ref[...]
    cum_eq = _cumsum_lanes(eq.astype(F32), tri)
    sel = gt | (eq & (cum_eq <= need))
    rank_ref[...] = _cumsum_lanes(sel.astype(F32), tri)

    slot = lax.broadcasted_iota(jnp.int32, (cap, 1), 0).astype(F32)
    ones = jnp.ones((SUBLANES, LANES), BF16)
    chunk = 512

    def expert_step(e, carry):
        part = jnp.zeros((cap, LANES), F32)
        for c0 in range(0, seq, chunk):
            r = rank_ref[pl.ds(e, 1), c0:c0 + chunk]
            m = jnp.where(r <= slot, 1.0, 0.0)
            for b0 in range(0, chunk, LANES):
                part = part + m[:, b0:b0 + LANES]
        cnt = _dot_nt(ones, part.astype(BF16))
        idx_ref[pl.ds(e, 1), :] = cnt[0:1, :].astype(jnp.int32)
        return carry

    lax.fori_loop(0, ne, expert_step, 0)


def _route_call(afft, cap):
    B, ne, L = afft.shape
    tri = jnp.asarray(np.triu(np.ones((LANES, LANES), np.float32))).astype(BF16)
    return pl.pallas_call(
        functools.partial(_route_body, cap=cap),
        grid=(B,),
        in_specs=[
            pl.BlockSpec((None, ne, L), lambda b: (b, 0, 0)),
            pl.BlockSpec((LANES, LANES), lambda b: (0, 0)),
        ],
        out_specs=pl.BlockSpec((None, ne, cap), lambda b: (b, 0, 0)),
        out_shape=jax.ShapeDtypeStruct((B, ne, cap), jnp.int32),
        scratch_shapes=[pltpu.VMEM((ne, L), F32)],
        compiler_params=_cparams(("arbitrary",)),
        name="route",
    )(afft, tri)


def _moe_body(idx_ref, idxn_ref, he_hbm, w1_ref, w3_ref, w2_ref, y_ref, xin_ref, sem_ref, *, cap, seq, nb):
    e = pl.program_id(0)
    b = pl.program_id(1)
    step = e * nb + b
    nsteps = pl.num_programs(0) * nb
    slot = step % 2

    def row_copy(row, buf, s):
        return pltpu.make_async_copy(he_hbm.at[pl.ds(row, 1), :], xin_ref.at[buf, pl.ds(s, 1), :], sem_ref.at[buf])

    def gather(idx_smem, batch, buf):
        base = batch * seq
        for s in range(cap):
            row_copy(base + idx_smem[0, 0, s], buf, s).start()

    @pl.when(step == 0)
    def _():
        gather(idx_ref, b, 0)

    @pl.when(step + 1 < nsteps)
    def _():
        nxt = step + 1
        gather(idxn_ref, nxt % nb, 1 - slot)

    def wait_row(s, carry):
        row_copy(0, slot, 0).wait()
        return carry

    lax.fori_loop(0, cap, wait_row, 0)

    xin = xin_ref[slot]
    xb = xin[:, :D_MODEL].astype(BF16)
    aff = xin[:, D_MODEL:]
    lane = lax.broadcasted_iota(jnp.int32, aff.shape, 1)
    gate = jnp.sum(jnp.where(lane == e, aff, 0.0), axis=1, keepdims=True)
    a1 = _dot(xb, w1_ref[...])
    a3 = _dot(xb, w3_ref[...])
    hid = (a1 * jax.nn.sigmoid(a1) * a3).astype(BF16)
    y_ref[...] = _dot(hid, w2_ref[...]) * gate


def _moe_call(l, idx, he, w1_bf, w3_bf, w2_bf):
    B, ne, cap = idx.shape
    seq = he.shape[1]
    dh = he.shape[2]
    ff = w1_bf.shape[-1]
    idx3 = idx.reshape(B * ne, 1, cap)
    he_flat = he.reshape(B * seq, dh)

    def cur(e, b):
        return (b * ne + e, 0, 0)

    def nxt(e, b):
        step = jnp.minimum(e * B + b + 1, ne * B - 1)
        return ((step % B) * ne + step // B, 0, 0)

    return pl.pallas_call(
        functools.partial(_moe_body, cap=cap, seq=seq, nb=B),
        grid=(ne, B),
        in_specs=[
            pl.BlockSpec((1, 1, cap), cur, memory_space=pltpu.SMEM),
            pl.BlockSpec((1, 1, cap), nxt, memory_space=pltpu.SMEM),
            pl.BlockSpec(memory_space=pl.ANY),
            pl.BlockSpec((None, None, D_MODEL, ff), lambda e, b: (l, e, 0, 0)),
            pl.BlockSpec((None, None, D_MODEL, ff), lambda e, b: (l, e, 0, 0)),
            pl.BlockSpec((None, None, ff, D_MODEL), lambda e, b: (l, e, 0, 0)),
        ],
        out_specs=pl.BlockSpec((None, None, cap, D_MODEL), lambda e, b: (b, e, 0, 0)),
        out_shape=jax.ShapeDtypeStruct((B, ne, cap, D_MODEL), F32),
        scratch_shapes=[
            pltpu.VMEM((2, cap, dh), F32),
            pltpu.SemaphoreType.DMA((2,)),
        ],
        compiler_params=_cparams(("arbitrary", "arbitrary")),
        name="moe",
    )(idx3, idx3, he_flat, w1_bf, w3_bf, w2_bf)


def _combine_body(idx_ref, x1_hbm, y_ref, o_ref, sem_ref, *, cap, group):
    b = pl.program_id(0)
    e = pl.program_id(1)

    @pl.when(e == 0)
    def _():
        cp = pltpu.make_async_copy(x1_hbm.at[pl.ds(b, 1)], o_ref, sem_ref.at[0])
        cp.start()
        cp.wait()

    def rows_step(i, carry):
        base = i * group
        toks = [idx_ref[0, 0, base + k] for k in range(group)]
        vals = [o_ref[0, pl.ds(toks[k], 1), :] + y_ref[pl.ds(base + k, 1), :] for k in range(group)]
        for k in range(group):
            o_ref[0, pl.ds(toks[k], 1), :] = vals[k]
        return carry

    lax.fori_loop(0, cap // group, rows_step, 0)


def _combine_call(idx, x1, y):
    B, ne, cap = idx.shape
    _, L, D = x1.shape
    idx3 = idx.reshape(B * ne, 1, cap)
    return pl.pallas_call(
        functools.partial(_combine_body, cap=cap, group=SUBLANES),
        grid=(B, ne),
        in_specs=[
            pl.BlockSpec((1, 1, cap), lambda b, e: (b * ne + e, 0, 0), memory_space=pltpu.SMEM),
            pl.BlockSpec(memory_space=pl.ANY),
            pl.BlockSpec((None, None, cap, D), lambda b, e: (b, e, 0, 0)),
        ],
        out_specs=pl.BlockSpec((1, L, D), lambda b, e: (b, 0, 0)),
        out_shape=jax.ShapeDtypeStruct((B, L, D), F32),
        scratch_shapes=[pltpu.SemaphoreType.DMA((1,))],
        compiler_params=_cparams(("arbitrary", "arbitrary")),
        name="combine",
    )(idx3, x1, y)


def _final_body(x_ref, g_ref, o_ref):
    o_ref[...] = _rms(x_ref[...], g_ref[...])


def _final_call(x, g):
    B, L, D = x.shape
    tm = 1024
    return pl.pallas_call(
        _final_body,
        grid=(B, L // tm),
        in_specs=[pl.BlockSpec((None, tm, D), lambda b, j: (b, j, 0)), pl.BlockSpec((1, D), lambda b, j: (0, 0))],
        out_specs=pl.BlockSpec((None, tm, D), lambda b, j: (b, j, 0)),
        out_shape=jax.ShapeDtypeStruct((B, L, D), F32),
        compiler_params=_cparams(("arbitrary", "arbitrary")),
        name="final_norm",
    )(x, g.reshape(1, D))


def kernel(x, mem, norm1_g, w_in, gate_b, sgu_ln_g, sgu_ln_b, sgu_w, sgu_b, hy_conv_w, hy_conv_b, hy_f_w1, hy_f_b1, hy_f_w2, hy_f_b2, hy_f_w3, hy_f_b3, hy_f_w4, hy_f_freq, hy_bias, mem_norm_g, w_kv, w_branch, w_out, norm2_g, w_router, w1, w3, w2, final_g):
    B, L, D = x.shape
    depth = w_in.shape[0]
    cap = EC_CAPACITY * L // N_EXPERTS
    nct = HY_WIDTH // LANES
    row = lambda a: a.reshape(depth, 1, a.shape[-1])

    w_in_bf = w_in.astype(BF16)
    w_kv_bf = w_kv.astype(BF16)
    w_branch_bf = w_branch.astype(BF16)
    w_out_bf = w_out.astype(BF16)
    sgu_w_bf = sgu_w.astype(BF16)
    w1_bf, w3_bf, w2_bf = w1.astype(BF16), w3.astype(BF16), w2.astype(BF16)
    sgu_bt = jnp.repeat(jnp.swapaxes(sgu_b, 1, 2), SGU_WIDTH // SGU_GROUPS, axis=2)
    wr_pad = jnp.pad(w_router, ((0, 0), (0, 0), (0, LANES - N_EXPERTS)))
    wr_t = jnp.swapaxes(w_router, 1, 2)
    hy_bias4 = hy_bias.reshape(depth, nct, 1, LANES)

    k_all, v_all = _kv_call(mem, mem_norm_g, w_kv_bf)
    filt = _filter_call(L, hy_f_w1, hy_f_b1, hy_f_w2, hy_f_b2, hy_f_w3, hy_f_b3, hy_f_w4, hy_f_freq)
    consts = _fft_constants(L)

    for l in range(depth):
        mp, gh, x0, u = _mixer_a_call(l, x, row(norm1_g), w_in_bf, row(gate_b), row(sgu_ln_g), row(sgu_ln_b),
                                      sgu_w_bf, sgu_bt, hy_conv_w, row(hy_conv_b), k_all, v_all, w_branch_bf)
        hy = _hyena_call(l, filt, u, x0, hy_bias4, consts)
        x1, he, afft = _mixer_c_call(l, x, mp, gh, hy, w_branch_bf, w_out_bf, row(norm2_g), wr_pad, wr_t)
        idx = _route_call(afft, cap)
        y = _moe_call(l, idx, he, w1_bf, w3_bf, w2_bf)
        x = _combine_call(idx, x1, y)
    return _final_call(x, final_g)
```

```python
import functools
import math

import numpy as np
import jax
import jax.numpy as jnp
from jax import lax
from jax.experimental import pallas as pl
from jax.experimental.pallas import tpu as pltpu

F32 = jnp.float32
BF16 = jnp.bfloat16

D_MODEL = 1024
SGU_WIDTH = 512
SGU_GROUPS = 4
SGU_CHUNK = 128
HY_WIDTH = 512
HY_EMB_DIM = 33
HY_BANDS = (HY_EMB_DIM - 1) // 2
HY_FILTER_HIDDEN = 64
HY_FAST_DECAY_PCT = 0.3
HY_SLOW_DECAY_PCT = 1.5
HY_DECAY_TARGET = 1e-2
XA_HEADS = 4
XA_HEAD_DIM = 128
XA_WIDTH = XA_HEADS * XA_HEAD_DIM
N_EXPERTS = 16
EC_CAPACITY = 2
NORM_EPS = 1e-6
LN_EPS = 1e-5

C_SGU = 2 * SGU_WIDTH
C_HY = C_SGU + 3 * HY_WIDTH
C_Q = C_HY + XA_WIDTH

LANES = 128
SUBLANES = 8
VMEM_LIMIT = 56 * 1024 * 1024

FFT_N2 = 128
FFT_ROWS = 256


def _cparams(semantics):
    return pltpu.CompilerParams(dimension_semantics=semantics, vmem_limit_bytes=VMEM_LIMIT)


def _rms(x, g):
    return x * lax.rsqrt(jnp.mean(x * x, axis=-1, keepdims=True) + NORM_EPS) * g


def _dot(a, b):
    return jnp.dot(a, b, preferred_element_type=F32)


def _dot_nt(a, b):
    return lax.dot_general(a, b, (((1,), (1,)), ((), ())), preferred_element_type=F32)


def _split_bf16(x):
    hi = x.astype(BF16)
    lo = (x - hi.astype(F32)).astype(BF16)
    return hi, lo


def _kv_body(mem_ref, g_ref, w_ref, k_ref, v_ref):
    m = _rms(mem_ref[...], g_ref[...]).astype(BF16)
    kv = _dot(m, w_ref[...])
    k_ref[...] = kv[:, :XA_WIDTH].astype(BF16)
    v_ref[...] = kv[:, XA_WIDTH:].astype(BF16)


def _kv_call(mem, mem_norm_g, w_kv_bf):
    B, M, D = mem.shape
    depth = w_kv_bf.shape[0]
    return pl.pallas_call(
        _kv_body,
        grid=(depth, B),
        in_specs=[
            pl.BlockSpec((None, M, D), lambda l, b: (b, 0, 0)),
            pl.BlockSpec((None, 1, D), lambda l, b: (l, 0, 0)),
            pl.BlockSpec((None, D, 2 * XA_WIDTH), lambda l, b: (l, 0, 0)),
        ],
        out_specs=[
            pl.BlockSpec((None, None, M, XA_WIDTH), lambda l, b: (l, b, 0, 0)),
            pl.BlockSpec((None, None, M, XA_WIDTH), lambda l, b: (l, b, 0, 0)),
        ],
        out_shape=[jax.ShapeDtypeStruct((depth, B, M, XA_WIDTH), BF16)] * 2,
        compiler_params=_cparams(("arbitrary", "arbitrary")),
        name="kv",
    )(mem, mem_norm_g.reshape(depth, 1, D), w_kv_bf)


def _filter_body(w1t_ref, w1c_ref, w1s_ref, b1_ref, w2_ref, b2_ref, w3_ref, b3_ref, w4_ref, fr_ref,
                 bands_ref, deltas_ref, o_ref, *, seq_len, rows):
    d = pl.program_id(1)
    r0 = pl.program_id(2) * rows
    ridx = r0 + lax.broadcasted_iota(jnp.int32, (rows, 1), 0)
    pos = (ridx + d * (seq_len - 2 * ridx)).astype(F32)
    t = pos * (1.0 / (seq_len - 1))
    w = pos * (2.0 * math.pi / seq_len)
    ang = w * bands_ref[...]
    hp = lax.Precision.HIGHEST
    fr = fr_ref[...]
    pre = (t * w1t_ref[...] + jnp.dot(jnp.cos(ang), w1c_ref[...], precision=hp, preferred_element_type=F32)
           - jnp.dot(jnp.sin(ang), w1s_ref[...], precision=hp, preferred_element_type=F32) + b1_ref[...])
    h = jnp.sin(fr * pre)
    h = jnp.sin(fr * (jnp.dot(h, w2_ref[...], precision=hp, preferred_element_type=F32) + b2_ref[...]))
    h = jnp.sin(fr * (jnp.dot(h, w3_ref[...], precision=hp, preferred_element_type=F32) + b3_ref[...]))
    h = jnp.dot(h, w4_ref[...], precision=hp, preferred_element_type=F32)
    h = h * jnp.exp(-t * deltas_ref[...])
    h = jnp.where((ridx == 0) & (d == 1), 0.0, h)
    for c in range(HY_WIDTH // LANES):
        o_ref[c] = h[:, c * LANES:(c + 1) * LANES]


def _filter_call(seq_len, w1, b1, w2, b2, w3, b3, w4, freq):
    depth = w1.shape[0]
    H = HY_FILTER_HIDDEN
    rows = 1024
    nct = HY_WIDTH // LANES
    bands = np.zeros((1, LANES), np.float32)
    bands[0, :HY_BANDS] = np.linspace(1e-4, HY_BANDS - 1, HY_BANDS)
    max_decay = math.log(HY_DECAY_TARGET) / HY_FAST_DECAY_PCT
    min_decay = math.log(HY_DECAY_TARGET) / HY_SLOW_DECAY_PCT
    deltas = np.abs(np.linspace(min_decay, max_decay, HY_WIDTH)).astype(np.float32).reshape(1, HY_WIDTH)
    pad = ((0, 0), (0, LANES - HY_BANDS), (0, 0))
    w1t = w1[:, 0:1, :]
    w1c = jnp.pad(w1[:, 1:1 + HY_BANDS, :], pad)
    w1s = jnp.pad(w1[:, 1 + HY_BANDS:, :], pad)
    vec = lambda n: pl.BlockSpec((None, 1, n), lambda l, d, r: (l, 0, 0))
    mat = lambda m, n: pl.BlockSpec((None, m, n), lambda l, d, r: (l, 0, 0))
    return pl.pallas_call(
        functools.partial(_filter_body, seq_len=seq_len, rows=rows),
        grid=(depth, 2, seq_len // rows),
        in_specs=[
            vec(H), mat(LANES, H), mat(LANES, H), vec(H), mat(H, H), vec(H), mat(H, H), vec(H),
            pl.BlockSpec((None, H, HY_WIDTH), lambda l, d, r: (l, 0, d)),
            vec(H),
            pl.BlockSpec((1, LANES), lambda l, d, r: (0, 0)),
            pl.BlockSpec((1, HY_WIDTH), lambda l, d, r: (0, 0)),
        ],
        out_specs=pl.BlockSpec((None, None, nct, rows, LANES), lambda l, d, r: (l, d, 0, r, 0)),
        out_shape=jax.ShapeDtypeStruct((depth, 2, nct, seq_len, LANES), F32),
        compiler_params=_cparams(("arbitrary", "arbitrary", "arbitrary")),
        name="hyena_filter",
    )(w1t, w1c, w1s, b1.reshape(depth, 1, H), w2, b2.reshape(depth, 1, H), w3, b3.reshape(depth, 1, H),
      w4, freq.reshape(depth, 1, H), jnp.asarray(bands), jnp.asarray(deltas))


def _mixer_a_body(*refs, tm, n_x):
    x_refs, refs = refs[:3 * n_x], refs[3 * n_x:]
    (g1_ref, win_ref, gb_ref, lng_ref, lnb_ref, sw_ref, sbt_ref, cw_ref, cb_ref, k_ref, v_ref, wba_ref, wbc_ref,
     mp_ref, gh_ref, x0_ref, u_ref) = refs
    x_cur = sum(r[...] for r in x_refs[0::3])
    x_prev = sum(r[...] for r in x_refs[1::3])
    x_next = sum(r[...] for r in x_refs[2::3])
    j = pl.program_id(1)
    nj = pl.num_programs(1)
    g1 = g1_ref[...]
    h = _rms(x_cur, g1)
    hb = h.astype(BF16)

    zs = _dot(hb, win_ref[:, 0:C_SGU])
    zs = 0.5 * zs * (1.0 + lax.erf(zs * (1.0 / math.sqrt(2.0))))
    us = zs[:, :SGU_WIDTH]
    vs = zs[:, SGU_WIDTH:]
    mu = jnp.mean(vs, axis=-1, keepdims=True)
    vc = vs - mu
    var = jnp.mean(vc * vc, axis=-1, keepdims=True)
    vn = (vc * lax.rsqrt(var + LN_EPS) * lng_ref[...] + lnb_ref[...]).astype(BF16)
    gd = SGU_WIDTH // SGU_GROUPS
    chunks = []
    for c in range(tm // SGU_CHUNK):
        groups = []
        for g in range(SGU_GROUPS):
            blk = vn[c * SGU_CHUNK:(c + 1) * SGU_CHUNK, g * gd:(g + 1) * gd]
            groups.append(_dot(sw_ref[g], blk))
        chunks.append(jnp.concatenate(groups, axis=1) + sbt_ref[...])
    a = us * jnp.concatenate(chunks, axis=0)

    q = _dot(hb, win_ref[:, C_HY:C_Q])
    scale = 1.0 / math.sqrt(XA_HEAD_DIM)
    heads = []
    for hd in range(XA_HEADS):
        sl = slice(hd * XA_HEAD_DIM, (hd + 1) * XA_HEAD_DIM)
        s = _dot_nt(q[:, sl].astype(BF16), k_ref[:, sl]) * scale
        s = s - jnp.max(s, axis=-1, keepdims=True)
        p = jnp.exp(s)
        p = p / jnp.sum(p, axis=-1, keepdims=True)
        heads.append(_dot(p.astype(BF16), v_ref[:, sl]))
    c_att = jnp.concatenate(heads, axis=1)

    gates = jax.nn.sigmoid(_dot(hb, win_ref[:, C_Q:]) + gb_ref[...])
    pa = _dot(a.astype(BF16), wba_ref[...])
    pc = _dot(c_att.astype(BF16), wbc_ref[...])
    mp_ref[...] = gates[:, 0:D_MODEL] * pa + gates[:, 2 * D_MODEL:] * pc
    gh_ref[...] = gates[:, D_MODEL:2 * D_MODEL].astype(BF16)

    hprev = _rms(x_prev, g1).astype(BF16)
    hnext = _rms(x_next, g1).astype(BF16)
    hext = jnp.concatenate([hprev, hb, hnext], axis=0)
    pe = _dot(hext, win_ref[:, C_SGU:C_HY])
    rows = lax.broadcasted_iota(jnp.int32, (tm, 1), 0)
    pm1 = jnp.where((rows == 0) & (j == 0), 0.0, pe[SUBLANES - 1:SUBLANES - 1 + tm])
    pp1 = jnp.where((rows == tm - 1) & (j == nj - 1), 0.0, pe[SUBLANES + 1:SUBLANES + 1 + tm])
    p0 = pe[SUBLANES:SUBLANES + tm]
    pconv = cw_ref[0:1, :] * pm1 + cw_ref[1:2, :] * p0 + cw_ref[2:3, :] * pp1 + cb_ref[...]
    x0 = pconv[:, :HY_WIDTH]
    u = pconv[:, HY_WIDTH:2 * HY_WIDTH] * pconv[:, 2 * HY_WIDTH:]
    for c in range(HY_WIDTH // LANES):
        x0_ref[c] = x0[:, c * LANES:(c + 1) * LANES]
        u_ref[c] = u[:, c * LANES:(c + 1) * LANES]


def _mixer_a_call(l, xs, norm1_g, w_in_bf, gate_b, sgu_ln_g, sgu_ln_b, sgu_w_bf, sgu_bt, conv_w, conv_b,
                  k_all, v_all, w_branch_bf):
    B, L, D = xs[0].shape
    tm = 256
    nct = HY_WIDTH // LANES
    in_cols = w_in_bf.shape[-1]
    hb_per_tile = tm // SUBLANES
    n_hb = L // SUBLANES
    const = lambda shape: pl.BlockSpec((None,) + shape, lambda b, j: (l,) + (0,) * len(shape),
                                       pipeline_mode=pl.Buffered(1))
    x_specs = [
        pl.BlockSpec((None, tm, D), lambda b, j: (b, j, 0)),
        pl.BlockSpec((None, SUBLANES, D), lambda b, j: (b, jnp.maximum(j * hb_per_tile - 1, 0), 0)),
        pl.BlockSpec((None, SUBLANES, D), lambda b, j: (b, jnp.minimum((j + 1) * hb_per_tile, n_hb - 1), 0)),
    ]
    x_args = [a for x in xs for a in (x, x, x)]
    return pl.pallas_call(
        functools.partial(_mixer_a_body, tm=tm, n_x=len(xs)),
        grid=(B, L // tm),
        in_specs=x_specs * len(xs) + [
            const((1, D)),
            const((D, in_cols)),
            const((1, 3 * D)),
            const((1, SGU_WIDTH)),
            const((1, SGU_WIDTH)),
            const((SGU_GROUPS, SGU_CHUNK, SGU_CHUNK)),
            const((SGU_CHUNK, SGU_WIDTH)),
            const((3, 3 * HY_WIDTH)),
            const((1, 3 * HY_WIDTH)),
            pl.BlockSpec((None, None, k_all.shape[2], XA_WIDTH), lambda b, j: (l, b, 0, 0)),
            pl.BlockSpec((None, None, k_all.shape[2], XA_WIDTH), lambda b, j: (l, b, 0, 0)),
            pl.BlockSpec((None, None, SGU_WIDTH, D), lambda b, j: (l, 0, 0, 0), pipeline_mode=pl.Buffered(1)),
            pl.BlockSpec((None, None, XA_WIDTH, D), lambda b, j: (l, 2, 0, 0), pipeline_mode=pl.Buffered(1)),
        ],
        out_specs=[
            pl.BlockSpec((None, tm, D), lambda b, j: (b, j, 0)),
            pl.BlockSpec((None, tm, D), lambda b, j: (b, j, 0)),
            pl.BlockSpec((None, nct, tm, LANES), lambda b, j: (b, 0, j, 0)),
            pl.BlockSpec((None, nct, tm, LANES), lambda b, j: (b, 0, j, 0)),
        ],
        out_shape=[
            jax.ShapeDtypeStruct((B, L, D), F32),
            jax.ShapeDtypeStruct((B, L, D), BF16),
            jax.ShapeDtypeStruct((B, nct, L, LANES), F32),
            jax.ShapeDtypeStruct((B, nct, L, LANES), F32),
        ],
        compiler_params=_cparams(("arbitrary", "arbitrary")),
        name="mixer_a",
    )(*x_args, norm1_g, w_in_bf, gate_b, sgu_ln_g, sgu_ln_b, sgu_w_bf, sgu_bt, conv_w, conv_b,
      k_all, v_all, w_branch_bf, w_branch_bf)


def _fft_constants(seq_len):
    n = 2 * seq_len
    n2 = FFT_N2
    n1 = n // n2
    nz = seq_len // n2
    lo = FFT_ROWS // nz
    tiles = n2 // lo
    k1n = n1 // 2 + 1
    assert lo == SUBLANES and nz * lo == FFT_ROWS
    k1 = np.arange(k1n).reshape(k1n, 1, 1, 1, 1).astype(np.float64)
    part = np.arange(2).reshape(1, 2, 1, 1, 1)
    a = np.arange(lo).reshape(1, 1, lo, 1, 1)
    m1 = np.arange(nz).reshape(1, 1, 1, nz, 1).astype(np.float64)
    a2 = np.arange(lo).reshape(1, 1, 1, 1, lo)
    fwd = np.zeros((tiles, k1n, 2, lo, nz, lo), np.float64)
    for j in range(tiles):
        theta = 2.0 * np.pi * (m1 * k1 / n1 + (lo * j + a) * k1 / n)
        val = np.where(part == 0, np.cos(theta), -np.sin(theta))
        fwd[j] = val * (a == a2)
    fwd = fwd.reshape(tiles, k1n * 2 * lo, nz * lo)
    weight = np.where((np.arange(k1n) == 0) | (np.arange(k1n) == n1 // 2), 1.0, 2.0) / n
    inv = fwd.reshape(tiles, k1n, 2 * lo, nz * lo) * weight.reshape(1, k1n, 1, 1)
    inv = inv.reshape(tiles, k1n * 2 * lo, nz * lo).transpose(0, 2, 1)
    phi = 2.0 * np.pi * np.outer(np.arange(n2), np.arange(n2)) / n2
    c, s = np.cos(phi), np.sin(phi)
    g_fwd = np.block([[c, s], [-s, c]])
    g_inv = np.block([[c, -s], [s, c]])
    f32 = lambda v: jnp.asarray(v.astype(np.float32)).astype(BF16)
    return f32(fwd), f32(inv), f32(g_fwd), f32(g_inv), (tiles, k1n, lo, nz)


def _hyena_body(filt_ref, u_ref, x0_ref, bias_ref, fa_ref, ia_ref, gf_ref, gi_ref, o_ref, z_ref, k_ref,
                *, tiles, k1n, lo, nz):
    s = pl.program_id(1)
    half = FFT_N2
    rows = nz * lo

    def stage_a(src_ref):
        for j in range(tiles):
            tile = src_ref[:, j].reshape(rows, LANES).astype(BF16)
            z_ref[:, :, j] = _dot(fa_ref[j], tile).reshape(k1n, 2, lo, LANES)

    def spectrum(k1):
        return _dot(gf_ref[...], z_ref[k1].reshape(2 * half, LANES).astype(BF16))

    @pl.when(s == 0)
    def _():
        k_ref[...] = jnp.zeros(k_ref.shape, F32)

    @pl.when(s < 2)
    def _():
        stage_a(filt_ref)
        odd_sign = jnp.where(s == 0, 1.0, -1.0).astype(F32)
        for k1 in range(k1n):
            x = spectrum(k1)
            k_ref[k1] += x if k1 % 2 == 0 else odd_sign * x

    @pl.when(s >= 2)
    def _():
        stage_a(u_ref)
        for k1 in range(k1n):
            x = spectrum(k1)
            xr, xi = x[:half], x[half:]
            kr, ki = k_ref[k1, :half], k_ref[k1, half:]
            y = jnp.concatenate([xr * kr - xi * ki, xr * ki + xi * kr], axis=0).astype(BF16)
            z_ref[k1] = _dot(gi_ref[...], y).reshape(2, tiles, lo, LANES)
        bias = bias_ref[...]
        for j in range(tiles):
            zt = z_ref[:, :, j].reshape(k1n * 2 * lo, LANES).astype(BF16)
            y = _dot(ia_ref[j], zt)
            uu = u_ref[:, j].reshape(rows, LANES)
            x0 = x0_ref[:, j].reshape(rows, LANES)
            o_ref[:, j] = (x0 * (y + bias * uu)).reshape(nz, lo, LANES)


def _hyena_call(l, filt, u, x0, hy_bias, consts):
    fa, ia, gf, gi, (tiles, k1n, lo, nz) = consts
    B, nct, L, _ = u.shape
    view = lambda a: a.reshape(a.shape[:-2] + (nz, tiles, lo, LANES))
    seq = (nz, tiles, lo, LANES)
    bidx = lambda s: jnp.maximum(s - 2, 0)
    single = lambda shape: pl.BlockSpec(shape, lambda c, s: (0,) * len(shape), pipeline_mode=pl.Buffered(1))
    out = pl.pallas_call(
        functools.partial(_hyena_body, tiles=tiles, k1n=k1n, lo=lo, nz=nz),
        grid=(nct, B + 2),
        in_specs=[
            pl.BlockSpec((None, None, None) + seq, lambda c, s: (l, jnp.minimum(s, 1), c, 0, 0, 0, 0)),
            pl.BlockSpec((None, None) + seq, lambda c, s: (bidx(s), c, 0, 0, 0, 0)),
            pl.BlockSpec((None, None) + seq, lambda c, s: (bidx(s), c, 0, 0, 0, 0)),
            pl.BlockSpec((None, None, 1, LANES), lambda c, s: (l, c, 0, 0)),
            single(fa.shape), single(ia.shape), single(gf.shape), single(gi.shape),
        ],
        out_specs=pl.BlockSpec((None, None) + seq, lambda c, s: (bidx(s), c, 0, 0, 0, 0)),
        out_shape=jax.ShapeDtypeStruct((B, nct) + seq, F32),
        scratch_shapes=[
            pltpu.VMEM((k1n, 2, tiles, lo, LANES), F32),
            pltpu.VMEM((k1n, 2 * FFT_N2, LANES), F32),
        ],
        compiler_params=_cparams(("arbitrary", "arbitrary")),
        name="hyena_fft",
    )(view(filt), view(u), view(x0), hy_bias, fa, ia, gf, gi)
    return out.reshape(B, nct, L, LANES)


def _pack_bf16_pairs(x):
    n = x.shape[1] // 2
    return pltpu.pack_elementwise([x[:, :n], x[:, n:]], packed_dtype=BF16)


def _unpack_bf16_pairs(p):
    lo = pltpu.unpack_elementwise(p, index=0, packed_dtype=BF16, unpacked_dtype=F32)
    hi = pltpu.unpack_elementwise(p, index=1, packed_dtype=BF16, unpacked_dtype=F32)
    return jnp.concatenate([lo, hi], axis=1).astype(BF16)


def _mixer_c_body(*refs, n_x):
    x_refs, refs = refs[:n_x], refs[n_x:]
    (mp_ref, gh_ref, hy_ref, wbh_ref, wo_ref, g2_ref, wr_ref, wrt_ref, x1_ref, hp_ref, aff_ref, afft_ref) = refs
    hy = jnp.concatenate([hy_ref[c] for c in range(HY_WIDTH // LANES)], axis=1).astype(BF16)
    merged = mp_ref[...] + gh_ref[...].astype(F32) * _dot(hy, wbh_ref[...])
    x1 = sum(r[...] for r in x_refs) + _dot(merged.astype(BF16), wo_ref[...])
    x1_ref[...] = x1
    h2 = _rms(x1, g2_ref[...])
    hp_ref[...] = _pack_bf16_pairs(h2)
    hh, hl = _split_bf16(h2)
    wr = wr_ref[...]
    wh, wl = _split_bf16(wr)
    logits = _dot(hh, wh) + _dot(hl, wh) + _dot(hh, wl)
    lane = lax.broadcasted_iota(jnp.int32, logits.shape, 1)
    logits = jnp.where(lane < N_EXPERTS, logits, -jnp.inf)
    e = jnp.exp(logits - jnp.max(logits, axis=-1, keepdims=True))
    aff_ref[...] = e / jnp.sum(e, axis=-1, keepdims=True)
    wrt = wrt_ref[...]
    wth, wtl = _split_bf16(wrt)
    lt = _dot_nt(wth, hh) + _dot_nt(wth, hl) + _dot_nt(wtl, hh)
    et = jnp.exp(lt - jnp.max(lt, axis=0, keepdims=True))
    afft_ref[...] = et / jnp.sum(et, axis=0, keepdims=True)


def _mixer_c_call(l, xs, mp, gh, hy, w_branch_bf, w_out_bf, norm2_g, wr_pad, wr_t):
    B, L, D = xs[0].shape
    tm = 512
    nct = HY_WIDTH // LANES
    const = lambda shape: pl.BlockSpec((None,) + shape, lambda b, j: (l,) + (0,) * len(shape),
                                       pipeline_mode=pl.Buffered(1))
    return pl.pallas_call(
        functools.partial(_mixer_c_body, n_x=len(xs)),
        grid=(B, L // tm),
        in_specs=[pl.BlockSpec((None, tm, D), lambda b, j: (b, j, 0))] * len(xs) + [
            pl.BlockSpec((None, tm, D), lambda b, j: (b, j, 0)),
            pl.BlockSpec((None, tm, D), lambda b, j: (b, j, 0)),
            pl.BlockSpec((None, nct, tm, LANES), lambda b, j: (b, 0, j, 0)),
            pl.BlockSpec((None, None, HY_WIDTH, D), lambda b, j: (l, 1, 0, 0), pipeline_mode=pl.Buffered(1)),
            const((D, D)),
            const((1, D)),
            const((D, LANES)),
            const((N_EXPERTS, D)),
        ],
        out_specs=[
            pl.BlockSpec((None, tm, D), lambda b, j: (b, j, 0)),
            pl.BlockSpec((None, tm, D // 2), lambda b, j: (b, j, 0)),
            pl.BlockSpec((None, tm, LANES), lambda b, j: (b, j, 0)),
            pl.BlockSpec((None, N_EXPERTS, tm), lambda b, j: (b, 0, j)),
        ],
        out_shape=[
            jax.ShapeDtypeStruct((B, L, D), F32),
            jax.ShapeDtypeStruct((B, L, D // 2), jnp.uint32),
            jax.ShapeDtypeStruct((B, L, LANES), F32),
            jax.ShapeDtypeStruct((B, N_EXPERTS, L), F32),
        ],
        compiler_params=_cparams(("arbitrary", "arbitrary")),
        name="mixer_c",
    )(*xs, mp, gh, hy, w_branch_bf, w_out_bf, norm2_g, wr_pad, wr_t)


def _cumsum_lanes(x, tri):
    outs = []
    carry = jnp.zeros((x.shape[0], 1), F32)
    for blk in range(x.shape[1] // LANES):
        loc = _dot(x[:, blk * LANES:(blk + 1) * LANES].astype(BF16), tri) + carry
        outs.append(loc)
        carry = loc[:, LANES - 1:LANES]
    return jnp.concatenate(outs, axis=1)


def _route_body(afft_ref, tri_ref, idx_ref, rank_ref, *, cap):
    a = afft_ref[...]
    ne, seq = a.shape

    def bit_step(i, v):
        cand = v | jnp.left_shift(jnp.int32(1), 30 - i)
        cnt = jnp.sum((a >= lax.bitcast_convert_type(cand, F32)).astype(F32), axis=1, keepdims=True)
        return jnp.where(cnt >= cap, cand, v)

    thr = lax.bitcast_convert_type(lax.fori_loop(0, 31, bit_step, jnp.zeros((ne, 1), jnp.int32)), F32)
    gt = a > thr
    eq = a == thr
    need = cap - jnp.sum(gt.astype(F32), axis=1, keepdims=True)
    tri = tri_ref[...]
    cum_eq = _cumsum_lanes(eq.astype(F32), tri)
    sel = gt | (eq & (cum_eq <= need))
    rank_ref[...] = _cumsum_lanes(sel.astype(F32), tri)

    slot = lax.broadcasted_iota(jnp.int32, (cap, 1), 0).astype(F32)
    ones = jnp.ones((SUBLANES, LANES), BF16)
    chunk = 512

    def expert_step(e, carry):
        part = jnp.zeros((cap, LANES), F32)
        for c0 in range(0, seq, chunk):
            r = rank_ref[pl.ds(e, 1), c0:c0 + chunk]
            m = jnp.where(r <= slot, 1.0, 0.0)
            for b0 in range(0, chunk, LANES):
                part = part + m[:, b0:b0 + LANES]
        cnt = _dot_nt(ones, part.astype(BF16))
        idx_ref[pl.ds(e, 1), :] = cnt[0:1, :].astype(jnp.int32)
        return carry

    lax.fori_loop(0, ne, expert_step, 0)


def _route_call(afft, cap):
    B, ne, L = afft.shape
    tri = jnp.asarray(np.triu(np.ones((LANES, LANES), np.float32))).astype(BF16)
    return pl.pallas_call(
        functools.partial(_route_body, cap=cap),
        grid=(B,),
        in_specs=[
            pl.BlockSpec((None, ne, L), lambda b: (b, 0, 0)),
            pl.BlockSpec((LANES, LANES), lambda b: (0, 0)),
        ],
        out_specs=pl.BlockSpec((None, ne, cap), lambda b: (b, 0, 0)),
        out_shape=jax.ShapeDtypeStruct((B, ne, cap), jnp.int32),
        scratch_shapes=[pltpu.VMEM((ne, L), F32)],
        compiler_params=_cparams(("arbitrary",)),
        name="route",
    )(afft, tri)


def _moe_body(idx_ref, idxp_ref, hp_ref, aff_ref, w1_ref, w3_ref, w2_ref, out_hbm,
              acc_ref, xin_ref, gsel_ref, y_ref, sem_ref, *, cap, ne, npairs):
    i = pl.program_id(0)
    slot = i % 2
    e = jnp.minimum(i, npairs - 1) % ne

    @pl.when(i == 0)
    def _():
        acc_ref[...] = jnp.zeros(acc_ref.shape, F32)
        y_ref[...] = jnp.zeros(y_ref.shape, F32)

    y_prev = y_ref.at[1 - slot]
    for g0 in range(0, cap, SUBLANES):
        toks = [idxp_ref[0, 0, g0 + k] for k in range(SUBLANES)]
        vals = [acc_ref[pl.ds(toks[k], 1), :] + y_prev[pl.ds(g0 + k, 1), :] for k in range(SUBLANES)]
        for k in range(SUBLANES):
            acc_ref[pl.ds(toks[k], 1), :] = vals[k]

    for s in range(cap):
        t = idx_ref[0, 0, s]
        xin_ref[pl.ds(s, 1), :] = hp_ref[pl.ds(t, 1), :]
        gsel_ref[pl.ds(s, 1), :] = aff_ref[pl.ds(t, 1), :]

    xb = _unpack_bf16_pairs(xin_ref[...])
    aff = gsel_ref[...]
    lane = lax.broadcasted_iota(jnp.int32, aff.shape, 1)
    gate = jnp.sum(jnp.where(lane == e, aff, 0.0), axis=1, keepdims=True)
    a1 = _dot(xb, w1_ref[...])
    a3 = _dot(xb, w3_ref[...])
    hid = (a1 * jax.nn.sigmoid(a1) * a3).astype(BF16)
    y_ref[slot] = _dot(hid, w2_ref[...]) * gate

    @pl.when((i > 0) & (i % ne == 0))
    def _():
        cp = pltpu.make_async_copy(acc_ref, out_hbm.at[i // ne - 1], sem_ref.at[0])
        cp.start()
        cp.wait()
        acc_ref[...] = jnp.zeros(acc_ref.shape, F32)


def _moe_call(l, idx, hp, aff, w1_bf, w3_bf, w2_bf):
    B, ne, cap = idx.shape
    L = hp.shape[1]
    ff = w1_bf.shape[-1]
    npairs = B * ne
    idx3 = idx.reshape(npairs, 1, cap)
    pair = lambda i: jnp.minimum(i, npairs - 1)
    prev = lambda i: jnp.maximum(i - 1, 0)
    return pl.pallas_call(
        functools.partial(_moe_body, cap=cap, ne=ne, npairs=npairs),
        grid=(npairs + 1,),
        in_specs=[
            pl.BlockSpec((1, 1, cap), lambda i: (pair(i), 0, 0), memory_space=pltpu.SMEM),
            pl.BlockSpec((1, 1, cap), lambda i: (prev(i), 0, 0), memory_space=pltpu.SMEM),
            pl.BlockSpec((None, L, D_MODEL // 2), lambda i: (pair(i) // ne, 0, 0), pipeline_mode=pl.Buffered(1)),
            pl.BlockSpec((None, L, LANES), lambda i: (pair(i) // ne, 0, 0), pipeline_mode=pl.Buffered(1)),
            pl.BlockSpec((None, None, D_MODEL, ff), lambda i: (l, pair(i) % ne, 0, 0)),
            pl.BlockSpec((None, None, D_MODEL, ff), lambda i: (l, pair(i) % ne, 0, 0)),
            pl.BlockSpec((None, None, ff, D_MODEL), lambda i: (l, pair(i) % ne, 0, 0)),
        ],
        out_specs=pl.BlockSpec(memory_space=pl.ANY),
        out_shape=jax.ShapeDtypeStruct((B, L, D_MODEL), F32),
        scratch_shapes=[
            pltpu.VMEM((L, D_MODEL), F32),
            pltpu.VMEM((cap, D_MODEL // 2), jnp.uint32),
            pltpu.VMEM((cap, LANES), F32),
            pltpu.VMEM((2, cap, D_MODEL), F32),
            pltpu.SemaphoreType.DMA((1,)),
        ],
        compiler_params=_cparams(("arbitrary",)),
        name="moe",
    )(idx3, idx3, hp, aff, w1_bf, w3_bf, w2_bf)


def _final_body(*refs):
    g_ref, o_ref = refs[-2:]
    o_ref[...] = _rms(sum(r[...] for r in refs[:-2]), g_ref[...])


def _final_call(xs, g):
    B, L, D = xs[0].shape
    tm = 1024
    return pl.pallas_call(
        _final_body,
        grid=(B, L // tm),
        in_specs=[pl.BlockSpec((None, tm, D), lambda b, j: (b, j, 0))] * len(xs)
        + [pl.BlockSpec((1, D), lambda b, j: (0, 0))],
        out_specs=pl.BlockSpec((None, tm, D), lambda b, j: (b, j, 0)),
        out_shape=jax.ShapeDtypeStruct((B, L, D), F32),
        compiler_params=_cparams(("arbitrary", "arbitrary")),
        name="final_norm",
    )(*xs, g.reshape(1, D))


def kernel(x, mem, norm1_g, w_in, gate_b, sgu_ln_g, sgu_ln_b, sgu_w, sgu_b, hy_conv_w, hy_conv_b, hy_f_w1, hy_f_b1, hy_f_w2, hy_f_b2, hy_f_w3, hy_f_b3, hy_f_w4, hy_f_freq, hy_bias, mem_norm_g, w_kv, w_branch, w_out, norm2_g, w_router, w1, w3, w2, final_g):
    B, L, D = x.shape
    depth = w_in.shape[0]
    cap = EC_CAPACITY * L // N_EXPERTS
    nct = HY_WIDTH // LANES
    row = lambda a: a.reshape(depth, 1, a.shape[-1])

    w_in_bf = w_in.astype(BF16)
    w_kv_bf = w_kv.astype(BF16)
    w_branch_bf = w_branch.astype(BF16)
    w_out_bf = w_out.astype(BF16)
    sgu_w_bf = sgu_w.astype(BF16)
    w1_bf, w3_bf, w2_bf = w1.astype(BF16), w3.astype(BF16), w2.astype(BF16)
    sgu_bt = jnp.repeat(jnp.swapaxes(sgu_b, 1, 2), SGU_WIDTH // SGU_GROUPS, axis=2)
    wr_pad = jnp.pad(w_router, ((0, 0), (0, 0), (0, LANES - N_EXPERTS)))
    wr_t = jnp.swapaxes(w_router, 1, 2)
    hy_bias4 = hy_bias.reshape(depth, nct, 1, LANES)

    k_all, v_all = _kv_call(mem, mem_norm_g, w_kv_bf)
    filt = _filter_call(L, hy_f_w1, hy_f_b1, hy_f_w2, hy_f_b2, hy_f_w3, hy_f_b3, hy_f_w4, hy_f_freq)
    consts = _fft_constants(L)

    xs = [x]
    for l in range(depth):
        mp, gh, x0, u = _mixer_a_call(l, xs, row(norm1_g), w_in_bf, row(gate_b), row(sgu_ln_g), row(sgu_ln_b),
                                      sgu_w_bf, sgu_bt, hy_conv_w, row(hy_conv_b), k_all, v_all, w_branch_bf)
        hy = _hyena_call(l, filt, u, x0, hy_bias4, consts)
        x1, hp, aff, afft = _mixer_c_call(l, xs, mp, gh, hy, w_branch_bf, w_out_bf, row(norm2_g), wr_pad, wr_t)
        idx = _route_call(afft, cap)
        xs = [x1, _moe_call(l, idx, hp, aff, w1_bf, w3_bf, w2_bf)]
    return _final_call(xs, final_g)
```

```python
import functools
import math

import numpy as np
import jax
import jax.numpy as jnp
from jax import lax
from jax.experimental import pallas as pl
from jax.experimental.pallas import tpu as pltpu

F32 = jnp.float32
BF16 = jnp.bfloat16

D_MODEL = 1024
SGU_WIDTH = 512
SGU_GROUPS = 4
SGU_CHUNK = 128
HY_WIDTH = 512
HY_EMB_DIM = 33
HY_BANDS = (HY_EMB_DIM - 1) // 2
HY_FILTER_HIDDEN = 64
HY_FAST_DECAY_PCT = 0.3
HY_SLOW_DECAY_PCT = 1.5
HY_DECAY_TARGET = 1e-2
XA_HEADS = 4
XA_HEAD_DIM = 128
XA_WIDTH = XA_HEADS * XA_HEAD_DIM
N_EXPERTS = 16
EC_CAPACITY = 2
NORM_EPS = 1e-6
LN_EPS = 1e-5

C_SGU = 2 * SGU_WIDTH
C_HY = C_SGU + 3 * HY_WIDTH
C_Q = C_HY + XA_WIDTH

LANES = 128
SUBLANES = 8
VMEM_LIMIT = 56 * 1024 * 1024

FFT_N2 = 128
FFT_ROWS = 256
HY_CT = LANES

MOE_ROW_CHUNK = 512


def _cparams(semantics, flags=None):
    return pltpu.CompilerParams(dimension_semantics=semantics, vmem_limit_bytes=VMEM_LIMIT, flags=flags)


def _rms(x, g):
    return x * lax.rsqrt(jnp.mean(x * x, axis=-1, keepdims=True) + NORM_EPS) * g


def _dot(a, b):
    return jnp.dot(a, b, preferred_element_type=F32)


def _dot_nt(a, b):
    return lax.dot_general(a, b, (((1,), (1,)), ((), ())), preferred_element_type=F32)


def _split_bf16(x):
    hi = x.astype(BF16)
    lo = (x - hi.astype(F32)).astype(BF16)
    return hi, lo


def _kv_body(mem_ref, g_ref, w_ref, k_ref, v_ref):
    m = _rms(mem_ref[...], g_ref[...]).astype(BF16)
    kv = _dot(m, w_ref[...])
    k_ref[...] = kv[:, :XA_WIDTH].astype(BF16)
    v_ref[...] = kv[:, XA_WIDTH:].astype(BF16)


def _kv_call(mem, mem_norm_g, w_kv_bf):
    B, M, D = mem.shape
    depth = w_kv_bf.shape[0]
    return pl.pallas_call(
        _kv_body,
        grid=(depth, B),
        in_specs=[
            pl.BlockSpec((None, M, D), lambda l, b: (b, 0, 0)),
            pl.BlockSpec((None, 1, D), lambda l, b: (l, 0, 0)),
            pl.BlockSpec((None, D, 2 * XA_WIDTH), lambda l, b: (l, 0, 0)),
        ],
        out_specs=[
            pl.BlockSpec((None, None, M, XA_WIDTH), lambda l, b: (l, b, 0, 0)),
            pl.BlockSpec((None, None, M, XA_WIDTH), lambda l, b: (l, b, 0, 0)),
        ],
        out_shape=[jax.ShapeDtypeStruct((depth, B, M, XA_WIDTH), BF16)] * 2,
        compiler_params=_cparams(("arbitrary", "arbitrary")),
        name="kv",
    )(mem, mem_norm_g.reshape(depth, 1, D), w_kv_bf)


FILT_GROUPS = LANES // HY_BANDS


def _filter_body(w1t_ref, w1c_ref, w1s_ref, b1_ref, w2_ref, b2_ref, w3_ref, b3_ref, w4_ref, fr_ref,
                 bands_ref, deltas_ref, o_ref, *, seq_len, rows):
    d = pl.program_id(1)
    r0 = pl.program_id(2) * rows
    ng, hid = FILT_GROUPS, HY_FILTER_HIDDEN
    rg = rows // ng

    def lags(width, per_group):
        ridx = (r0 + lax.broadcasted_iota(jnp.int32, (rg, width), 0)
                + (lax.broadcasted_iota(jnp.int32, (rg, width), 1) // per_group) * rg)
        return ridx, (ridx + d * (seq_len - 2 * ridx)).astype(F32)

    _, pos_b = lags(LANES, HY_BANDS)
    ridx, pos_h = lags(ng * hid, hid)
    t = pos_h * (1.0 / (seq_len - 1))
    ang = pos_b * (2.0 * math.pi / seq_len) * bands_ref[...]
    hp = lax.Precision.HIGHEST
    dot = lambda a, w_ref: jnp.dot(a, w_ref[...], precision=hp, preferred_element_type=F32)
    fr = fr_ref[...]
    h = jnp.sin(fr * (t * w1t_ref[...] + dot(jnp.cos(ang), w1c_ref) - dot(jnp.sin(ang), w1s_ref) + b1_ref[...]))
    h = jnp.sin(fr * (dot(h, w2_ref) + b2_ref[...]))
    h = jnp.sin(fr * (dot(h, w3_ref) + b3_ref[...]))
    for g in range(ng):
        og = dot(h[:, g * hid:(g + 1) * hid], w4_ref)
        og = og * jnp.exp(-t[:, g * hid:g * hid + 1] * deltas_ref[...])
        og = jnp.where((ridx[:, g * hid:g * hid + 1] == 0) & (d == 1), 0.0, og)
        for c in range(HY_WIDTH // HY_CT):
            o_ref[c, g * rg:(g + 1) * rg, :] = og[:, c * HY_CT:(c + 1) * HY_CT]


def _filter_call(seq_len, w1, b1, w2, b2, w3, b3, w4, freq):
    depth = w1.shape[0]
    H = HY_FILTER_HIDDEN
    ng = FILT_GROUPS
    rows = 1024
    nct = HY_WIDTH // HY_CT
    bands = np.tile(np.linspace(1e-4, HY_BANDS - 1, HY_BANDS), ng).astype(np.float32).reshape(1, LANES)
    max_decay = math.log(HY_DECAY_TARGET) / HY_FAST_DECAY_PCT
    min_decay = math.log(HY_DECAY_TARGET) / HY_SLOW_DECAY_PCT
    deltas = np.abs(np.linspace(min_decay, max_decay, HY_WIDTH)).astype(np.float32).reshape(1, HY_WIDTH)
    eye = jnp.eye(ng, dtype=F32)
    bdiag = lambda w: jnp.einsum("gh,lij->lgihj", eye, w).reshape(depth, ng * w.shape[1], ng * w.shape[2])
    tiled = lambda v: jnp.tile(v.reshape(depth, 1, H), (1, 1, ng))
    w1t = tiled(w1[:, 0, :])
    w1c = bdiag(w1[:, 1:1 + HY_BANDS, :])
    w1s = bdiag(w1[:, 1 + HY_BANDS:, :])
    vec = lambda n: pl.BlockSpec((None, 1, n), lambda l, d, r: (l, 0, 0))
    mat = lambda m, n: pl.BlockSpec((None, m, n), lambda l, d, r: (l, 0, 0))
    gh = ng * H
    return pl.pallas_call(
        functools.partial(_filter_body, seq_len=seq_len, rows=rows),
        grid=(depth, 2, seq_len // rows),
        in_specs=[
            vec(gh), mat(LANES, gh), mat(LANES, gh), vec(gh), mat(gh, gh), vec(gh), mat(gh, gh), vec(gh),
            pl.BlockSpec((None, H, HY_WIDTH), lambda l, d, r: (l, 0, d)),
            vec(gh),
            pl.BlockSpec((1, LANES), lambda l, d, r: (0, 0)),
            pl.BlockSpec((1, HY_WIDTH), lambda l, d, r: (0, 0)),
        ],
        out_specs=pl.BlockSpec((None, None, nct, rows, HY_CT), lambda l, d, r: (l, d, 0, r, 0)),
        out_shape=jax.ShapeDtypeStruct((depth, 2, nct, seq_len, HY_CT), F32),
        compiler_params=_cparams(("arbitrary", "arbitrary", "arbitrary")),
        name="hyena_filter",
    )(w1t, w1c, w1s, tiled(b1), bdiag(w2), tiled(b2), bdiag(w3), tiled(b3), w4, tiled(freq),
      jnp.asarray(bands), jnp.asarray(deltas))


def _mixer_a_body(*refs, tm, n_x):
    x_refs, refs = refs[:3 * n_x], refs[3 * n_x:]
    (g1_ref, win_ref, gb_ref, lng_ref, lnb_ref, sw_ref, sbt_ref, cw_ref, cb_ref, k_ref, v_ref, wba_ref, wbc_ref,
     mp_ref, gh_ref, x0_ref, u_ref) = refs
    x_cur = sum(r[...] for r in x_refs[0::3])
    x_prev = sum(r[...] for r in x_refs[1::3])
    x_next = sum(r[...] for r in x_refs[2::3])
    j = pl.program_id(1)
    nj = pl.num_programs(1)
    g1 = g1_ref[...]
    h = _rms(x_cur, g1)
    hb = h.astype(BF16)

    zs = _dot(hb, win_ref[:, 0:C_SGU])
    zs = 0.5 * zs * (1.0 + lax.erf(zs * (1.0 / math.sqrt(2.0))))
    us = zs[:, :SGU_WIDTH]
    vs = zs[:, SGU_WIDTH:]
    mu = jnp.mean(vs, axis=-1, keepdims=True)
    vc = vs - mu
    var = jnp.mean(vc * vc, axis=-1, keepdims=True)
    vn = (vc * lax.rsqrt(var + LN_EPS) * lng_ref[...] + lnb_ref[...]).astype(BF16)
    gd = SGU_WIDTH // SGU_GROUPS
    chunks = []
    for c in range(tm // SGU_CHUNK):
        groups = []
        for g in range(SGU_GROUPS):
            blk = vn[c * SGU_CHUNK:(c + 1) * SGU_CHUNK, g * gd:(g + 1) * gd]
            groups.append(_dot(sw_ref[g], blk))
        chunks.append(jnp.concatenate(groups, axis=1) + sbt_ref[...])
    a = us * jnp.concatenate(chunks, axis=0)

    q = _dot(hb, win_ref[:, C_HY:C_Q])
    scale = 1.0 / math.sqrt(XA_HEAD_DIM)
    heads = []
    for hd in range(XA_HEADS):
        sl = slice(hd * XA_HEAD_DIM, (hd + 1) * XA_HEAD_DIM)
        s = _dot_nt(q[:, sl].astype(BF16), k_ref[:, sl]) * scale
        s = s - jnp.max(s, axis=-1, keepdims=True)
        p = jnp.exp(s)
        p = p / jnp.sum(p, axis=-1, keepdims=True)
        heads.append(_dot(p.astype(BF16), v_ref[:, sl]))
    c_att = jnp.concatenate(heads, axis=1)

    gates = jax.nn.sigmoid(_dot(hb, win_ref[:, C_Q:]) + gb_ref[...])
    pa = _dot(a.astype(BF16), wba_ref[...])
    pc = _dot(c_att.astype(BF16), wbc_ref[...])
    mp_ref[...] = gates[:, 0:D_MODEL] * pa + gates[:, 2 * D_MODEL:] * pc
    gh_ref[...] = gates[:, D_MODEL:2 * D_MODEL].astype(BF16)

    hprev = _rms(x_prev, g1).astype(BF16)
    hnext = _rms(x_next, g1).astype(BF16)
    hext = jnp.concatenate([hprev, hb, hnext], axis=0)
    pe = _dot(hext, win_ref[:, C_SGU:C_HY])
    rows = lax.broadcasted_iota(jnp.int32, (tm, 1), 0)
    pm1 = jnp.where((rows == 0) & (j == 0), 0.0, pe[SUBLANES - 1:SUBLANES - 1 + tm])
    pp1 = jnp.where((rows == tm - 1) & (j == nj - 1), 0.0, pe[SUBLANES + 1:SUBLANES + 1 + tm])
    p0 = pe[SUBLANES:SUBLANES + tm]
    pconv = cw_ref[0:1, :] * pm1 + cw_ref[1:2, :] * p0 + cw_ref[2:3, :] * pp1 + cb_ref[...]
    x0 = pconv[:, :HY_WIDTH]
    u = pconv[:, HY_WIDTH:2 * HY_WIDTH] * pconv[:, 2 * HY_WIDTH:]
    for c in range(HY_WIDTH // HY_CT):
        x0_ref[c] = x0[:, c * HY_CT:(c + 1) * HY_CT]
        u_ref[c] = u[:, c * HY_CT:(c + 1) * HY_CT]


def _mixer_a_call(l, xs, norm1_g, w_in_bf, gate_b, sgu_ln_g, sgu_ln_b, sgu_w_bf, sgu_bt, conv_w, conv_b,
                  k_all, v_all, w_branch_bf):
    B, L, D = xs[0].shape
    tm = 512
    nct = HY_WIDTH // HY_CT
    in_cols = w_in_bf.shape[-1]
    hb_per_tile = tm // SUBLANES
    n_hb = L // SUBLANES
    const = lambda shape: pl.BlockSpec((None,) + shape, lambda b, j: (l,) + (0,) * len(shape),
                                       pipeline_mode=pl.Buffered(1))
    x_specs = [
        pl.BlockSpec((None, tm, D), lambda b, j: (b, j, 0)),
        pl.BlockSpec((None, SUBLANES, D), lambda b, j: (b, jnp.maximum(j * hb_per_tile - 1, 0), 0)),
        pl.BlockSpec((None, SUBLANES, D), lambda b, j: (b, jnp.minimum((j + 1) * hb_per_tile, n_hb - 1), 0)),
    ]
    x_args = [a for x in xs for a in (x, x, x)]
    return pl.pallas_call(
        functools.partial(_mixer_a_body, tm=tm, n_x=len(xs)),
        grid=(B, L // tm),
        in_specs=x_specs * len(xs) + [
            const((1, D)),
            const((D, in_cols)),
            const((1, 3 * D)),
            const((1, SGU_WIDTH)),
            const((1, SGU_WIDTH)),
            const((SGU_GROUPS, SGU_CHUNK, SGU_CHUNK)),
            const((SGU_CHUNK, SGU_WIDTH)),
            const((3, 3 * HY_WIDTH)),
            const((1, 3 * HY_WIDTH)),
            pl.BlockSpec((None, None, k_all.shape[2], XA_WIDTH), lambda b, j: (l, b, 0, 0)),
            pl.BlockSpec((None, None, k_all.shape[2], XA_WIDTH), lambda b, j: (l, b, 0, 0)),
            pl.BlockSpec((None, None, SGU_WIDTH, D), lambda b, j: (l, 0, 0, 0), pipeline_mode=pl.Buffered(1)),
            pl.BlockSpec((None, None, XA_WIDTH, D), lambda b, j: (l, 2, 0, 0), pipeline_mode=pl.Buffered(1)),
        ],
        out_specs=[
            pl.BlockSpec((None, tm, D), lambda b, j: (b, j, 0)),
            pl.BlockSpec((None, tm, D), lambda b, j: (b, j, 0)),
            pl.BlockSpec((None, nct, tm, HY_CT), lambda b, j: (b, 0, j, 0)),
            pl.BlockSpec((None, nct, tm, HY_CT), lambda b, j: (b, 0, j, 0)),
        ],
        out_shape=[
            jax.ShapeDtypeStruct((B, L, D), F32),
            jax.ShapeDtypeStruct((B, L, D), BF16),
            jax.ShapeDtypeStruct((B, nct, L, HY_CT), F32),
            jax.ShapeDtypeStruct((B, nct, L, HY_CT), F32),
        ],
        compiler_params=_cparams(("arbitrary", "arbitrary")),
        name="mixer_a",
    )(*x_args, norm1_g, w_in_bf, gate_b, sgu_ln_g, sgu_ln_b, sgu_w_bf, sgu_bt, conv_w, conv_b,
      k_all, v_all, w_branch_bf, w_branch_bf)


def _fft_constants(seq_len):
    n = 2 * seq_len
    n2 = FFT_N2
    n1 = n // n2
    nz = seq_len // n2
    lo = FFT_ROWS // nz
    tiles = n2 // lo
    k1n = n1 // 2 + 1
    assert lo == SUBLANES and nz * lo == FFT_ROWS
    k1 = np.arange(k1n).reshape(k1n, 1, 1, 1, 1).astype(np.float64)
    part = np.arange(2).reshape(1, 2, 1, 1, 1)
    a = np.arange(lo).reshape(1, 1, lo, 1, 1)
    m1 = np.arange(nz).reshape(1, 1, 1, nz, 1).astype(np.float64)
    a2 = np.arange(lo).reshape(1, 1, 1, 1, lo)
    fwd = np.zeros((tiles, k1n, 2, lo, nz, lo), np.float64)
    for j in range(tiles):
        theta = 2.0 * np.pi * (m1 * k1 / n1 + (lo * j + a) * k1 / n)
        val = np.where(part == 0, np.cos(theta), -np.sin(theta))
        fwd[j] = val * (a == a2)
    fwd = fwd.reshape(tiles, k1n * 2 * lo, nz * lo)
    weight = np.where((np.arange(k1n) == 0) | (np.arange(k1n) == n1 // 2), 1.0, 2.0) / n
    inv = fwd.reshape(tiles, k1n, 2 * lo, nz * lo) * weight.reshape(1, k1n, 1, 1)
    inv = inv.reshape(tiles, k1n * 2 * lo, nz * lo).transpose(0, 2, 1)
    phi = 2.0 * np.pi * np.outer(np.arange(n2), np.arange(n2)) / n2
    c, s = np.cos(phi), np.sin(phi)
    g_fwd = np.block([[c, s], [-s, c]])
    g_inv = np.block([[c, -s], [s, c]])
    f32 = lambda v: jnp.asarray(v.astype(np.float32)).astype(BF16)
    return f32(fwd), f32(inv), f32(g_fwd), f32(g_inv), (tiles, k1n, lo, nz)


def _hyena_body(filt_ref, u_ref, x0_ref, bias_ref, fa_ref, ia_ref, gf_ref, gi_ref, o_ref, z_ref, k_ref,
                *, tiles, k1n, lo, nz):
    s = pl.program_id(1)
    half = FFT_N2
    rows = nz * lo

    def stage_a(src_ref):
        for j in range(tiles):
            tile = src_ref[:, j].reshape(rows, LANES).astype(BF16)
            z_ref[:, :, j] = _dot(fa_ref[j], tile).reshape(k1n, 2, lo, LANES)

    def spectrum(k1):
        return _dot(gf_ref[...], z_ref[k1].reshape(2 * half, LANES).astype(BF16))

    @pl.when(s == 0)
    def _():
        k_ref[...] = jnp.zeros(k_ref.shape, F32)

    @pl.when(s < 2)
    def _():
        stage_a(filt_ref)
        odd_sign = jnp.where(s == 0, 1.0, -1.0).astype(F32)
        for k1 in range(k1n):
            x = spectrum(k1)
            k_ref[k1] += x if k1 % 2 == 0 else odd_sign * x

    @pl.when(s >= 2)
    def _():
        stage_a(u_ref)
        for k1 in range(k1n):
            x = spectrum(k1)
            xr, xi = x[:half], x[half:]
            kr, ki = k_ref[k1, :half], k_ref[k1, half:]
            y = jnp.concatenate([xr * kr - xi * ki, xr * ki + xi * kr], axis=0).astype(BF16)
            z_ref[k1] = _dot(gi_ref[...], y).reshape(2, tiles, lo, LANES)
        bias = bias_ref[...]
        for j in range(tiles):
            zt = z_ref[:, :, j].reshape(k1n * 2 * lo, LANES).astype(BF16)
            y = _dot(ia_ref[j], zt)
            uu = u_ref[:, j].reshape(rows, LANES)
            x0 = x0_ref[:, j].reshape(rows, LANES)
            o_ref[:, j] = (x0 * (y + bias * uu)).reshape(nz, lo, LANES)


def _hyena_call(l, filt, u, x0, hy_bias, consts):
    fa, ia, gf, gi, (tiles, k1n, lo, nz) = consts
    B, nct, L, _ = u.shape
    view = lambda a: a.reshape(a.shape[:-2] + (nz, tiles, lo, LANES))
    seq = (nz, tiles, lo, LANES)
    bidx = lambda s: jnp.maximum(s - 2, 0)
    single = lambda shape: pl.BlockSpec(shape, lambda c, s: (0,) * len(shape), pipeline_mode=pl.Buffered(1))
    out = pl.pallas_call(
        functools.partial(_hyena_body, tiles=tiles, k1n=k1n, lo=lo, nz=nz),
        grid=(nct, B + 2),
        in_specs=[
            pl.BlockSpec((None, None, None) + seq, lambda c, s: (l, jnp.minimum(s, 1), c, 0, 0, 0, 0)),
            pl.BlockSpec((None, None) + seq, lambda c, s: (bidx(s), c, 0, 0, 0, 0)),
            pl.BlockSpec((None, None) + seq, lambda c, s: (bidx(s), c, 0, 0, 0, 0)),
            pl.BlockSpec((None, None, 1, LANES), lambda c, s: (l, c, 0, 0)),
            single(fa.shape), single(ia.shape), single(gf.shape), single(gi.shape),
        ],
        out_specs=pl.BlockSpec((None, None) + seq, lambda c, s: (bidx(s), c, 0, 0, 0, 0)),
        out_shape=jax.ShapeDtypeStruct((B, nct) + seq, F32),
        scratch_shapes=[
            pltpu.VMEM((k1n, 2, tiles, lo, LANES), F32),
            pltpu.VMEM((k1n, 2 * FFT_N2, LANES), F32),
        ],
        compiler_params=_cparams(("arbitrary", "arbitrary")),
        name="hyena_fft",
    )(view(filt), view(u), view(x0), hy_bias, fa, ia, gf, gi)
    return out.reshape(B, nct, L, HY_CT)


def _pack_bf16_pairs(x):
    n = x.shape[1] // 2
    return pltpu.pack_elementwise([x[:, :n], x[:, n:]], packed_dtype=BF16)


def _unpack_bf16_pairs(p):
    lo = pltpu.unpack_elementwise(p, index=0, packed_dtype=BF16, unpacked_dtype=F32)
    hi = pltpu.unpack_elementwise(p, index=1, packed_dtype=BF16, unpacked_dtype=F32)
    return jnp.concatenate([lo, hi], axis=1).astype(BF16)


def _mixer_c_body(*refs, n_x):
    x_refs, refs = refs[:n_x], refs[n_x:]
    (mp_ref, gh_ref, hy_ref, wbh_ref, wo_ref, g2_ref, wr_ref, x1_ref, hp_ref, aff_ref, afft_ref) = refs
    hy = jnp.concatenate([hy_ref[c] for c in range(HY_WIDTH // HY_CT)], axis=1).astype(BF16)
    merged = mp_ref[...] + gh_ref[...].astype(F32) * _dot(hy, wbh_ref[...])
    x1 = sum(r[...] for r in x_refs) + _dot(merged.astype(BF16), wo_ref[...])
    x1_ref[...] = x1
    h2 = _rms(x1, g2_ref[...])
    hp_ref[...] = _pack_bf16_pairs(h2)
    tm = h2.shape[0]
    hh, hl = _split_bf16(h2)
    wh, wl = _split_bf16(wr_ref[...])
    prod = _dot(jnp.concatenate([hh, hl], axis=0), jnp.concatenate([wh, wl], axis=1))
    logits = prod[:tm, :LANES] + prod[:tm, LANES:] + prod[tm:, :LANES]
    lane = lax.broadcasted_iota(jnp.int32, logits.shape, 1)
    logits = jnp.where(lane < N_EXPERTS, logits, -jnp.inf)
    e = jnp.exp(logits - jnp.max(logits, axis=-1, keepdims=True))
    aff = e / jnp.sum(e, axis=-1, keepdims=True)
    aff_ref[...] = aff
    afft_ref[...] = aff.T[:N_EXPERTS, :]


def _mixer_c_call(l, xs, mp, gh, hy, w_branch_bf, w_out_bf, norm2_g, wr_pad):
    B, L, D = xs[0].shape
    tm = 512
    nct = HY_WIDTH // HY_CT
    const = lambda shape: pl.BlockSpec((None,) + shape, lambda b, j: (l,) + (0,) * len(shape),
                                       pipeline_mode=pl.Buffered(1))
    return pl.pallas_call(
        functools.partial(_mixer_c_body, n_x=len(xs)),
        grid=(B, L // tm),
        in_specs=[pl.BlockSpec((None, tm, D), lambda b, j: (b, j, 0))] * len(xs) + [
            pl.BlockSpec((None, tm, D), lambda b, j: (b, j, 0)),
            pl.BlockSpec((None, tm, D), lambda b, j: (b, j, 0)),
            pl.BlockSpec((None, nct, tm, HY_CT), lambda b, j: (b, 0, j, 0)),
            pl.BlockSpec((None, None, HY_WIDTH, D), lambda b, j: (l, 1, 0, 0), pipeline_mode=pl.Buffered(1)),
            const((D, D)),
            const((1, D)),
            const((D, LANES)),
        ],
        out_specs=[
            pl.BlockSpec((None, tm, D), lambda b, j: (b, j, 0)),
            pl.BlockSpec((None, tm, D // 2), lambda b, j: (b, j, 0)),
            pl.BlockSpec((None, tm, LANES), lambda b, j: (b, j, 0)),
            pl.BlockSpec((None, N_EXPERTS, tm), lambda b, j: (b, 0, j)),
        ],
        out_shape=[
            jax.ShapeDtypeStruct((B, L, D), F32),
            jax.ShapeDtypeStruct((B, L, D // 2), jnp.uint32),
            jax.ShapeDtypeStruct((B, L, LANES), F32),
            jax.ShapeDtypeStruct((B, N_EXPERTS, L), F32),
        ],
        compiler_params=_cparams(("arbitrary", "arbitrary")),
        name="mixer_c",
    )(*xs, mp, gh, hy, w_branch_bf, w_out_bf, norm2_g, wr_pad)


def _cumsum_lanes(x, tri):
    outs = []
    carry = jnp.zeros((x.shape[0], 1), F32)
    for blk in range(x.shape[1] // LANES):
        loc = _dot(x[:, blk * LANES:(blk + 1) * LANES].astype(BF16), tri) + carry
        outs.append(loc)
        carry = loc[:, LANES - 1:LANES]
    return jnp.concatenate(outs, axis=1)


def _route_body(afft_ref, tri_ref, idx_ref, sel_ref, thr_ref, *, cap):
    b = pl.program_id(0)
    nb, ne, seq = afft_ref.shape

    @pl.when(b == 0)
    def _():
        a_all = afft_ref[...].reshape(nb * ne, seq)

        def bit_step(i, v):
            cand = v | jnp.left_shift(jnp.int32(1), 30 - i)
            cnt = jnp.sum((a_all >= lax.bitcast_convert_type(cand, F32)).astype(F32), axis=1, keepdims=True)
            return jnp.where(cnt >= cap, cand, v)

        bits = lax.fori_loop(0, 31, bit_step, jnp.zeros((nb * ne, 1), jnp.int32))
        thr_ref[...] = jnp.broadcast_to(lax.bitcast_convert_type(bits, F32), thr_ref.shape)

    a = afft_ref[b]
    thr = thr_ref[pl.ds(pl.multiple_of(b * ne, ne), ne), 0:1]
    gt = a > thr
    eq = a == thr
    need = cap - jnp.sum(gt.astype(F32), axis=1, keepdims=True)
    tri = tri_ref[...]
    cum_eq = _cumsum_lanes(eq.astype(F32), tri)
    sel = gt | (eq & (cum_eq <= need))
    sel_ref[...] = sel.astype(F32)

    nblk = seq // LANES
    f32c = lambda v: jnp.where(v, 1.0, 0.0)
    lane = lax.broadcasted_iota(jnp.int32, (cap, LANES), 1)
    slot = lax.broadcasted_iota(jnp.int32, (cap, LANES), 0).astype(F32)
    ones_row = jnp.ones((SUBLANES, LANES), BF16)
    bi = lax.broadcasted_iota(jnp.int32, (nblk, LANES), 0)
    bj = lax.broadcasted_iota(jnp.int32, (nblk, LANES), 1)
    upper = f32c((bi <= bj) | (bj >= nblk)).astype(BF16)
    lower = f32c(bj < bi)[:, :nblk].astype(BF16)
    half = float(cap // 2)
    experts = range(ne)
    selb = [jnp.concatenate([sel_ref[e:e + 1, j * LANES:(j + 1) * LANES] for j in range(nblk)], axis=0).astype(BF16)
            for e in experts]
    loc = [_dot(selb[e], tri) for e in experts]
    tot_row = [_dot_nt(ones_row, selb[e])[:, :nblk].astype(BF16) for e in experts]
    incl_row = [_dot(tot_row[e], upper)[0:1, :] for e in experts]
    excl = [_dot(lower, jnp.broadcast_to(loc[e][:, LANES - 1:LANES], (nblk, LANES)).astype(BF16)) for e in experts]
    jb = [jnp.sum(f32c(incl_row[e] <= slot), axis=1, keepdims=True) for e in experts]
    onehot = [f32c(jb[e].astype(jnp.int32) == lane).astype(BF16) for e in experts]
    pad = jnp.zeros((LANES - nblk, 3 * LANES), F32)
    table = [jnp.concatenate([jnp.concatenate([loc[e], jnp.minimum(excl[e], half),
                                               jnp.maximum(excl[e] - half, 0.0)], axis=1), pad], axis=0).astype(BF16)
             for e in experts]
    got = [_dot(onehot[e], table[e]) for e in experts]
    idx_t = jnp.zeros((cap, LANES), F32)
    for e in experts:
        local = slot - (got[e][:, LANES:2 * LANES] + got[e][:, 2 * LANES:])
        pos = jnp.sum(f32c(got[e][:, :LANES] <= local), axis=1, keepdims=True)
        idx_t = idx_t + jnp.where(lane == e, jb[e] * float(LANES) + pos, 0.0)
    idx_ref[...] = idx_t.T[:ne, :].astype(jnp.int32)


def _route_call(afft, cap):
    B, ne, L = afft.shape
    tri = jnp.asarray(np.triu(np.ones((LANES, LANES), np.float32))).astype(BF16)
    return pl.pallas_call(
        functools.partial(_route_body, cap=cap),
        grid=(B,),
        in_specs=[
            pl.BlockSpec((B, ne, L), lambda b: (0, 0, 0)),
            pl.BlockSpec((LANES, LANES), lambda b: (0, 0)),
        ],
        out_specs=pl.BlockSpec((None, ne, cap), lambda b: (b, 0, 0)),
        out_shape=jax.ShapeDtypeStruct((B, ne, cap), jnp.int32),
        scratch_shapes=[pltpu.VMEM((ne, L), F32), pltpu.VMEM((B * ne, LANES), F32)],
        compiler_params=_cparams(("arbitrary",)),
        name="route",
    )(afft, tri)


def _moe_body(idx_ref, idxp_ref, hp_ref, aff_ref, w1_ref, w3_ref, w2_ref, out_hbm,
              acc_ref, xin_ref, gsel_ref, y_ref, sem_ref, *, cap, ne, npairs):
    i = pl.program_id(0)
    slot = i % 2
    e = jnp.minimum(i, npairs - 1) % ne

    @pl.when(i == 0)
    def _():
        acc_ref[...] = jnp.zeros(acc_ref.shape, F32)
        y_ref[...] = jnp.zeros(y_ref.shape, F32)

    y_prev = y_ref.at[1 - slot]
    for g0 in range(0, cap, SUBLANES):
        toks = [idxp_ref[0, 0, g0 + k] for k in range(SUBLANES)]
        vals = [acc_ref[pl.ds(toks[k], 1), :] + y_prev[pl.ds(g0 + k, 1), :] for k in range(SUBLANES)]
        for k in range(SUBLANES):
            acc_ref[pl.ds(toks[k], 1), :] = vals[k]

    for s in range(cap):
        t = idx_ref[0, 0, s]
        xin_ref[pl.ds(s, 1), :] = hp_ref[pl.ds(t, 1), :]
        gsel_ref[pl.ds(s, 1), :] = aff_ref[pl.ds(t, 1), :]

    aff = gsel_ref[...]
    lane = lax.broadcasted_iota(jnp.int32, aff.shape, 1)
    gate = jnp.sum(jnp.where(lane == e, aff, 0.0), axis=1, keepdims=True)
    for r0 in range(0, cap, MOE_ROW_CHUNK):
        rs = slice(r0, r0 + MOE_ROW_CHUNK)
        xb = _unpack_bf16_pairs(xin_ref[rs, :])
        a1 = _dot(xb, w1_ref[...])
        a3 = _dot(xb, w3_ref[...])
        hid = (a1 * jax.nn.sigmoid(a1) * a3).astype(BF16)
        y_ref[slot, rs, :] = _dot(hid, w2_ref[...]) * gate[rs]

    @pl.when((i > 0) & (i % ne == 0))
    def _():
        cp = pltpu.make_async_copy(acc_ref, out_hbm.at[i // ne - 1], sem_ref.at[0])
        cp.start()
        cp.wait()
        acc_ref[...] = jnp.zeros(acc_ref.shape, F32)


def _moe_call(l, idx, hp, aff, w1_bf, w3_bf, w2_bf):
    B, ne, cap = idx.shape
    L = hp.shape[1]
    ff = w1_bf.shape[-1]
    npairs = B * ne
    idx3 = idx.reshape(npairs, 1, cap)
    pair = lambda i: jnp.minimum(i, npairs - 1)
    prev = lambda i: jnp.maximum(i - 1, 0)
    return pl.pallas_call(
        functools.partial(_moe_body, cap=cap, ne=ne, npairs=npairs),
        grid=(npairs + 1,),
        in_specs=[
            pl.BlockSpec((1, 1, cap), lambda i: (pair(i), 0, 0), memory_space=pltpu.SMEM),
            pl.BlockSpec((1, 1, cap), lambda i: (prev(i), 0, 0), memory_space=pltpu.SMEM),
            pl.BlockSpec((None, L, D_MODEL // 2), lambda i: (pair(i) // ne, 0, 0), pipeline_mode=pl.Buffered(1)),
            pl.BlockSpec((None, L, LANES), lambda i: (pair(i) // ne, 0, 0), pipeline_mode=pl.Buffered(1)),
            pl.BlockSpec((None, None, D_MODEL, ff), lambda i: (l, pair(i) % ne, 0, 0)),
            pl.BlockSpec((None, None, D_MODEL, ff), lambda i: (l, pair(i) % ne, 0, 0)),
            pl.BlockSpec((None, None, ff, D_MODEL), lambda i: (l, pair(i) % ne, 0, 0)),
        ],
        out_specs=pl.BlockSpec(memory_space=pl.ANY),
        out_shape=jax.ShapeDtypeStruct((B, L, D_MODEL), F32),
        scratch_shapes=[
            pltpu.VMEM((L, D_MODEL), F32),
            pltpu.VMEM((cap, D_MODEL // 2), jnp.uint32),
            pltpu.VMEM((cap, LANES), F32),
            pltpu.VMEM((2, cap, D_MODEL), F32),
            pltpu.SemaphoreType.DMA((1,)),
        ],
        compiler_params=_cparams(("arbitrary",)),
        name="moe",
    )(idx3, idx3, hp, aff, w1_bf, w3_bf, w2_bf)


def _final_body(*refs):
    g_ref, o_ref = refs[-2:]
    o_ref[...] = _rms(sum(r[...] for r in refs[:-2]), g_ref[...])


def _final_call(xs, g):
    B, L, D = xs[0].shape
    tm = 1024
    return pl.pallas_call(
        _final_body,
        grid=(B, L // tm),
        in_specs=[pl.BlockSpec((None, tm, D), lambda b, j: (b, j, 0))] * len(xs)
        + [pl.BlockSpec((1, D), lambda b, j: (0, 0))],
        out_specs=pl.BlockSpec((None, tm, D), lambda b, j: (b, j, 0)),
        out_shape=jax.ShapeDtypeStruct((B, L, D), F32),
        compiler_params=_cparams(("arbitrary", "arbitrary")),
        name="final_norm",
    )(*xs, g.reshape(1, D))


def kernel(x, mem, norm1_g, w_in, gate_b, sgu_ln_g, sgu_ln_b, sgu_w, sgu_b, hy_conv_w, hy_conv_b, hy_f_w1, hy_f_b1, hy_f_w2, hy_f_b2, hy_f_w3, hy_f_b3, hy_f_w4, hy_f_freq, hy_bias, mem_norm_g, w_kv, w_branch, w_out, norm2_g, w_router, w1, w3, w2, final_g):
    B, L, D = x.shape
    depth = w_in.shape[0]
    cap = EC_CAPACITY * L // N_EXPERTS
    nct = HY_WIDTH // HY_CT
    row = lambda a: a.reshape(depth, 1, a.shape[-1])

    w_in_bf = w_in.astype(BF16)
    w_kv_bf = w_kv.astype(BF16)
    w_branch_bf = w_branch.astype(BF16)
    w_out_bf = w_out.astype(BF16)
    sgu_w_bf = sgu_w.astype(BF16)
    w1_bf, w3_bf, w2_bf = w1.astype(BF16), w3.astype(BF16), w2.astype(BF16)
    sgu_bt = jnp.repeat(jnp.swapaxes(sgu_b, 1, 2), SGU_WIDTH // SGU_GROUPS, axis=2)
    wr_pad = jnp.pad(w_router, ((0, 0), (0, 0), (0, LANES - N_EXPERTS)))
    hy_bias4 = hy_bias.reshape(depth, nct, 1, LANES)

    k_all, v_all = _kv_call(mem, mem_norm_g, w_kv_bf)
    filt = _filter_call(L, hy_f_w1, hy_f_b1, hy_f_w2, hy_f_b2, hy_f_w3, hy_f_b3, hy_f_w4, hy_f_freq)
    consts = _fft_constants(L)

    xs = [x]
    for l in range(depth):
        mp, gh, x0, u = _mixer_a_call(l, xs, row(norm1_g), w_in_bf, row(gate_b), row(sgu_ln_g), row(sgu_ln_b),
                                      sgu_w_bf, sgu_bt, hy_conv_w, row(hy_conv_b), k_all, v_all, w_branch_bf)
        hy = _hyena_call(l, filt, u, x0, hy_bias4, consts)
        x1, hp, aff, afft = _mixer_c_call(l, xs, mp, gh, hy, w_branch_bf, w_out_bf, row(norm2_g), wr_pad)
        idx = _route_call(afft, cap)
        xs = [x1, _moe_call(l, idx, hp, aff, w1_bf, w3_bf, w2_bf)]
    return _final_call(xs, final_g)
```

```python
import functools
import math

import numpy as np
import jax
import jax.numpy as jnp
from jax import lax
from jax.experimental import pallas as pl
from jax.experimental.pallas import tpu as pltpu

F32 = jnp.float32
BF16 = jnp.bfloat16

D_MODEL = 1024
SGU_WIDTH = 512
SGU_GROUPS = 4
SGU_CHUNK = 128
HY_WIDTH = 512
HY_EMB_DIM = 33
HY_BANDS = (HY_EMB_DIM - 1) // 2
HY_FILTER_HIDDEN = 64
HY_FAST_DECAY_PCT = 0.3
HY_SLOW_DECAY_PCT = 1.5
HY_DECAY_TARGET = 1e-2
XA_HEADS = 4
XA_HEAD_DIM = 128
XA_WIDTH = XA_HEADS * XA_HEAD_DIM
N_EXPERTS = 16
EC_CAPACITY = 2
NORM_EPS = 1e-6
LN_EPS = 1e-5

C_SGU = 2 * SGU_WIDTH
C_HY = C_SGU + 3 * HY_WIDTH
C_Q = C_HY + XA_WIDTH

LANES = 128
SUBLANES = 8
VMEM_LIMIT = 56 * 1024 * 1024

FFT_N2 = 128
FFT_ROWS = 256
HY_CT = LANES

MOE_ROW_CHUNK = 512


def _cparams(semantics, flags=None):
    return pltpu.CompilerParams(dimension_semantics=semantics, vmem_limit_bytes=VMEM_LIMIT, flags=flags)


def _rms(x, g):
    return x * lax.rsqrt(jnp.mean(x * x, axis=-1, keepdims=True) + NORM_EPS) * g


def _dot(a, b):
    return jnp.dot(a, b, preferred_element_type=F32)


def _dot_nt(a, b):
    return lax.dot_general(a, b, (((1,), (1,)), ((), ())), preferred_element_type=F32)


def _split_bf16(x):
    hi = x.astype(BF16)
    lo = (x - hi.astype(F32)).astype(BF16)
    return hi, lo


def _kv_body(mem_ref, g_ref, w_ref, k_ref, v_ref):
    m = _rms(mem_ref[...], g_ref[...]).astype(BF16)
    kv = _dot(m, w_ref[...])
    k_ref[...] = kv[:, :XA_WIDTH].astype(BF16)
    v_ref[...] = kv[:, XA_WIDTH:].astype(BF16)


def _kv_call(mem, mem_norm_g, w_kv_bf):
    B, M, D = mem.shape
    depth = w_kv_bf.shape[0]
    return pl.pallas_call(
        _kv_body,
        grid=(depth, B),
        in_specs=[
            pl.BlockSpec((None, M, D), lambda l, b: (b, 0, 0)),
            pl.BlockSpec((None, 1, D), lambda l, b: (l, 0, 0)),
            pl.BlockSpec((None, D, 2 * XA_WIDTH), lambda l, b: (l, 0, 0)),
        ],
        out_specs=[
            pl.BlockSpec((None, None, M, XA_WIDTH), lambda l, b: (l, b, 0, 0)),
            pl.BlockSpec((None, None, M, XA_WIDTH), lambda l, b: (l, b, 0, 0)),
        ],
        out_shape=[jax.ShapeDtypeStruct((depth, B, M, XA_WIDTH), BF16)] * 2,
        compiler_params=_cparams(("arbitrary", "arbitrary")),
        name="kv",
    )(mem, mem_norm_g.reshape(depth, 1, D), w_kv_bf)


FILT_GROUPS = LANES // HY_BANDS


def _filter_body(w1t_ref, w1c_ref, w1s_ref, b1_ref, w2_ref, b2_ref, w3_ref, b3_ref, w4_ref, fr_ref,
                 bands_ref, deltas_ref, o_ref, *, seq_len, rows):
    d = pl.program_id(1)
    r0 = pl.program_id(2) * rows
    ng, hid = FILT_GROUPS, HY_FILTER_HIDDEN
    rg = rows // ng

    def lags(width, per_group):
        ridx = (r0 + lax.broadcasted_iota(jnp.int32, (rg, width), 0)
                + (lax.broadcasted_iota(jnp.int32, (rg, width), 1) // per_group) * rg)
        return ridx, (ridx + d * (seq_len - 2 * ridx)).astype(F32)

    _, pos_b = lags(LANES, HY_BANDS)
    ridx, pos_h = lags(ng * hid, hid)
    t = pos_h * (1.0 / (seq_len - 1))
    ang = pos_b * (2.0 * math.pi / seq_len) * bands_ref[...]
    hp = lax.Precision.HIGHEST
    dot = lambda a, w_ref: jnp.dot(a, w_ref[...], precision=hp, preferred_element_type=F32)
    fr = fr_ref[...]
    h = jnp.sin(fr * (t * w1t_ref[...] + dot(jnp.cos(ang), w1c_ref) - dot(jnp.sin(ang), w1s_ref) + b1_ref[...]))
    h = jnp.sin(fr * (dot(h, w2_ref) + b2_ref[...]))
    h = jnp.sin(fr * (dot(h, w3_ref) + b3_ref[...]))
    for g in range(ng):
        og = dot(h[:, g * hid:(g + 1) * hid], w4_ref)
        og = og * jnp.exp(-t[:, g * hid:g * hid + 1] * deltas_ref[...])
        og = jnp.where((ridx[:, g * hid:g * hid + 1] == 0) & (d == 1), 0.0, og)
        for c in range(HY_WIDTH // HY_CT):
            o_ref[c, g * rg:(g + 1) * rg, :] = og[:, c * HY_CT:(c + 1) * HY_CT]


def _filter_call(seq_len, w1, b1, w2, b2, w3, b3, w4, freq):
    depth = w1.shape[0]
    H = HY_FILTER_HIDDEN
    ng = FILT_GROUPS
    rows = 1024
    nct = HY_WIDTH // HY_CT
    bands = np.tile(np.linspace(1e-4, HY_BANDS - 1, HY_BANDS), ng).astype(np.float32).reshape(1, LANES)
    max_decay = math.log(HY_DECAY_TARGET) / HY_FAST_DECAY_PCT
    min_decay = math.log(HY_DECAY_TARGET) / HY_SLOW_DECAY_PCT
    deltas = np.abs(np.linspace(min_decay, max_decay, HY_WIDTH)).astype(np.float32).reshape(1, HY_WIDTH)
    eye = jnp.eye(ng, dtype=F32)
    bdiag = lambda w: jnp.einsum("gh,lij->lgihj", eye, w).reshape(depth, ng * w.shape[1], ng * w.shape[2])
    tiled = lambda v: jnp.tile(v.reshape(depth, 1, H), (1, 1, ng))
    w1t = tiled(w1[:, 0, :])
    w1c = bdiag(w1[:, 1:1 + HY_BANDS, :])
    w1s = bdiag(w1[:, 1 + HY_BANDS:, :])
    vec = lambda n: pl.BlockSpec((None, 1, n), lambda l, d, r: (l, 0, 0))
    mat = lambda m, n: pl.BlockSpec((None, m, n), lambda l, d, r: (l, 0, 0))
    gh = ng * H
    return pl.pallas_call(
        functools.partial(_filter_body, seq_len=seq_len, rows=rows),
        grid=(depth, 2, seq_len // rows),
        in_specs=[
            vec(gh), mat(LANES, gh), mat(LANES, gh), vec(gh), mat(gh, gh), vec(gh), mat(gh, gh), vec(gh),
            pl.BlockSpec((None, H, HY_WIDTH), lambda l, d, r: (l, 0, d)),
            vec(gh),
            pl.BlockSpec((1, LANES), lambda l, d, r: (0, 0)),
            pl.BlockSpec((1, HY_WIDTH), lambda l, d, r: (0, 0)),
        ],
        out_specs=pl.BlockSpec((None, None, nct, rows, HY_CT), lambda l, d, r: (l, d, 0, r, 0)),
        out_shape=jax.ShapeDtypeStruct((depth, 2, nct, seq_len, HY_CT), F32),
        compiler_params=_cparams(("arbitrary", "arbitrary", "arbitrary")),
        name="hyena_filter",
    )(w1t, w1c, w1s, tiled(b1), bdiag(w2), tiled(b2), bdiag(w3), tiled(b3), w4, tiled(freq),
      jnp.asarray(bands), jnp.asarray(deltas))


def _mixer_a_body(*refs, tm, n_x):
    x_refs, refs = refs[:3 * n_x], refs[3 * n_x:]
    (g1_ref, win_ref, gb_ref, lng_ref, lnb_ref, sw_ref, sbt_ref, cw_ref, cb_ref, k_ref, v_ref, wba_ref, wbc_ref,
     we1_ref, we3_ref, we2_ref, mp_ref, gh_ref, x0_ref, u_ref, wb1_ref, wb3_ref, wb2_ref) = refs
    for src, dst in ((we1_ref, wb1_ref), (we3_ref, wb3_ref), (we2_ref, wb2_ref)):
        dst[...] = src[...].astype(BF16)
    x_cur = sum(r[...] for r in x_refs[0::3])
    x_prev = sum(r[...] for r in x_refs[1::3])
    x_next = sum(r[...] for r in x_refs[2::3])
    j = pl.program_id(1)
    nj = pl.num_programs(1)
    g1 = g1_ref[...]
    h = _rms(x_cur, g1)
    hb = h.astype(BF16)

    zs = _dot(hb, win_ref[:, 0:C_SGU])
    zs = 0.5 * zs * (1.0 + lax.erf(zs * (1.0 / math.sqrt(2.0))))
    us = zs[:, :SGU_WIDTH]
    vs = zs[:, SGU_WIDTH:]
    mu = jnp.mean(vs, axis=-1, keepdims=True)
    vc = vs - mu
    var = jnp.mean(vc * vc, axis=-1, keepdims=True)
    vn = (vc * lax.rsqrt(var + LN_EPS) * lng_ref[...] + lnb_ref[...]).astype(BF16)
    gd = SGU_WIDTH // SGU_GROUPS
    chunks = []
    for c in range(tm // SGU_CHUNK):
        groups = []
        for g in range(SGU_GROUPS):
            blk = vn[c * SGU_CHUNK:(c + 1) * SGU_CHUNK, g * gd:(g + 1) * gd]
            groups.append(_dot(sw_ref[g], blk))
        chunks.append(jnp.concatenate(groups, axis=1) + sbt_ref[...])
    a = us * jnp.concatenate(chunks, axis=0)

    q = _dot(hb, win_ref[:, C_HY:C_Q])
    scale = 1.0 / math.sqrt(XA_HEAD_DIM)
    heads = []
    for hd in range(XA_HEADS):
        sl = slice(hd * XA_HEAD_DIM, (hd + 1) * XA_HEAD_DIM)
        s = _dot_nt(q[:, sl].astype(BF16), k_ref[:, sl]) * scale
        s = s - jnp.max(s, axis=-1, keepdims=True)
        p = jnp.exp(s)
        p = p / jnp.sum(p, axis=-1, keepdims=True)
        heads.append(_dot(p.astype(BF16), v_ref[:, sl]))
    c_att = jnp.concatenate(heads, axis=1)

    gates = jax.nn.sigmoid(_dot(hb, win_ref[:, C_Q:]) + gb_ref[...])
    pa = _dot(a.astype(BF16), wba_ref[...])
    pc = _dot(c_att.astype(BF16), wbc_ref[...])
    mp_ref[...] = (gates[:, 0:D_MODEL] * pa + gates[:, 2 * D_MODEL:] * pc).astype(BF16)
    gh_ref[...] = gates[:, D_MODEL:2 * D_MODEL].astype(BF16)

    hprev = _rms(x_prev, g1).astype(BF16)
    hnext = _rms(x_next, g1).astype(BF16)
    hext = jnp.concatenate([hprev, hb, hnext], axis=0)
    pe = _dot(hext, win_ref[:, C_SGU:C_HY])
    rows = lax.broadcasted_iota(jnp.int32, (tm, 1), 0)
    pm1 = jnp.where((rows == 0) & (j == 0), 0.0, pe[SUBLANES - 1:SUBLANES - 1 + tm])
    pp1 = jnp.where((rows == tm - 1) & (j == nj - 1), 0.0, pe[SUBLANES + 1:SUBLANES + 1 + tm])
    p0 = pe[SUBLANES:SUBLANES + tm]
    pconv = cw_ref[0:1, :] * pm1 + cw_ref[1:2, :] * p0 + cw_ref[2:3, :] * pp1 + cb_ref[...]
    x0 = pconv[:, :HY_WIDTH]
    u = pconv[:, HY_WIDTH:2 * HY_WIDTH] * pconv[:, 2 * HY_WIDTH:]
    for c in range(HY_WIDTH // HY_CT):
        x0_ref[c] = x0[:, c * HY_CT:(c + 1) * HY_CT]
        u_ref[c] = u[:, c * HY_CT:(c + 1) * HY_CT]


def _mixer_a_call(l, xs, norm1_g, w_in_bf, gate_b, sgu_ln_g, sgu_ln_b, sgu_w_bf, sgu_bt, conv_w, conv_b,
                  k_all, v_all, w_branch_bf, expert_ws):
    B, L, D = xs[0].shape
    tm = 512
    nct = HY_WIDTH // HY_CT
    nsteps = B * (L // tm)
    e_rows = expert_ws[0].shape[1] * expert_ws[0].shape[2]
    e_cols = expert_ws[0].shape[3]
    assert all(w.shape[1] * w.shape[2] == e_rows and w.shape[3] == e_cols for w in expert_ws)
    slab = e_rows // nsteps
    assert slab * nsteps == e_rows and slab % (2 * SUBLANES) == 0
    e_flat = [w.reshape(w.shape[0], e_rows, e_cols) for w in expert_ws]
    step = lambda b, j: b * (L // tm) + j
    in_cols = w_in_bf.shape[-1]
    hb_per_tile = tm // SUBLANES
    n_hb = L // SUBLANES
    const = lambda shape: pl.BlockSpec((None,) + shape, lambda b, j: (l,) + (0,) * len(shape),
                                       pipeline_mode=pl.Buffered(1))
    x_specs = [
        pl.BlockSpec((None, tm, D), lambda b, j: (b, j, 0)),
        pl.BlockSpec((None, SUBLANES, D), lambda b, j: (b, jnp.maximum(j * hb_per_tile - 1, 0), 0)),
        pl.BlockSpec((None, SUBLANES, D), lambda b, j: (b, jnp.minimum((j + 1) * hb_per_tile, n_hb - 1), 0)),
    ]
    x_args = [a for x in xs for a in (x, x, x)]
    return pl.pallas_call(
        functools.partial(_mixer_a_body, tm=tm, n_x=len(xs)),
        grid=(B, L // tm),
        in_specs=x_specs * len(xs) + [
            const((1, D)),
            const((D, in_cols)),
            const((1, 3 * D)),
            const((1, SGU_WIDTH)),
            const((1, SGU_WIDTH)),
            const((SGU_GROUPS, SGU_CHUNK, SGU_CHUNK)),
            const((SGU_CHUNK, SGU_WIDTH)),
            const((3, 3 * HY_WIDTH)),
            const((1, 3 * HY_WIDTH)),
            pl.BlockSpec((None, None, k_all.shape[2], XA_WIDTH), lambda b, j: (l, b, 0, 0)),
            pl.BlockSpec((None, None, k_all.shape[2], XA_WIDTH), lambda b, j: (l, b, 0, 0)),
            pl.BlockSpec((None, None, SGU_WIDTH, D), lambda b, j: (l, 0, 0, 0), pipeline_mode=pl.Buffered(1)),
            pl.BlockSpec((None, None, XA_WIDTH, D), lambda b, j: (l, 2, 0, 0), pipeline_mode=pl.Buffered(1)),
        ] + [pl.BlockSpec((None, slab, e_cols), lambda b, j: (l, step(b, j), 0))] * len(e_flat),
        out_specs=[
            pl.BlockSpec((None, tm, D), lambda b, j: (b, j, 0)),
            pl.BlockSpec((None, tm, D), lambda b, j: (b, j, 0)),
            pl.BlockSpec((None, nct, tm, HY_CT), lambda b, j: (b, 0, j, 0)),
            pl.BlockSpec((None, nct, tm, HY_CT), lambda b, j: (b, 0, j, 0)),
        ] + [pl.BlockSpec((slab, e_cols), lambda b, j: (step(b, j), 0))] * len(e_flat),
        out_shape=[
            jax.ShapeDtypeStruct((B, L, D), BF16),
            jax.ShapeDtypeStruct((B, L, D), BF16),
            jax.ShapeDtypeStruct((B, nct, L, HY_CT), F32),
            jax.ShapeDtypeStruct((B, nct, L, HY_CT), F32),
        ] + [jax.ShapeDtypeStruct((e_rows, e_cols), BF16)] * len(e_flat),
        compiler_params=_cparams(("arbitrary", "arbitrary")),
        name="mixer_a",
    )(*x_args, norm1_g, w_in_bf, gate_b, sgu_ln_g, sgu_ln_b, sgu_w_bf, sgu_bt, conv_w, conv_b,
      k_all, v_all, w_branch_bf, w_branch_bf, *e_flat)


def _fft_constants(seq_len):
    n = 2 * seq_len
    n2 = FFT_N2
    n1 = n // n2
    nz = seq_len // n2
    lo = FFT_ROWS // nz
    tiles = n2 // lo
    k1n = n1 // 2 + 1
    assert lo == SUBLANES and nz * lo == FFT_ROWS
    k1 = np.arange(k1n).reshape(k1n, 1, 1, 1, 1).astype(np.float64)
    part = np.arange(2).reshape(1, 2, 1, 1, 1)
    a = np.arange(lo).reshape(1, 1, lo, 1, 1)
    m1 = np.arange(nz).reshape(1, 1, 1, nz, 1).astype(np.float64)
    a2 = np.arange(lo).reshape(1, 1, 1, 1, lo)
    fwd = np.zeros((tiles, k1n, 2, lo, nz, lo), np.float64)
    for j in range(tiles):
        theta = 2.0 * np.pi * (m1 * k1 / n1 + (lo * j + a) * k1 / n)
        val = np.where(part == 0, np.cos(theta), -np.sin(theta))
        fwd[j] = val * (a == a2)
    fwd = fwd.reshape(tiles, k1n * 2 * lo, nz * lo)
    weight = np.where((np.arange(k1n) == 0) | (np.arange(k1n) == n1 // 2), 1.0, 2.0) / n
    inv = fwd.reshape(tiles, k1n, 2 * lo, nz * lo) * weight.reshape(1, k1n, 1, 1)
    inv = inv.reshape(tiles, k1n * 2 * lo, nz * lo).transpose(0, 2, 1)
    phi = 2.0 * np.pi * np.outer(np.arange(n2), np.arange(n2)) / n2
    c, s = np.cos(phi), np.sin(phi)
    g_fwd = np.block([[c, s], [-s, c]])
    g_inv = np.block([[c, -s], [s, c]])
    f32 = lambda v: jnp.asarray(v.astype(np.float32)).astype(BF16)
    return f32(fwd), f32(inv), f32(g_fwd), f32(g_inv), (tiles, k1n, lo, nz)


def _hyena_body(filt_ref, u_ref, x0_ref, bias_ref, fa_ref, ia_ref, gf_ref, gi_ref, o_ref, z_ref, k_ref,
                *, tiles, k1n, lo, nz):
    s = pl.program_id(1)
    half = FFT_N2
    rows = nz * lo

    def stage_a(src_ref):
        for j in range(tiles):
            tile = src_ref[:, j].reshape(rows, LANES).astype(BF16)
            z_ref[:, :, j] = _dot(fa_ref[j], tile).reshape(k1n, 2, lo, LANES)

    def spectrum(k1):
        return _dot(gf_ref[...], z_ref[k1].reshape(2 * half, LANES).astype(BF16))

    @pl.when(s == 0)
    def _():
        k_ref[...] = jnp.zeros(k_ref.shape, F32)

    @pl.when(s < 2)
    def _():
        stage_a(filt_ref)
        odd_sign = jnp.where(s == 0, 1.0, -1.0).astype(F32)
        for k1 in range(k1n):
            x = spectrum(k1)
            k_ref[k1] += x if k1 % 2 == 0 else odd_sign * x

    @pl.when(s >= 2)
    def _():
        stage_a(u_ref)
        for k1 in range(k1n):
            x = spectrum(k1)
            xr, xi = x[:half], x[half:]
            kr, ki = k_ref[k1, :half], k_ref[k1, half:]
            y = jnp.concatenate([xr * kr - xi * ki, xr * ki + xi * kr], axis=0).astype(BF16)
            z_ref[k1] = _dot(gi_ref[...], y).reshape(2, tiles, lo, LANES)
        bias = bias_ref[...]
        for j in range(tiles):
            zt = z_ref[:, :, j].reshape(k1n * 2 * lo, LANES).astype(BF16)
            y = _dot(ia_ref[j], zt)
            uu = u_ref[:, j].reshape(rows, LANES)
            x0 = x0_ref[:, j].reshape(rows, LANES)
            o_ref[:, j] = (x0 * (y + bias * uu)).reshape(nz, lo, LANES)


def _hyena_call(l, filt, u, x0, hy_bias, consts):
    fa, ia, gf, gi, (tiles, k1n, lo, nz) = consts
    B, nct, L, _ = u.shape
    view = lambda a: a.reshape(a.shape[:-2] + (nz, tiles, lo, LANES))
    seq = (nz, tiles, lo, LANES)
    bidx = lambda s: jnp.maximum(s - 2, 0)
    single = lambda shape: pl.BlockSpec(shape, lambda c, s: (0,) * len(shape), pipeline_mode=pl.Buffered(1))
    out = pl.pallas_call(
        functools.partial(_hyena_body, tiles=tiles, k1n=k1n, lo=lo, nz=nz),
        grid=(nct, B + 2),
        in_specs=[
            pl.BlockSpec((None, None, None) + seq, lambda c, s: (l, jnp.minimum(s, 1), c, 0, 0, 0, 0)),
            pl.BlockSpec((None, None) + seq, lambda c, s: (bidx(s), c, 0, 0, 0, 0)),
            pl.BlockSpec((None, None) + seq, lambda c, s: (bidx(s), c, 0, 0, 0, 0)),
            pl.BlockSpec((None, None, 1, LANES), lambda c, s: (l, c, 0, 0)),
            single(fa.shape), single(ia.shape), single(gf.shape), single(gi.shape),
        ],
        out_specs=pl.BlockSpec((None, None) + seq, lambda c, s: (bidx(s), c, 0, 0, 0, 0)),
        out_shape=jax.ShapeDtypeStruct((B, nct) + seq, F32),
        scratch_shapes=[
            pltpu.VMEM((k1n, 2, tiles, lo, LANES), F32),
            pltpu.VMEM((k1n, 2 * FFT_N2, LANES), F32),
        ],
        compiler_params=_cparams(("arbitrary", "arbitrary")),
        name="hyena_fft",
    )(view(filt), view(u), view(x0), hy_bias, fa, ia, gf, gi)
    return out.reshape(B, nct, L, HY_CT)


def _pack_bf16_pairs(x):
    n = x.shape[1] // 2
    return pltpu.pack_elementwise([x[:, :n], x[:, n:]], packed_dtype=BF16)


def _unpack_bf16_pairs(p):
    lo = pltpu.unpack_elementwise(p, index=0, packed_dtype=BF16, unpacked_dtype=F32)
    hi = pltpu.unpack_elementwise(p, index=1, packed_dtype=BF16, unpacked_dtype=F32)
    return jnp.concatenate([lo, hi], axis=1).astype(BF16)


def _mixer_c_body(*refs, n_x):
    x_refs, refs = refs[:n_x], refs[n_x:]
    (mp_ref, gh_ref, hy_ref, wbh_ref, wo_ref, g2_ref, wr_ref, x1_ref, hp_ref, aff_ref, afft_ref) = refs
    hy = jnp.concatenate([hy_ref[c] for c in range(HY_WIDTH // HY_CT)], axis=1).astype(BF16)
    merged = mp_ref[...].astype(F32) + gh_ref[...].astype(F32) * _dot(hy, wbh_ref[...])
    x1 = sum(r[...] for r in x_refs) + _dot(merged.astype(BF16), wo_ref[...])
    x1_ref[...] = x1
    h2 = _rms(x1, g2_ref[...])
    hp_ref[...] = _pack_bf16_pairs(h2)
    tm = h2.shape[0]
    hh, hl = _split_bf16(h2)
    wh, wl = _split_bf16(wr_ref[...])
    prod = _dot(jnp.concatenate([hh, hl], axis=0), jnp.concatenate([wh, wl], axis=1))
    logits = prod[:tm, :LANES] + prod[:tm, LANES:] + prod[tm:, :LANES]
    lane = lax.broadcasted_iota(jnp.int32, logits.shape, 1)
    logits = jnp.where(lane < N_EXPERTS, logits, -jnp.inf)
    e = jnp.exp(logits - jnp.max(logits, axis=-1, keepdims=True))
    aff = e / jnp.sum(e, axis=-1, keepdims=True)
    aff_ref[...] = aff
    afft_ref[...] = aff.T[:N_EXPERTS, :]


def _mixer_c_call(l, xs, mp, gh, hy, w_branch_bf, w_out_bf, norm2_g, wr_pad):
    B, L, D = xs[0].shape
    tm = 512
    nct = HY_WIDTH // HY_CT
    const = lambda shape: pl.BlockSpec((None,) + shape, lambda b, j: (l,) + (0,) * len(shape),
                                       pipeline_mode=pl.Buffered(1))
    return pl.pallas_call(
        functools.partial(_mixer_c_body, n_x=len(xs)),
        grid=(B, L // tm),
        in_specs=[pl.BlockSpec((None, tm, D), lambda b, j: (b, j, 0))] * len(xs) + [
            pl.BlockSpec((None, tm, D), lambda b, j: (b, j, 0)),
            pl.BlockSpec((None, tm, D), lambda b, j: (b, j, 0)),
            pl.BlockSpec((None, nct, tm, HY_CT), lambda b, j: (b, 0, j, 0)),
            pl.BlockSpec((None, None, HY_WIDTH, D), lambda b, j: (l, 1, 0, 0), pipeline_mode=pl.Buffered(1)),
            const((D, D)),
            const((1, D)),
            const((D, LANES)),
        ],
        out_specs=[
            pl.BlockSpec((None, tm, D), lambda b, j: (b, j, 0)),
            pl.BlockSpec((None, tm, D // 2), lambda b, j: (b, j, 0)),
            pl.BlockSpec((None, tm, LANES), lambda b, j: (b, j, 0)),
            pl.BlockSpec((None, N_EXPERTS, tm), lambda b, j: (b, 0, j)),
        ],
        out_shape=[
            jax.ShapeDtypeStruct((B, L, D), F32),
            jax.ShapeDtypeStruct((B, L, D // 2), jnp.uint32),
            jax.ShapeDtypeStruct((B, L, LANES), F32),
            jax.ShapeDtypeStruct((B, N_EXPERTS, L), F32),
        ],
        compiler_params=_cparams(("arbitrary", "arbitrary")),
        name="mixer_c",
    )(*xs, mp, gh, hy, w_branch_bf, w_out_bf, norm2_g, wr_pad)


def _cumsum_lanes(x, tri):
    outs = []
    carry = jnp.zeros((x.shape[0], 1), F32)
    for blk in range(x.shape[1] // LANES):
        loc = _dot(x[:, blk * LANES:(blk + 1) * LANES].astype(BF16), tri) + carry
        outs.append(loc)
        carry = loc[:, LANES - 1:LANES]
    return jnp.concatenate(outs, axis=1)


def _route_body(afft_ref, tri_ref, idx_ref, sel_ref, thr_ref, *, cap):
    b = pl.program_id(0)
    nb, ne, seq = afft_ref.shape

    @pl.when(b == 0)
    def _():
        a_all = afft_ref[...].reshape(nb * ne, seq)

        def bit_step(i, v):
            cand = v | jnp.left_shift(jnp.int32(1), 30 - i)
            cnt = jnp.sum((a_all >= lax.bitcast_convert_type(cand, F32)).astype(F32), axis=1, keepdims=True)
            return jnp.where(cnt >= cap, cand, v)

        bits = lax.fori_loop(0, 31, bit_step, jnp.zeros((nb * ne, 1), jnp.int32))
        thr_ref[...] = jnp.broadcast_to(lax.bitcast_convert_type(bits, F32), thr_ref.shape)

    a = afft_ref[b]
    thr = thr_ref[pl.ds(pl.multiple_of(b * ne, ne), ne), 0:1]
    gt = a > thr
    eq = a == thr
    need = cap - jnp.sum(gt.astype(F32), axis=1, keepdims=True)
    tri = tri_ref[...]
    cum_eq = _cumsum_lanes(eq.astype(F32), tri)
    sel = gt | (eq & (cum_eq <= need))
    sel_ref[...] = sel.astype(F32)

    nblk = seq // LANES
    f32c = lambda v: jnp.where(v, 1.0, 0.0)
    lane = lax.broadcasted_iota(jnp.int32, (cap, LANES), 1)
    slot = lax.broadcasted_iota(jnp.int32, (cap, LANES), 0).astype(F32)
    ones_row = jnp.ones((SUBLANES, LANES), BF16)
    bi = lax.broadcasted_iota(jnp.int32, (nblk, LANES), 0)
    bj = lax.broadcasted_iota(jnp.int32, (nblk, LANES), 1)
    upper = f32c((bi <= bj) | (bj >= nblk)).astype(BF16)
    lower = f32c(bj < bi)[:, :nblk].astype(BF16)
    half = float(cap // 2)
    experts = range(ne)
    selb = [jnp.concatenate([sel_ref[e:e + 1, j * LANES:(j + 1) * LANES] for j in range(nblk)], axis=0).astype(BF16)
            for e in experts]
    loc = [_dot(selb[e], tri) for e in experts]
    tot_row = [_dot_nt(ones_row, selb[e])[:, :nblk].astype(BF16) for e in experts]
    incl_row = [_dot(tot_row[e], upper)[0:1, :] for e in experts]
    excl = [_dot(lower, jnp.broadcast_to(loc[e][:, LANES - 1:LANES], (nblk, LANES)).astype(BF16)) for e in experts]
    jb = [jnp.sum(f32c(incl_row[e] <= slot), axis=1, keepdims=True) for e in experts]
    onehot = [f32c(jb[e].astype(jnp.int32) == lane).astype(BF16) for e in experts]
    pad = jnp.zeros((LANES - nblk, 3 * LANES), F32)
    table = [jnp.concatenate([jnp.concatenate([loc[e], jnp.minimum(excl[e], half),
                                               jnp.maximum(excl[e] - half, 0.0)], axis=1), pad], axis=0).astype(BF16)
             for e in experts]
    got = [_dot(onehot[e], table[e]) for e in experts]
    idx_t = jnp.zeros((cap, LANES), F32)
    for e in experts:
        local = slot - (got[e][:, LANES:2 * LANES] + got[e][:, 2 * LANES:])
        pos = jnp.sum(f32c(got[e][:, :LANES] <= local), axis=1, keepdims=True)
        idx_t = idx_t + jnp.where(lane == e, jb[e] * float(LANES) + pos, 0.0)
    idx_ref[...] = idx_t.T[:ne, :].astype(jnp.int32)


def _route_call(afft, cap):
    B, ne, L = afft.shape
    tri = jnp.asarray(np.triu(np.ones((LANES, LANES), np.float32))).astype(BF16)
    return pl.pallas_call(
        functools.partial(_route_body, cap=cap),
        grid=(B,),
        in_specs=[
            pl.BlockSpec((B, ne, L), lambda b: (0, 0, 0)),
            pl.BlockSpec((LANES, LANES), lambda b: (0, 0)),
        ],
        out_specs=pl.BlockSpec((None, ne, cap), lambda b: (b, 0, 0)),
        out_shape=jax.ShapeDtypeStruct((B, ne, cap), jnp.int32),
        scratch_shapes=[pltpu.VMEM((ne, L), F32), pltpu.VMEM((B * ne, LANES), F32)],
        compiler_params=_cparams(("arbitrary",)),
        name="route",
    )(afft, tri)


def _moe_body(idx_ref, idxp_ref, hp_ref, aff_ref, w1_ref, w3_ref, w2_ref, out_hbm,
              acc_ref, xin_ref, gsel_ref, y_ref, sem_ref, *, cap, ne, npairs):
    i = pl.program_id(0)
    slot = i % 2
    e = jnp.minimum(i, npairs - 1) % ne

    @pl.when(i == 0)
    def _():
        acc_ref[...] = jnp.zeros(acc_ref.shape, F32)
        y_ref[...] = jnp.zeros(y_ref.shape, F32)

    y_prev = y_ref.at[1 - slot]
    for g0 in range(0, cap, SUBLANES):
        toks = [idxp_ref[0, 0, g0 + k] for k in range(SUBLANES)]
        y_tile = y_prev[pl.ds(g0, SUBLANES), :]
        vals = [acc_ref[pl.ds(toks[k], 1), :] + y_tile[k:k + 1, :] for k in range(SUBLANES)]
        for k in range(SUBLANES):
            acc_ref[pl.ds(toks[k], 1), :] = vals[k]

    for s in range(cap):
        t = idx_ref[0, 0, s]
        xin_ref[pl.ds(s, 1), :] = hp_ref[pl.ds(t, 1), :]
        gsel_ref[pl.ds(s, 1), :] = aff_ref[pl.ds(t, 1), :]

    aff = gsel_ref[...]
    lane = lax.broadcasted_iota(jnp.int32, aff.shape, 1)
    gate = jnp.sum(jnp.where(lane == e, aff, 0.0), axis=1, keepdims=True)
    for r0 in range(0, cap, MOE_ROW_CHUNK):
        rs = slice(r0, r0 + MOE_ROW_CHUNK)
        xb = _unpack_bf16_pairs(xin_ref[rs, :])
        a1 = _dot(xb, w1_ref[...])
        a3 = _dot(xb, w3_ref[...])
        hid = (a1 * jax.nn.sigmoid(a1) * a3).astype(BF16)
        y_ref[slot, rs, :] = _dot(hid, w2_ref[...]) * gate[rs]

    @pl.when((i > 0) & (i % ne == 0))
    def _():
        cp = pltpu.make_async_copy(acc_ref, out_hbm.at[i // ne - 1], sem_ref.at[0])
        cp.start()
        cp.wait()
        acc_ref[...] = jnp.zeros(acc_ref.shape, F32)


def _moe_call(idx, hp, aff, w1_rows, w3_rows, w2_rows):
    B, ne, cap = idx.shape
    L = hp.shape[1]
    ff = w1_rows.shape[-1]
    w1_bf, w3_bf = (w.reshape(ne, D_MODEL, ff) for w in (w1_rows, w3_rows))
    w2_bf = w2_rows.reshape(ne, ff, D_MODEL)
    npairs = B * ne
    idx3 = idx.reshape(npairs, 1, cap)
    pair = lambda i: jnp.minimum(i, npairs - 1)
    prev = lambda i: jnp.maximum(i - 1, 0)
    return pl.pallas_call(
        functools.partial(_moe_body, cap=cap, ne=ne, npairs=npairs),
        grid=(npairs + 1,),
        in_specs=[
            pl.BlockSpec((1, 1, cap), lambda i: (pair(i), 0, 0), memory_space=pltpu.SMEM),
            pl.BlockSpec((1, 1, cap), lambda i: (prev(i), 0, 0), memory_space=pltpu.SMEM),
            pl.BlockSpec((None, L, D_MODEL // 2), lambda i: (pair(i) // ne, 0, 0), pipeline_mode=pl.Buffered(1)),
            pl.BlockSpec((None, L, LANES), lambda i: (pair(i) // ne, 0, 0), pipeline_mode=pl.Buffered(1)),
            pl.BlockSpec((None, D_MODEL, ff), lambda i: (pair(i) % ne, 0, 0)),
            pl.BlockSpec((None, D_MODEL, ff), lambda i: (pair(i) % ne, 0, 0)),
            pl.BlockSpec((None, ff, D_MODEL), lambda i: (pair(i) % ne, 0, 0)),
        ],
        out_specs=pl.BlockSpec(memory_space=pl.ANY),
        out_shape=jax.ShapeDtypeStruct((B, L, D_MODEL), F32),
        scratch_shapes=[
            pltpu.VMEM((L, D_MODEL), F32),
            pltpu.VMEM((cap, D_MODEL // 2), jnp.uint32),
            pltpu.VMEM((cap, LANES), F32),
            pltpu.VMEM((2, cap, D_MODEL), F32),
            pltpu.SemaphoreType.DMA((1,)),
        ],
        compiler_params=_cparams(("arbitrary",)),
        name="moe",
    )(idx3, idx3, hp, aff, w1_bf, w3_bf, w2_bf)


def _final_body(*refs):
    g_ref, o_ref = refs[-2:]
    o_ref[...] = _rms(sum(r[...] for r in refs[:-2]), g_ref[...])


def _final_call(xs, g):
    B, L, D = xs[0].shape
    tm = 1024
    return pl.pallas_call(
        _final_body,
        grid=(B, L // tm),
        in_specs=[pl.BlockSpec((None, tm, D), lambda b, j: (b, j, 0))] * len(xs)
        + [pl.BlockSpec((1, D), lambda b, j: (0, 0))],
        out_specs=pl.BlockSpec((None, tm, D), lambda b, j: (b, j, 0)),
        out_shape=jax.ShapeDtypeStruct((B, L, D), F32),
        compiler_params=_cparams(("arbitrary", "arbitrary")),
        name="final_norm",
    )(*xs, g.reshape(1, D))


def kernel(x, mem, norm1_g, w_in, gate_b, sgu_ln_g, sgu_ln_b, sgu_w, sgu_b, hy_conv_w, hy_conv_b, hy_f_w1, hy_f_b1, hy_f_w2, hy_f_b2, hy_f_w3, hy_f_b3, hy_f_w4, hy_f_freq, hy_bias, mem_norm_g, w_kv, w_branch, w_out, norm2_g, w_router, w1, w3, w2, final_g):
    B, L, D = x.shape
    depth = w_in.shape[0]
    cap = EC_CAPACITY * L // N_EXPERTS
    nct = HY_WIDTH // HY_CT
    row = lambda a: a.reshape(depth, 1, a.shape[-1])

    w_in_bf = w_in.astype(BF16)
    w_kv_bf = w_kv.astype(BF16)
    w_branch_bf = w_branch.astype(BF16)
    w_out_bf = w_out.astype(BF16)
    sgu_w_bf = sgu_w.astype(BF16)
    sgu_bt = jnp.repeat(jnp.swapaxes(sgu_b, 1, 2), SGU_WIDTH // SGU_GROUPS, axis=2)
    wr_pad = jnp.pad(w_router, ((0, 0), (0, 0), (0, LANES - N_EXPERTS)))
    hy_bias4 = hy_bias.reshape(depth, nct, 1, LANES)

    k_all, v_all = _kv_call(mem, mem_norm_g, w_kv_bf)
    filt = _filter_call(L, hy_f_w1, hy_f_b1, hy_f_w2, hy_f_b2, hy_f_w3, hy_f_b3, hy_f_w4, hy_f_freq)
    consts = _fft_constants(L)

    xs = [x]
    for l in range(depth):
        mp, gh, x0, u, w1_bf, w3_bf, w2_bf = _mixer_a_call(
            l, xs, row(norm1_g), w_in_bf, row(gate_b), row(sgu_ln_g), row(sgu_ln_b), sgu_w_bf, sgu_bt,
            hy_conv_w, row(hy_conv_b), k_all, v_all, w_branch_bf, (w1, w3, w2))
        hy = _hyena_call(l, filt, u, x0, hy_bias4, consts)
        x1, hp, aff, afft = _mixer_c_call(l, xs, mp, gh, hy, w_branch_bf, w_out_bf, row(norm2_g), wr_pad)
        idx = _route_call(afft, cap)
        xs = [x1, _moe_call(idx, hp, aff, w1_bf, w3_bf, w2_bf)]
    return _final_call(xs, final_g)
```

```python
import functools
import math

import numpy as np
import jax
import jax.numpy as jnp
from jax import lax
from jax.experimental import pallas as pl
from jax.experimental.pallas import tpu as pltpu

F32 = jnp.float32
BF16 = jnp.bfloat16

D_MODEL = 1024
SGU_WIDTH = 512
SGU_GROUPS = 4
SGU_CHUNK = 128
HY_WIDTH = 512
HY_EMB_DIM = 33
HY_BANDS = (HY_EMB_DIM - 1) // 2
HY_FILTER_HIDDEN = 64
HY_FAST_DECAY_PCT = 0.3
HY_SLOW_DECAY_PCT = 1.5
HY_DECAY_TARGET = 1e-2
XA_HEADS = 4
XA_HEAD_DIM = 128
XA_WIDTH = XA_HEADS * XA_HEAD_DIM
N_EXPERTS = 16
EC_CAPACITY = 2
NORM_EPS = 1e-6
LN_EPS = 1e-5

C_SGU = 2 * SGU_WIDTH
C_HY = C_SGU + 3 * HY_WIDTH
C_Q = C_HY + XA_WIDTH

LANES = 128
SUBLANES = 8
VMEM_LIMIT = 56 * 1024 * 1024

FFT_N2 = 128
FFT_ROWS = 256
HY_CT = 256

MOE_RMW_ROWS = 8


def _cparams(semantics, flags=None):
    return pltpu.CompilerParams(dimension_semantics=semantics, vmem_limit_bytes=VMEM_LIMIT, flags=flags)


def _rms(x, g):
    return x * lax.rsqrt(jnp.mean(x * x, axis=-1, keepdims=True) + NORM_EPS) * g


def _dot(a, b):
    return jnp.dot(a, b, preferred_element_type=F32)


def _dot_nt(a, b):
    return lax.dot_general(a, b, (((1,), (1,)), ((), ())), preferred_element_type=F32)


def _split_bf16(x):
    hi = x.astype(BF16)
    lo = (x - hi.astype(F32)).astype(BF16)
    return hi, lo


def _kv_body(mem_ref, g_ref, w_ref, k_ref, v_ref):
    m = _rms(mem_ref[...], g_ref[...]).astype(BF16)
    kv = _dot(m, w_ref[...])
    k_ref[...] = kv[:, :XA_WIDTH].astype(BF16)
    v_ref[...] = kv[:, XA_WIDTH:].astype(BF16)


def _kv_call(mem, mem_norm_g, w_kv_bf):
    B, M, D = mem.shape
    depth = w_kv_bf.shape[0]
    return pl.pallas_call(
        _kv_body,
        grid=(depth, B),
        in_specs=[
            pl.BlockSpec((None, M, D), lambda l, b: (b, 0, 0)),
            pl.BlockSpec((None, 1, D), lambda l, b: (l, 0, 0)),
            pl.BlockSpec((None, D, 2 * XA_WIDTH), lambda l, b: (l, 0, 0)),
        ],
        out_specs=[
            pl.BlockSpec((None, None, M, XA_WIDTH), lambda l, b: (l, b, 0, 0)),
            pl.BlockSpec((None, None, M, XA_WIDTH), lambda l, b: (l, b, 0, 0)),
        ],
        out_shape=[jax.ShapeDtypeStruct((depth, B, M, XA_WIDTH), BF16)] * 2,
        compiler_params=_cparams(("arbitrary", "arbitrary")),
        name="kv",
    )(mem, mem_norm_g.reshape(depth, 1, D), w_kv_bf)


FILT_GROUPS = LANES // HY_BANDS


def _filter_body(w1t_ref, w1c_ref, w1s_ref, b1_ref, w2_ref, b2_ref, w3_ref, b3_ref, w4_ref, fr_ref,
                 bands_ref, deltas_ref, o_ref, *, seq_len, rows):
    d = pl.program_id(1)
    r0 = pl.program_id(2) * rows
    ng, hid = FILT_GROUPS, HY_FILTER_HIDDEN
    rg = rows // ng

    def lags(width, per_group):
        ridx = (r0 + lax.broadcasted_iota(jnp.int32, (rg, width), 0)
                + (lax.broadcasted_iota(jnp.int32, (rg, width), 1) // per_group) * rg)
        return ridx, (ridx + d * (seq_len - 2 * ridx)).astype(F32)

    _, pos_b = lags(LANES, HY_BANDS)
    ridx, pos_h = lags(ng * hid, hid)
    t = pos_h * (1.0 / (seq_len - 1))
    ang = pos_b * (2.0 * math.pi / seq_len) * bands_ref[...]
    hp = lax.Precision.HIGHEST
    dot = lambda a, w_ref: jnp.dot(a, w_ref[...], precision=hp, preferred_element_type=F32)
    fr = fr_ref[...]
    h = jnp.sin(fr * (t * w1t_ref[...] + dot(jnp.cos(ang), w1c_ref) - dot(jnp.sin(ang), w1s_ref) + b1_ref[...]))
    h = jnp.sin(fr * (dot(h, w2_ref) + b2_ref[...]))
    h = jnp.sin(fr * (dot(h, w3_ref) + b3_ref[...]))
    for g in range(ng):
        og = dot(h[:, g * hid:(g + 1) * hid], w4_ref)
        og = og * jnp.exp(-t[:, g * hid:g * hid + 1] * deltas_ref[...])
        og = jnp.where((ridx[:, g * hid:g * hid + 1] == 0) & (d == 1), 0.0, og)
        for c in range(HY_WIDTH // HY_CT):
            o_ref[c, g * rg:(g + 1) * rg, :] = og[:, c * HY_CT:(c + 1) * HY_CT]


def _filter_call(seq_len, w1, b1, w2, b2, w3, b3, w4, freq):
    depth = w1.shape[0]
    H = HY_FILTER_HIDDEN
    ng = FILT_GROUPS
    rows = 1024
    nct = HY_WIDTH // HY_CT
    bands = np.tile(np.linspace(1e-4, HY_BANDS - 1, HY_BANDS), ng).astype(np.float32).reshape(1, LANES)
    max_decay = math.log(HY_DECAY_TARGET) / HY_FAST_DECAY_PCT
    min_decay = math.log(HY_DECAY_TARGET) / HY_SLOW_DECAY_PCT
    deltas = np.abs(np.linspace(min_decay, max_decay, HY_WIDTH)).astype(np.float32).reshape(1, HY_WIDTH)
    eye = jnp.eye(ng, dtype=F32)
    bdiag = lambda w: jnp.einsum("gh,lij->lgihj", eye, w).reshape(depth, ng * w.shape[1], ng * w.shape[2])
    tiled = lambda v: jnp.tile(v.reshape(depth, 1, H), (1, 1, ng))
    w1t = tiled(w1[:, 0, :])
    w1c = bdiag(w1[:, 1:1 + HY_BANDS, :])
    w1s = bdiag(w1[:, 1 + HY_BANDS:, :])
    vec = lambda n: pl.BlockSpec((None, 1, n), lambda l, d, r: (l, 0, 0))
    mat = lambda m, n: pl.BlockSpec((None, m, n), lambda l, d, r: (l, 0, 0))
    gh = ng * H
    return pl.pallas_call(
        functools.partial(_filter_body, seq_len=seq_len, rows=rows),
        grid=(depth, 2, seq_len // rows),
        in_specs=[
            vec(gh), mat(LANES, gh), mat(LANES, gh), vec(gh), mat(gh, gh), vec(gh), mat(gh, gh), vec(gh),
            pl.BlockSpec((None, H, HY_WIDTH), lambda l, d, r: (l, 0, d)),
            vec(gh),
            pl.BlockSpec((1, LANES), lambda l, d, r: (0, 0)),
            pl.BlockSpec((1, HY_WIDTH), lambda l, d, r: (0, 0)),
        ],
        out_specs=pl.BlockSpec((None, None, nct, rows, HY_CT), lambda l, d, r: (l, d, 0, r, 0)),
        out_shape=jax.ShapeDtypeStruct((depth, 2, nct, seq_len, HY_CT), F32),
        compiler_params=_cparams(("arbitrary", "arbitrary", "arbitrary")),
        name="hyena_filter",
    )(w1t, w1c, w1s, tiled(b1), bdiag(w2), tiled(b2), bdiag(w3), tiled(b3), w4, tiled(freq),
      jnp.asarray(bands), jnp.asarray(deltas))


def _mixer_a_body(*refs, tm, n_x):
    x_refs, refs = refs[:3 * n_x], refs[3 * n_x:]
    (g1_ref, win_ref, gb_ref, lng_ref, lnb_ref, sw_ref, sbt_ref, cw_ref, cb_ref, k_ref, v_ref, wba_ref, wbc_ref,
     we1_ref, we3_ref, we2_ref, mp_ref, gh_ref, x0_ref, u_ref, wb1_ref, wb3_ref, wb2_ref) = refs
    for src, dst in ((we1_ref, wb1_ref), (we3_ref, wb3_ref), (we2_ref, wb2_ref)):
        dst[...] = src[...].astype(BF16)
    x_cur = sum(r[...] for r in x_refs[0::3])
    x_prev = sum(r[...] for r in x_refs[1::3])
    x_next = sum(r[...] for r in x_refs[2::3])
    j = pl.program_id(1)
    nj = pl.num_programs(1)
    g1 = g1_ref[...]
    h = _rms(x_cur, g1)
    hb = h.astype(BF16)

    zs = _dot(hb, win_ref[:, 0:C_SGU])
    zs = 0.5 * zs * (1.0 + lax.erf(zs * (1.0 / math.sqrt(2.0))))
    us = zs[:, :SGU_WIDTH]
    vs = zs[:, SGU_WIDTH:]
    mu = jnp.mean(vs, axis=-1, keepdims=True)
    vc = vs - mu
    var = jnp.mean(vc * vc, axis=-1, keepdims=True)
    vn = (vc * lax.rsqrt(var + LN_EPS) * lng_ref[...] + lnb_ref[...]).astype(BF16)
    gd = SGU_WIDTH // SGU_GROUPS
    chunks = []
    for c in range(tm // SGU_CHUNK):
        groups = []
        for g in range(SGU_GROUPS):
            blk = vn[c * SGU_CHUNK:(c + 1) * SGU_CHUNK, g * gd:(g + 1) * gd]
            groups.append(_dot(sw_ref[g], blk))
        chunks.append(jnp.concatenate(groups, axis=1) + sbt_ref[...])
    a = us * jnp.concatenate(chunks, axis=0)

    q = _dot(hb, win_ref[:, C_HY:C_Q])
    scale = 1.0 / math.sqrt(XA_HEAD_DIM)
    heads = []
    for hd in range(XA_HEADS):
        sl = slice(hd * XA_HEAD_DIM, (hd + 1) * XA_HEAD_DIM)
        s = _dot_nt(q[:, sl].astype(BF16), k_ref[:, sl]) * scale
        s = s - jnp.max(s, axis=-1, keepdims=True)
        p = jnp.exp(s)
        p = p / jnp.sum(p, axis=-1, keepdims=True)
        heads.append(_dot(p.astype(BF16), v_ref[:, sl]))
    c_att = jnp.concatenate(heads, axis=1)

    gates = jax.nn.sigmoid(_dot(hb, win_ref[:, C_Q:]) + gb_ref[...])
    pa = _dot(a.astype(BF16), wba_ref[...])
    pc = _dot(c_att.astype(BF16), wbc_ref[...])
    mp_ref[...] = (gates[:, 0:D_MODEL] * pa + gates[:, 2 * D_MODEL:] * pc).astype(BF16)
    gh_ref[...] = gates[:, D_MODEL:2 * D_MODEL].astype(BF16)

    hprev = _rms(x_prev, g1).astype(BF16)
    hnext = _rms(x_next, g1).astype(BF16)
    hext = jnp.concatenate([hprev, hb, hnext], axis=0)
    pe = _dot(hext, win_ref[:, C_SGU:C_HY])
    rows = lax.broadcasted_iota(jnp.int32, (tm, 1), 0)
    pm1 = jnp.where((rows == 0) & (j == 0), 0.0, pe[SUBLANES - 1:SUBLANES - 1 + tm])
    pp1 = jnp.where((rows == tm - 1) & (j == nj - 1), 0.0, pe[SUBLANES + 1:SUBLANES + 1 + tm])
    p0 = pe[SUBLANES:SUBLANES + tm]
    pconv = cw_ref[0:1, :] * pm1 + cw_ref[1:2, :] * p0 + cw_ref[2:3, :] * pp1 + cb_ref[...]
    x0 = pconv[:, :HY_WIDTH]
    u = pconv[:, HY_WIDTH:2 * HY_WIDTH] * pconv[:, 2 * HY_WIDTH:]
    for c in range(HY_WIDTH // HY_CT):
        x0_ref[c] = x0[:, c * HY_CT:(c + 1) * HY_CT]
        u_ref[c] = u[:, c * HY_CT:(c + 1) * HY_CT]


def _mixer_a_call(l, xs, norm1_g, w_in_bf, gate_b, sgu_ln_g, sgu_ln_b, sgu_w_bf, sgu_bt, conv_w, conv_b,
                  k_all, v_all, w_branch_bf, expert_ws):
    B, L, D = xs[0].shape
    tm = 512
    nct = HY_WIDTH // HY_CT
    nsteps = B * (L // tm)
    e_rows = expert_ws[0].shape[1] * expert_ws[0].shape[2]
    e_cols = expert_ws[0].shape[3]
    assert all(w.shape[1] * w.shape[2] == e_rows and w.shape[3] == e_cols for w in expert_ws)
    slab = e_rows // nsteps
    assert slab * nsteps == e_rows and slab % (2 * SUBLANES) == 0
    e_flat = [w.reshape(w.shape[0], e_rows, e_cols) for w in expert_ws]
    step = lambda b, j: b * (L // tm) + j
    in_cols = w_in_bf.shape[-1]
    hb_per_tile = tm // SUBLANES
    n_hb = L // SUBLANES
    const = lambda shape: pl.BlockSpec((None,) + shape, lambda b, j: (l,) + (0,) * len(shape),
                                       pipeline_mode=pl.Buffered(1))
    x_specs = [
        pl.BlockSpec((None, tm, D), lambda b, j: (b, j, 0)),
        pl.BlockSpec((None, SUBLANES, D), lambda b, j: (b, jnp.maximum(j * hb_per_tile - 1, 0), 0)),
        pl.BlockSpec((None, SUBLANES, D), lambda b, j: (b, jnp.minimum((j + 1) * hb_per_tile, n_hb - 1), 0)),
    ]
    x_args = [a for x in xs for a in (x, x, x)]
    return pl.pallas_call(
        functools.partial(_mixer_a_body, tm=tm, n_x=len(xs)),
        grid=(B, L // tm),
        in_specs=x_specs * len(xs) + [
            const((1, D)),
            const((D, in_cols)),
            const((1, 3 * D)),
            const((1, SGU_WIDTH)),
            const((1, SGU_WIDTH)),
            const((SGU_GROUPS, SGU_CHUNK, SGU_CHUNK)),
            const((SGU_CHUNK, SGU_WIDTH)),
            const((3, 3 * HY_WIDTH)),
            const((1, 3 * HY_WIDTH)),
            pl.BlockSpec((None, None, k_all.shape[2], XA_WIDTH), lambda b, j: (l, b, 0, 0)),
            pl.BlockSpec((None, None, k_all.shape[2], XA_WIDTH), lambda b, j: (l, b, 0, 0)),
            pl.BlockSpec((None, None, SGU_WIDTH, D), lambda b, j: (l, 0, 0, 0), pipeline_mode=pl.Buffered(1)),
            pl.BlockSpec((None, None, XA_WIDTH, D), lambda b, j: (l, 2, 0, 0), pipeline_mode=pl.Buffered(1)),
        ] + [pl.BlockSpec((None, slab, e_cols), lambda b, j: (l, step(b, j), 0))] * len(e_flat),
        out_specs=[
            pl.BlockSpec((None, tm, D), lambda b, j: (b, j, 0)),
            pl.BlockSpec((None, tm, D), lambda b, j: (b, j, 0)),
            pl.BlockSpec((None, nct, tm, HY_CT), lambda b, j: (b, 0, j, 0)),
            pl.BlockSpec((None, nct, tm, HY_CT), lambda b, j: (b, 0, j, 0)),
        ] + [pl.BlockSpec((slab, e_cols), lambda b, j: (step(b, j), 0))] * len(e_flat),
        out_shape=[
            jax.ShapeDtypeStruct((B, L, D), BF16),
            jax.ShapeDtypeStruct((B, L, D), BF16),
            jax.ShapeDtypeStruct((B, nct, L, HY_CT), F32),
            jax.ShapeDtypeStruct((B, nct, L, HY_CT), F32),
        ] + [jax.ShapeDtypeStruct((e_rows, e_cols), BF16)] * len(e_flat),
        compiler_params=_cparams(("arbitrary", "arbitrary")),
        name="mixer_a",
    )(*x_args, norm1_g, w_in_bf, gate_b, sgu_ln_g, sgu_ln_b, sgu_w_bf, sgu_bt, conv_w, conv_b,
      k_all, v_all, w_branch_bf, w_branch_bf, *e_flat)


def _fft_constants(seq_len):
    n = 2 * seq_len
    n2 = FFT_N2
    n1 = n // n2
    nz = seq_len // n2
    lo = FFT_ROWS // nz
    tiles = n2 // lo
    k1n = n1 // 2 + 1
    assert lo == SUBLANES and nz * lo == FFT_ROWS
    k1 = np.arange(k1n).reshape(k1n, 1, 1, 1, 1).astype(np.float64)
    part = np.arange(2).reshape(1, 2, 1, 1, 1)
    a = np.arange(lo).reshape(1, 1, lo, 1, 1)
    m1 = np.arange(nz).reshape(1, 1, 1, nz, 1).astype(np.float64)
    a2 = np.arange(lo).reshape(1, 1, 1, 1, lo)
    theta = 2.0 * np.pi * m1 * k1 / n1
    fwd = (np.where(part == 0, np.cos(theta), -np.sin(theta)) * (a == a2)).reshape(k1n * 2 * lo, nz * lo)
    weight = np.where((np.arange(k1n) == 0) | (np.arange(k1n) == n1 // 2), 1.0, 2.0) / n
    inv = (fwd.reshape(k1n, 2 * lo * nz * lo) * weight.reshape(k1n, 1)).reshape(k1n * 2 * lo, nz * lo).T
    n2_idx = (lo * np.arange(tiles).reshape(tiles, 1, 1) + np.arange(lo).reshape(1, lo, 1)).astype(np.float64)
    ang = 2.0 * np.pi * n2_idx * np.arange(k1n).reshape(1, 1, k1n) / n
    tw_cos = np.zeros((tiles, lo, LANES), np.float64)
    tw_sin = np.zeros((tiles, lo, LANES), np.float64)
    tw_cos[:, :, :k1n] = np.cos(ang)
    tw_sin[:, :, :k1n] = np.sin(ang)
    phi = 2.0 * np.pi * np.outer(np.arange(n2), np.arange(n2)) / n2
    c, s = np.cos(phi), np.sin(phi)
    g_fwd = np.block([[c, s], [-s, c]])
    g_inv = np.block([[c, -s], [s, c]])
    f32 = lambda v: jnp.asarray(v.astype(np.float32))
    bf = lambda v: f32(v).astype(BF16)
    return bf(fwd), bf(inv), bf(g_fwd), bf(g_inv), f32(tw_cos), f32(tw_sin), (tiles, k1n, lo, nz)


def _hyena_body(filt_ref, u_ref, x0_ref, bias_ref, fa_ref, ia_ref, gf_ref, gi_ref, twc_ref, tws_ref,
                o_ref, z_ref, k_ref, *, tiles, k1n, lo, nz):
    s = pl.program_id(1)
    half = FFT_N2
    rows = nz * lo
    ct = HY_CT

    def twiddles(j):
        c, sn = twc_ref[j], tws_ref[j]
        bc = lambda t: jnp.stack([jnp.broadcast_to(t[:, k1:k1 + 1], (lo, ct)) for k1 in range(k1n)])
        return bc(c), bc(sn)

    def stage_a(src_ref):
        for j in range(tiles):
            tile = src_ref[:, j].reshape(rows, ct).astype(BF16)
            zb = _dot(fa_ref[...], tile).reshape(k1n, 2, lo, ct)
            zr, zi = zb[:, 0], zb[:, 1]
            c, sn = twiddles(j)
            z_ref[:, 0, j] = zr * c + zi * sn
            z_ref[:, 1, j] = zi * c - zr * sn

    def spectrum(k1):
        return _dot(gf_ref[...], z_ref[k1].reshape(2 * half, ct).astype(BF16))

    @pl.when(s == 0)
    def _():
        k_ref[...] = jnp.zeros(k_ref.shape, F32)

    @pl.when(s < 2)
    def _():
        stage_a(filt_ref)
        odd_sign = jnp.where(s == 0, 1.0, -1.0).astype(F32)
        for k1 in range(k1n):
            x = spectrum(k1)
            k_ref[k1] += x if k1 % 2 == 0 else odd_sign * x

    @pl.when(s >= 2)
    def _():
        stage_a(u_ref)
        for k1 in range(k1n):
            x = spectrum(k1)
            xr, xi = x[:half], x[half:]
            kr, ki = k_ref[k1, :half], k_ref[k1, half:]
            y = jnp.concatenate([xr * kr - xi * ki, xr * ki + xi * kr], axis=0).astype(BF16)
            z_ref[k1] = _dot(gi_ref[...], y).reshape(2, tiles, lo, ct)
        bias = bias_ref[...]
        for j in range(tiles):
            vr, vi = z_ref[:, 0, j], z_ref[:, 1, j]
            c, sn = twiddles(j)
            wr = vr * c - vi * sn
            wi = vi * c + vr * sn
            zt = jnp.stack([wr, wi], axis=1).reshape(k1n * 2 * lo, ct).astype(BF16)
            y = _dot(ia_ref[...], zt)
            uu = u_ref[:, j].reshape(rows, ct)
            x0 = x0_ref[:, j].reshape(rows, ct)
            o_ref[:, j] = (x0 * (y + bias * uu)).reshape(nz, lo, ct)


def _hyena_call(l, filt, u, x0, hy_bias, consts):
    fa, ia, gf, gi, twc, tws, (tiles, k1n, lo, nz) = consts
    B, nct, L, ct = u.shape
    view = lambda a: a.reshape(a.shape[:-2] + (nz, tiles, lo, ct))
    seq = (nz, tiles, lo, ct)
    bidx = lambda s: jnp.maximum(s - 2, 0)
    single = lambda shape: pl.BlockSpec(shape, lambda c, s: (0,) * len(shape), pipeline_mode=pl.Buffered(1))
    out = pl.pallas_call(
        functools.partial(_hyena_body, tiles=tiles, k1n=k1n, lo=lo, nz=nz),
        grid=(nct, B + 2),
        in_specs=[
            pl.BlockSpec((None, None, None) + seq, lambda c, s: (l, jnp.minimum(s, 1), c, 0, 0, 0, 0),
                         pipeline_mode=pl.Buffered(1)),
            pl.BlockSpec((None, None) + seq, lambda c, s: (bidx(s), c, 0, 0, 0, 0)),
            pl.BlockSpec((None, None) + seq, lambda c, s: (bidx(s), c, 0, 0, 0, 0)),
            pl.BlockSpec((None, None, 1, ct), lambda c, s: (l, c, 0, 0)),
            single(fa.shape), single(ia.shape), single(gf.shape), single(gi.shape),
            single(twc.shape), single(tws.shape),
        ],
        out_specs=pl.BlockSpec((None, None) + seq, lambda c, s: (bidx(s), c, 0, 0, 0, 0)),
        out_shape=jax.ShapeDtypeStruct((B, nct) + seq, F32),
        scratch_shapes=[
            pltpu.VMEM((k1n, 2, tiles, lo, ct), F32),
            pltpu.VMEM((k1n, 2 * FFT_N2, ct), F32),
        ],
        compiler_params=_cparams(("arbitrary", "arbitrary")),
        name="hyena_fft",
    )(view(filt), view(u), view(x0), hy_bias, fa, ia, gf, gi, twc, tws)
    return out.reshape(B, nct, L, HY_CT)


def _pack_bf16_pairs(x):
    n = x.shape[1] // 2
    return pltpu.pack_elementwise([x[:, :n], x[:, n:]], packed_dtype=BF16)


def _unpack_bf16_pairs(p):
    lo = pltpu.unpack_elementwise(p, index=0, packed_dtype=BF16, unpacked_dtype=F32)
    hi = pltpu.unpack_elementwise(p, index=1, packed_dtype=BF16, unpacked_dtype=F32)
    return jnp.concatenate([lo, hi], axis=1).astype(BF16)


def _mixer_c_body(*refs, n_x):
    x_refs, refs = refs[:n_x], refs[n_x:]
    (mp_ref, gh_ref, hy_ref, wbh_ref, wo_ref, g2_ref, wr_ref, x1_ref, hp_ref, aff_ref, afft_ref) = refs
    hy = jnp.concatenate([hy_ref[c] for c in range(HY_WIDTH // HY_CT)], axis=1).astype(BF16)
    merged = mp_ref[...].astype(F32) + gh_ref[...].astype(F32) * _dot(hy, wbh_ref[...])
    x1 = sum(r[...] for r in x_refs) + _dot(merged.astype(BF16), wo_ref[...])
    x1_ref[...] = x1
    h2 = _rms(x1, g2_ref[...])
    hp_ref[...] = _pack_bf16_pairs(h2)
    tm = h2.shape[0]
    hh, hl = _split_bf16(h2)
    wh, wl = _split_bf16(wr_ref[...])
    prod = _dot(jnp.concatenate([hh, hl], axis=0), jnp.concatenate([wh, wl], axis=1))
    logits = prod[:tm, :LANES] + prod[:tm, LANES:] + prod[tm:, :LANES]
    lane = lax.broadcasted_iota(jnp.int32, logits.shape, 1)
    logits = jnp.where(lane < N_EXPERTS, logits, -jnp.inf)
    e = jnp.exp(logits - jnp.max(logits, axis=-1, keepdims=True))
    aff = e / jnp.sum(e, axis=-1, keepdims=True)
    aff_ref[...] = aff
    afft_ref[...] = aff.T[:N_EXPERTS, :]


def _mixer_c_call(l, xs, mp, gh, hy, w_branch_bf, w_out_bf, norm2_g, wr_pad):
    B, L, D = xs[0].shape
    tm = 512
    nct = HY_WIDTH // HY_CT
    const = lambda shape: pl.BlockSpec((None,) + shape, lambda b, j: (l,) + (0,) * len(shape),
                                       pipeline_mode=pl.Buffered(1))
    return pl.pallas_call(
        functools.partial(_mixer_c_body, n_x=len(xs)),
        grid=(B, L // tm),
        in_specs=[pl.BlockSpec((None, tm, D), lambda b, j: (b, j, 0))] * len(xs) + [
            pl.BlockSpec((None, tm, D), lambda b, j: (b, j, 0)),
            pl.BlockSpec((None, tm, D), lambda b, j: (b, j, 0)),
            pl.BlockSpec((None, nct, tm, HY_CT), lambda b, j: (b, 0, j, 0)),
            pl.BlockSpec((None, None, HY_WIDTH, D), lambda b, j: (l, 1, 0, 0), pipeline_mode=pl.Buffered(1)),
            const((D, D)),
            const((1, D)),
            const((D, LANES)),
        ],
        out_specs=[
            pl.BlockSpec((None, tm, D), lambda b, j: (b, j, 0)),
            pl.BlockSpec((None, tm, D // 2), lambda b, j: (b, j, 0)),
            pl.BlockSpec((None, tm, LANES), lambda b, j: (b, j, 0)),
            pl.BlockSpec((None, N_EXPERTS, tm), lambda b, j: (b, 0, j)),
        ],
        out_shape=[
            jax.ShapeDtypeStruct((B, L, D), F32),
            jax.ShapeDtypeStruct((B, L, D // 2), jnp.uint32),
            jax.ShapeDtypeStruct((B, L, LANES), F32),
            jax.ShapeDtypeStruct((B, N_EXPERTS, L), F32),
        ],
        compiler_params=_cparams(("arbitrary", "arbitrary")),
        name="mixer_c",
    )(*xs, mp, gh, hy, w_branch_bf, w_out_bf, norm2_g, wr_pad)


def _cumsum_lanes(x, tri):
    outs = []
    carry = jnp.zeros((x.shape[0], 1), F32)
    for blk in range(x.shape[1] // LANES):
        loc = _dot(x[:, blk * LANES:(blk + 1) * LANES].astype(BF16), tri) + carry
        outs.append(loc)
        carry = loc[:, LANES - 1:LANES]
    return jnp.concatenate(outs, axis=1)


def _route_body(afft_ref, tri_ref, idx_ref, sel_ref, thr_ref, *, cap):
    b = pl.program_id(0)
    nb, ne, seq = afft_ref.shape

    @pl.when(b == 0)
    def _():
        a_all = afft_ref[...].reshape(nb * ne, seq)

        def bit_step(i, v):
            cand = v | jnp.left_shift(jnp.int32(1), 30 - i)
            cnt = jnp.sum((a_all >= lax.bitcast_convert_type(cand, F32)).astype(F32), axis=1, keepdims=True)
            return jnp.where(cnt >= cap, cand, v)

        bits = lax.fori_loop(0, 31, bit_step, jnp.zeros((nb * ne, 1), jnp.int32))
        thr_ref[...] = jnp.broadcast_to(lax.bitcast_convert_type(bits, F32), thr_ref.shape)

    a = afft_ref[b]
    thr = thr_ref[pl.ds(pl.multiple_of(b * ne, ne), ne), 0:1]
    gt = a > thr
    eq = a == thr
    need = cap - jnp.sum(gt.astype(F32), axis=1, keepdims=True)
    tri = tri_ref[...]
    cum_eq = _cumsum_lanes(eq.astype(F32), tri)
    sel = gt | (eq & (cum_eq <= need))
    sel_ref[...] = sel.astype(F32)

    nblk = seq // LANES
    f32c = lambda v: jnp.where(v, 1.0, 0.0)
    lane = lax.broadcasted_iota(jnp.int32, (cap, LANES), 1)
    slot = lax.broadcasted_iota(jnp.int32, (cap, LANES), 0).astype(F32)
    ones_row = jnp.ones((SUBLANES, LANES), BF16)
    bi = lax.broadcasted_iota(jnp.int32, (nblk, LANES), 0)
    bj = lax.broadcasted_iota(jnp.int32, (nblk, LANES), 1)
    upper = f32c((bi <= bj) | (bj >= nblk)).astype(BF16)
    lower = f32c(bj < bi)[:, :nblk].astype(BF16)
    half = float(cap // 2)
    experts = range(ne)
    selb = [jnp.concatenate([sel_ref[e:e + 1, j * LANES:(j + 1) * LANES] for j in range(nblk)], axis=0).astype(BF16)
            for e in experts]
    loc = [_dot(selb[e], tri) for e in experts]
    tot_row = [_dot_nt(ones_row, selb[e])[:, :nblk].astype(BF16) for e in experts]
    incl_row = [_dot(tot_row[e], upper)[0:1, :] for e in experts]
    excl = [_dot(lower, jnp.broadcast_to(loc[e][:, LANES - 1:LANES], (nblk, LANES)).astype(BF16)) for e in experts]
    jb = [jnp.sum(f32c(incl_row[e] <= slot), axis=1, keepdims=True) for e in experts]
    onehot = [f32c(jb[e].astype(jnp.int32) == lane).astype(BF16) for e in experts]
    pad = jnp.zeros((LANES - nblk, 3 * LANES), F32)
    table = [jnp.concatenate([jnp.concatenate([loc[e], jnp.minimum(excl[e], half),
                                               jnp.maximum(excl[e] - half, 0.0)], axis=1), pad], axis=0).astype(BF16)
             for e in experts]
    got = [_dot(onehot[e], table[e]) for e in experts]
    idx_t = jnp.zeros((cap, LANES), F32)
    for e in experts:
        local = slot - (got[e][:, LANES:2 * LANES] + got[e][:, 2 * LANES:])
        pos = jnp.sum(f32c(got[e][:, :LANES] <= local), axis=1, keepdims=True)
        idx_t = idx_t + jnp.where(lane == e, jb[e] * float(LANES) + pos, 0.0)
    idx_ref[...] = idx_t.T[:ne, :].astype(jnp.int32)


def _route_call(afft, cap):
    B, ne, L = afft.shape
    tri = jnp.asarray(np.triu(np.ones((LANES, LANES), np.float32))).astype(BF16)
    return pl.pallas_call(
        functools.partial(_route_body, cap=cap),
        grid=(B,),
        in_specs=[
            pl.BlockSpec((B, ne, L), lambda b: (0, 0, 0)),
            pl.BlockSpec((LANES, LANES), lambda b: (0, 0)),
        ],
        out_specs=pl.BlockSpec((None, ne, cap), lambda b: (b, 0, 0)),
        out_shape=jax.ShapeDtypeStruct((B, ne, cap), jnp.int32),
        scratch_shapes=[pltpu.VMEM((ne, L), F32), pltpu.VMEM((B * ne, LANES), F32)],
        compiler_params=_cparams(("arbitrary",)),
        name="route",
    )(afft, tri)


def _moe_body(idx_ref, idxp_ref, hp_ref, aff_ref, w1_ref, w3_ref, w2_ref, out_hbm,
              acc_ref, xin_ref, gsel_ref, y_ref, sem_ref, *, cap, ne, npairs):
    i = pl.program_id(0)
    slot = i % 2
    e = jnp.minimum(i, npairs - 1) % ne

    @pl.when(i == 0)
    def _():
        acc_ref[...] = jnp.zeros(acc_ref.shape, F32)
        y_ref[...] = jnp.zeros(y_ref.shape, F32)

    y_prev = y_ref.at[1 - slot]
    for g0 in range(0, cap, SUBLANES):
        y_tile = y_prev[pl.ds(g0, SUBLANES), :]
        for k0 in range(0, SUBLANES, MOE_RMW_ROWS):
            ks = range(k0, k0 + MOE_RMW_ROWS)
            toks = [idxp_ref[0, 0, g0 + k] for k in ks]
            vals = [acc_ref[pl.ds(t, 1), :] + y_tile[k:k + 1, :] for t, k in zip(toks, ks)]
            for t, v in zip(toks, vals):
                acc_ref[pl.ds(t, 1), :] = v

    for s in range(cap):
        t = idx_ref[0, 0, s]
        xin_ref[pl.ds(s, 1), :] = hp_ref[pl.ds(t, 1), :]
        gsel_ref[pl.ds(s, 1), :] = aff_ref[pl.ds(t, 1), :]

    aff = gsel_ref[...]
    lane = lax.broadcasted_iota(jnp.int32, aff.shape, 1)
    gate = jnp.sum(jnp.where(lane == e, aff, 0.0), axis=1, keepdims=True)
    xb = _unpack_bf16_pairs(xin_ref[...])
    a1 = _dot(xb, w1_ref[...])
    a3 = _dot(xb, w3_ref[...])
    hid = (a1 * jax.nn.sigmoid(a1) * a3).astype(BF16)
    y_ref[slot] = _dot(hid, w2_ref[...]) * gate

    @pl.when((i > 0) & (i % ne == 0))
    def _():
        cp = pltpu.make_async_copy(acc_ref, out_hbm.at[i // ne - 1], sem_ref.at[0])
        cp.start()
        cp.wait()
        acc_ref[...] = jnp.zeros(acc_ref.shape, F32)


def _moe_call(idx, hp, aff, w1_rows, w3_rows, w2_rows):
    B, ne, cap = idx.shape
    L = hp.shape[1]
    ff = w1_rows.shape[-1]
    w1_bf, w3_bf = (w.reshape(ne, D_MODEL, ff) for w in (w1_rows, w3_rows))
    w2_bf = w2_rows.reshape(ne, ff, D_MODEL)
    npairs = B * ne
    idx3 = idx.reshape(npairs, 1, cap)
    pair = lambda i: jnp.minimum(i, npairs - 1)
    prev = lambda i: jnp.maximum(i - 1, 0)
    return pl.pallas_call(
        functools.partial(_moe_body, cap=cap, ne=ne, npairs=npairs),
        grid=(npairs + 1,),
        in_specs=[
            pl.BlockSpec((1, 1, cap), lambda i: (pair(i), 0, 0), memory_space=pltpu.SMEM),
            pl.BlockSpec((1, 1, cap), lambda i: (prev(i), 0, 0), memory_space=pltpu.SMEM),
            pl.BlockSpec((None, L, D_MODEL // 2), lambda i: (pair(i) // ne, 0, 0), pipeline_mode=pl.Buffered(1)),
            pl.BlockSpec((None, L, LANES), lambda i: (pair(i) // ne, 0, 0), pipeline_mode=pl.Buffered(1)),
            pl.BlockSpec((None, D_MODEL, ff), lambda i: (pair(i) % ne, 0, 0)),
            pl.BlockSpec((None, D_MODEL, ff), lambda i: (pair(i) % ne, 0, 0)),
            pl.BlockSpec((None, ff, D_MODEL), lambda i: (pair(i) % ne, 0, 0)),
        ],
        out_specs=pl.BlockSpec(memory_space=pl.ANY),
        out_shape=jax.ShapeDtypeStruct((B, L, D_MODEL), F32),
        scratch_shapes=[
            pltpu.VMEM((L, D_MODEL), F32),
            pltpu.VMEM((cap, D_MODEL // 2), jnp.uint32),
            pltpu.VMEM((cap, LANES), F32),
            pltpu.VMEM((2, cap, D_MODEL), F32),
            pltpu.SemaphoreType.DMA((1,)),
        ],
        compiler_params=_cparams(("arbitrary",)),
        name="moe",
    )(idx3, idx3, hp, aff, w1_bf, w3_bf, w2_bf)


def _final_body(*refs):
    g_ref, o_ref = refs[-2:]
    o_ref[...] = _rms(sum(r[...] for r in refs[:-2]), g_ref[...])


def _final_call(xs, g):
    B, L, D = xs[0].shape
    tm = 1024
    return pl.pallas_call(
        _final_body,
        grid=(B, L // tm),
        in_specs=[pl.BlockSpec((None, tm, D), lambda b, j: (b, j, 0))] * len(xs)
        + [pl.BlockSpec((1, D), lambda b, j: (0, 0))],
        out_specs=pl.BlockSpec((None, tm, D), lambda b, j: (b, j, 0)),
        out_shape=jax.ShapeDtypeStruct((B, L, D), F32),
        compiler_params=_cparams(("arbitrary", "arbitrary")),
        name="final_norm",
    )(*xs, g.reshape(1, D))


def kernel(x, mem, norm1_g, w_in, gate_b, sgu_ln_g, sgu_ln_b, sgu_w, sgu_b, hy_conv_w, hy_conv_b, hy_f_w1, hy_f_b1, hy_f_w2, hy_f_b2, hy_f_w3, hy_f_b3, hy_f_w4, hy_f_freq, hy_bias, mem_norm_g, w_kv, w_branch, w_out, norm2_g, w_router, w1, w3, w2, final_g):
    B, L, D = x.shape
    depth = w_in.shape[0]
    cap = EC_CAPACITY * L // N_EXPERTS
    nct = HY_WIDTH // HY_CT
    row = lambda a: a.reshape(depth, 1, a.shape[-1])

    w_in_bf = w_in.astype(BF16)
    w_kv_bf = w_kv.astype(BF16)
    w_branch_bf = w_branch.astype(BF16)
    w_out_bf = w_out.astype(BF16)
    sgu_w_bf = sgu_w.astype(BF16)
    sgu_bt = jnp.repeat(jnp.swapaxes(sgu_b, 1, 2), SGU_WIDTH // SGU_GROUPS, axis=2)
    wr_pad = jnp.pad(w_router, ((0, 0), (0, 0), (0, LANES - N_EXPERTS)))
    hy_bias4 = hy_bias.reshape(depth, nct, 1, HY_CT)

    k_all, v_all = _kv_call(mem, mem_norm_g, w_kv_bf)
    filt = _filter_call(L, hy_f_w1, hy_f_b1, hy_f_w2, hy_f_b2, hy_f_w3, hy_f_b3, hy_f_w4, hy_f_freq)
    consts = _fft_constants(L)

    xs = [x]
    for l in range(depth):
        mp, gh, x0, u, w1_bf, w3_bf, w2_bf = _mixer_a_call(
            l, xs, row(norm1_g), w_in_bf, row(gate_b), row(sgu_ln_g), row(sgu_ln_b), sgu_w_bf, sgu_bt,
            hy_conv_w, row(hy_conv_b), k_all, v_all, w_branch_bf, (w1, w3, w2))
        hy = _hyena_call(l, filt, u, x0, hy_bias4, consts)
        x1, hp, aff, afft = _mixer_c_call(l, xs, mp, gh, hy, w_branch_bf, w_out_bf, row(norm2_g), wr_pad)
        idx = _route_call(afft, cap)
        xs = [x1, _moe_call(idx, hp, aff, w1_bf, w3_bf, w2_bf)]
    return _final_call(xs, final_g)
```

```python
import functools
import math

import numpy as np
import jax
import jax.numpy as jnp
from jax import lax
from jax.experimental import pallas as pl
from jax.experimental.pallas import tpu as pltpu

F32 = jnp.float32
BF16 = jnp.bfloat16

D_MODEL = 1024
SGU_WIDTH = 512
SGU_GROUPS = 4
SGU_CHUNK = 128
HY_WIDTH = 512
HY_EMB_DIM = 33
HY_BANDS = (HY_EMB_DIM - 1) // 2
HY_FILTER_HIDDEN = 64
HY_FAST_DECAY_PCT = 0.3
HY_SLOW_DECAY_PCT = 1.5
HY_DECAY_TARGET = 1e-2
XA_HEADS = 4
XA_HEAD_DIM = 128
XA_WIDTH = XA_HEADS * XA_HEAD_DIM
N_EXPERTS = 16
EC_CAPACITY = 2
NORM_EPS = 1e-6
LN_EPS = 1e-5

C_SGU = 2 * SGU_WIDTH
C_HY = C_SGU + 3 * HY_WIDTH
C_Q = C_HY + XA_WIDTH

LANES = 128
SUBLANES = 8
VMEM_LIMIT = 56 * 1024 * 1024

FFT_N2 = 128
FFT_ROWS = 256
HY_CT = 256

MOE_RMW_ROWS = 8


def _cparams(semantics, flags=None):
    return pltpu.CompilerParams(dimension_semantics=semantics, vmem_limit_bytes=VMEM_LIMIT, flags=flags)


def _rms(x, g):
    return x * lax.rsqrt(jnp.mean(x * x, axis=-1, keepdims=True) + NORM_EPS) * g


def _dot(a, b):
    return jnp.dot(a, b, preferred_element_type=F32)


def _dot_nt(a, b):
    return lax.dot_general(a, b, (((1,), (1,)), ((), ())), preferred_element_type=F32)


def _split_bf16(x):
    hi = x.astype(BF16)
    lo = (x - hi.astype(F32)).astype(BF16)
    return hi, lo


def _kv_body(mem_ref, g_ref, w_ref, k_ref, v_ref):
    m = _rms(mem_ref[...], g_ref[...]).astype(BF16)
    kv = _dot(m, w_ref[...])
    k_ref[...] = kv[:, :XA_WIDTH].astype(BF16)
    v_ref[...] = kv[:, XA_WIDTH:].astype(BF16)


def _kv_call(mem, mem_norm_g, w_kv_bf):
    B, M, D = mem.shape
    depth = w_kv_bf.shape[0]
    return pl.pallas_call(
        _kv_body,
        grid=(depth, B),
        in_specs=[
            pl.BlockSpec((None, M, D), lambda l, b: (b, 0, 0)),
            pl.BlockSpec((None, 1, D), lambda l, b: (l, 0, 0)),
            pl.BlockSpec((None, D, 2 * XA_WIDTH), lambda l, b: (l, 0, 0)),
        ],
        out_specs=[
            pl.BlockSpec((None, None, M, XA_WIDTH), lambda l, b: (l, b, 0, 0)),
            pl.BlockSpec((None, None, M, XA_WIDTH), lambda l, b: (l, b, 0, 0)),
        ],
        out_shape=[jax.ShapeDtypeStruct((depth, B, M, XA_WIDTH), BF16)] * 2,
        compiler_params=_cparams(("arbitrary", "arbitrary")),
        name="kv",
    )(mem, mem_norm_g.reshape(depth, 1, D), w_kv_bf)


FILT_GROUPS = LANES // HY_BANDS


def _filter_body(w1t_ref, w1c_ref, w1s_ref, b1_ref, w2_ref, b2_ref, w3_ref, b3_ref, w4_ref, fr_ref,
                 bands_ref, deltas_ref, o_ref, *, seq_len, rows):
    r0 = pl.program_id(1) * rows
    ng, hid = FILT_GROUPS, HY_FILTER_HIDDEN
    rg = rows // ng

    def lags(width, per_group):
        return (r0 + lax.broadcasted_iota(jnp.int32, (rg, width), 0)
                + (lax.broadcasted_iota(jnp.int32, (rg, width), 1) // per_group) * rg)

    pos_b = lags(LANES, HY_BANDS).astype(F32)
    lag = lags(ng * hid, hid)
    t = lag.astype(F32) * (1.0 / (seq_len - 1))
    ang = pos_b * (2.0 * math.pi / seq_len) * bands_ref[...]
    hp = lax.Precision.HIGHEST
    dot = lambda a, w_ref: jnp.dot(a, w_ref[...], precision=hp, preferred_element_type=F32)
    fr = fr_ref[...]
    h = jnp.sin(fr * (t * w1t_ref[...] + dot(jnp.cos(ang), w1c_ref) - dot(jnp.sin(ang), w1s_ref) + b1_ref[...]))
    h = jnp.sin(fr * (dot(h, w2_ref) + b2_ref[...]))
    h = jnp.sin(fr * (dot(h, w3_ref) + b3_ref[...]))
    for g in range(ng):
        og = dot(h[:, g * hid:(g + 1) * hid], w4_ref)
        decay = jnp.exp(-t[:, g * hid:g * hid + 1] * deltas_ref[...])
        fwd = og[:, :HY_WIDTH] * decay
        bwd = jnp.where(lag[:, g * hid:g * hid + 1] == 0, 0.0, og[:, HY_WIDTH:] * decay)
        for c in range(HY_WIDTH // HY_CT):
            o_ref[0, c, g * rg:(g + 1) * rg, :] = fwd[:, c * HY_CT:(c + 1) * HY_CT]
            o_ref[1, c, g * rg:(g + 1) * rg, :] = bwd[:, c * HY_CT:(c + 1) * HY_CT]


def _filter_call(seq_len, w1, b1, w2, b2, w3, b3, w4, freq):
    depth = w1.shape[0]
    H = HY_FILTER_HIDDEN
    ng = FILT_GROUPS
    rows = 1024
    nct = HY_WIDTH // HY_CT
    bands = np.tile(np.linspace(1e-4, HY_BANDS - 1, HY_BANDS), ng).astype(np.float32).reshape(1, LANES)
    max_decay = math.log(HY_DECAY_TARGET) / HY_FAST_DECAY_PCT
    min_decay = math.log(HY_DECAY_TARGET) / HY_SLOW_DECAY_PCT
    deltas = np.abs(np.linspace(min_decay, max_decay, HY_WIDTH)).astype(np.float32).reshape(1, HY_WIDTH)
    eye = jnp.eye(ng, dtype=F32)
    bdiag = lambda w: jnp.einsum("gh,lij->lgihj", eye, w).reshape(depth, ng * w.shape[1], ng * w.shape[2])
    tiled = lambda v: jnp.tile(v.reshape(depth, 1, H), (1, 1, ng))
    w1t = tiled(w1[:, 0, :])
    w1c = bdiag(w1[:, 1:1 + HY_BANDS, :])
    w1s = bdiag(w1[:, 1 + HY_BANDS:, :])
    vec = lambda n: pl.BlockSpec((None, 1, n), lambda l, r: (l, 0, 0))
    mat = lambda m, n: pl.BlockSpec((None, m, n), lambda l, r: (l, 0, 0))
    gh = ng * H
    return pl.pallas_call(
        functools.partial(_filter_body, seq_len=seq_len, rows=rows),
        grid=(depth, seq_len // rows),
        in_specs=[
            vec(gh), mat(LANES, gh), mat(LANES, gh), vec(gh), mat(gh, gh), vec(gh), mat(gh, gh), vec(gh),
            mat(H, 2 * HY_WIDTH),
            vec(gh),
            pl.BlockSpec((1, LANES), lambda l, r: (0, 0)),
            pl.BlockSpec((1, HY_WIDTH), lambda l, r: (0, 0)),
        ],
        out_specs=pl.BlockSpec((None, 2, nct, rows, HY_CT), lambda l, r: (l, 0, 0, r, 0)),
        out_shape=jax.ShapeDtypeStruct((depth, 2, nct, seq_len, HY_CT), F32),
        compiler_params=_cparams(("arbitrary", "arbitrary")),
        name="hyena_filter",
    )(w1t, w1c, w1s, tiled(b1), bdiag(w2), tiled(b2), bdiag(w3), tiled(b3), w4, tiled(freq),
      jnp.asarray(bands), jnp.asarray(deltas))


def _mixer_a_body(*refs, tm, n_x):
    x_refs, refs = refs[:3 * n_x], refs[3 * n_x:]
    (g1_ref, win_ref, gb_ref, lng_ref, lnb_ref, sw_ref, sbt_ref, cw_ref, cb_ref, k_ref, v_ref, wba_ref, wbc_ref,
     we1_ref, we3_ref, we2_ref, mp_ref, gh_ref, x0_ref, u_ref, wb1_ref, wb3_ref, wb2_ref) = refs
    x_cur = sum(r[...] for r in x_refs[0::3])
    x_prev = sum(r[...] for r in x_refs[1::3])
    x_next = sum(r[...] for r in x_refs[2::3])
    j = pl.program_id(1)
    nj = pl.num_programs(1)
    g1 = g1_ref[...]
    h = _rms(x_cur, g1)
    hb = h.astype(BF16)

    zs = _dot(hb, win_ref[:, 0:C_SGU])
    zs = 0.5 * zs * (1.0 + lax.erf(zs * (1.0 / math.sqrt(2.0))))
    us = zs[:, :SGU_WIDTH]
    vs = zs[:, SGU_WIDTH:]
    mu = jnp.mean(vs, axis=-1, keepdims=True)
    vc = vs - mu
    var = jnp.mean(vc * vc, axis=-1, keepdims=True)
    vn = (vc * lax.rsqrt(var + LN_EPS) * lng_ref[...] + lnb_ref[...]).astype(BF16)
    gd = SGU_WIDTH // SGU_GROUPS
    chunks = []
    for c in range(tm // SGU_CHUNK):
        groups = []
        for g in range(SGU_GROUPS):
            blk = vn[c * SGU_CHUNK:(c + 1) * SGU_CHUNK, g * gd:(g + 1) * gd]
            groups.append(_dot(sw_ref[g], blk))
        chunks.append(jnp.concatenate(groups, axis=1) + sbt_ref[...])
    a = us * jnp.concatenate(chunks, axis=0)

    q = _dot(hb, win_ref[:, C_HY:C_Q])
    scale = 1.0 / math.sqrt(XA_HEAD_DIM)
    heads = []
    for hd in range(XA_HEADS):
        sl = slice(hd * XA_HEAD_DIM, (hd + 1) * XA_HEAD_DIM)
        s = _dot_nt(q[:, sl].astype(BF16), k_ref[:, sl]) * scale
        s = s - jnp.max(s, axis=-1, keepdims=True)
        p = jnp.exp(s)
        p = p / jnp.sum(p, axis=-1, keepdims=True)
        heads.append(_dot(p.astype(BF16), v_ref[:, sl]))
    c_att = jnp.concatenate(heads, axis=1)

    gates = jax.nn.sigmoid(_dot(hb, win_ref[:, C_Q:]) + gb_ref[...])
    pa = _dot(a.astype(BF16), wba_ref[...])
    pc = _dot(c_att.astype(BF16), wbc_ref[...])
    mp_ref[...] = (gates[:, 0:D_MODEL] * pa + gates[:, 2 * D_MODEL:] * pc).astype(BF16)
    gh_ref[...] = gates[:, D_MODEL:2 * D_MODEL].astype(BF16)

    hprev = _rms(x_prev, g1).astype(BF16)
    hnext = _rms(x_next, g1).astype(BF16)
    hext = jnp.concatenate([hprev, hb, hnext], axis=0)
    pe = _dot(hext, win_ref[:, C_SGU:C_HY])
    rows = lax.broadcasted_iota(jnp.int32, (tm, 1), 0)
    pm1 = jnp.where((rows == 0) & (j == 0), 0.0, pe[SUBLANES - 1:SUBLANES - 1 + tm])
    pp1 = jnp.where((rows == tm - 1) & (j == nj - 1), 0.0, pe[SUBLANES + 1:SUBLANES + 1 + tm])
    p0 = pe[SUBLANES:SUBLANES + tm]
    pconv = cw_ref[0:1, :] * pm1 + cw_ref[1:2, :] * p0 + cw_ref[2:3, :] * pp1 + cb_ref[...]
    x0 = pconv[:, :HY_WIDTH]
    u = pconv[:, HY_WIDTH:2 * HY_WIDTH] * pconv[:, 2 * HY_WIDTH:]
    for c in range(HY_WIDTH // HY_CT):
        x0_ref[c] = x0[:, c * HY_CT:(c + 1) * HY_CT]
        u_ref[c] = u[:, c * HY_CT:(c + 1) * HY_CT]

    for src, dst in ((we1_ref, wb1_ref), (we3_ref, wb3_ref), (we2_ref, wb2_ref)):
        dst[...] = src[...].astype(BF16)


def _mixer_a_call(l, xs, norm1_g, w_in_bf, gate_b, sgu_ln_g, sgu_ln_b, sgu_w_bf, sgu_bt, conv_w, conv_b,
                  k_all, v_all, w_branch_bf, expert_ws):
    B, L, D = xs[0].shape
    tm = 512
    nct = HY_WIDTH // HY_CT
    nsteps = B * (L // tm)
    e_rows = expert_ws[0].shape[1] * expert_ws[0].shape[2]
    e_cols = expert_ws[0].shape[3]
    assert all(w.shape[1] * w.shape[2] == e_rows and w.shape[3] == e_cols for w in expert_ws)
    slab = e_rows // nsteps
    assert slab * nsteps == e_rows and slab % (2 * SUBLANES) == 0
    e_flat = [w.reshape(w.shape[0], e_rows, e_cols) for w in expert_ws]
    step = lambda b, j: b * (L // tm) + j
    in_cols = w_in_bf.shape[-1]
    hb_per_tile = tm // SUBLANES
    n_hb = L // SUBLANES
    const = lambda shape: pl.BlockSpec((None,) + shape, lambda b, j: (l,) + (0,) * len(shape),
                                       pipeline_mode=pl.Buffered(1))
    x_specs = [
        pl.BlockSpec((None, tm, D), lambda b, j: (b, j, 0)),
        pl.BlockSpec((None, SUBLANES, D), lambda b, j: (b, jnp.maximum(j * hb_per_tile - 1, 0), 0)),
        pl.BlockSpec((None, SUBLANES, D), lambda b, j: (b, jnp.minimum((j + 1) * hb_per_tile, n_hb - 1), 0)),
    ]
    x_args = [a for x in xs for a in (x, x, x)]
    return pl.pallas_call(
        functools.partial(_mixer_a_body, tm=tm, n_x=len(xs)),
        grid=(B, L // tm),
        in_specs=x_specs * len(xs) + [
            const((1, D)),
            const((D, in_cols)),
            const((1, 3 * D)),
            const((1, SGU_WIDTH)),
            const((1, SGU_WIDTH)),
            const((SGU_GROUPS, SGU_CHUNK, SGU_CHUNK)),
            const((SGU_CHUNK, SGU_WIDTH)),
            const((3, 3 * HY_WIDTH)),
            const((1, 3 * HY_WIDTH)),
            pl.BlockSpec((None, None, k_all.shape[2], XA_WIDTH), lambda b, j: (l, b, 0, 0)),
            pl.BlockSpec((None, None, k_all.shape[2], XA_WIDTH), lambda b, j: (l, b, 0, 0)),
            pl.BlockSpec((None, None, SGU_WIDTH, D), lambda b, j: (l, 0, 0, 0), pipeline_mode=pl.Buffered(1)),
            pl.BlockSpec((None, None, XA_WIDTH, D), lambda b, j: (l, 2, 0, 0), pipeline_mode=pl.Buffered(1)),
        ] + [pl.BlockSpec((None, slab, e_cols), lambda b, j: (l, step(b, j), 0))] * len(e_flat),
        out_specs=[
            pl.BlockSpec((None, tm, D), lambda b, j: (b, j, 0)),
            pl.BlockSpec((None, tm, D), lambda b, j: (b, j, 0)),
            pl.BlockSpec((None, nct, tm, HY_CT), lambda b, j: (b, 0, j, 0)),
            pl.BlockSpec((None, nct, tm, HY_CT), lambda b, j: (b, 0, j, 0)),
        ] + [pl.BlockSpec((slab, e_cols), lambda b, j: (step(b, j), 0))] * len(e_flat),
        out_shape=[
            jax.ShapeDtypeStruct((B, L, D), BF16),
            jax.ShapeDtypeStruct((B, L, D), BF16),
            jax.ShapeDtypeStruct((B, nct, L, HY_CT), F32),
            jax.ShapeDtypeStruct((B, nct, L, HY_CT), F32),
        ] + [jax.ShapeDtypeStruct((e_rows, e_cols), BF16)] * len(e_flat),
        compiler_params=_cparams(("arbitrary", "arbitrary")),
        name="mixer_a",
    )(*x_args, norm1_g, w_in_bf, gate_b, sgu_ln_g, sgu_ln_b, sgu_w_bf, sgu_bt, conv_w, conv_b,
      k_all, v_all, w_branch_bf, w_branch_bf, *e_flat)


def _fft_constants(seq_len):
    n = 2 * seq_len
    n2 = FFT_N2
    n1 = n // n2
    nz = seq_len // n2
    lo = FFT_ROWS // nz
    tiles = n2 // lo
    k1n = n1 // 2 + 1
    assert lo == SUBLANES and nz * lo == FFT_ROWS
    k1 = np.arange(k1n).reshape(k1n, 1, 1, 1, 1).astype(np.float64)
    part = np.arange(2).reshape(1, 2, 1, 1, 1)
    a = np.arange(lo).reshape(1, 1, lo, 1, 1)
    m1 = np.arange(nz).reshape(1, 1, 1, nz, 1).astype(np.float64)
    a2 = np.arange(lo).reshape(1, 1, 1, 1, lo)
    theta = 2.0 * np.pi * m1 * k1 / n1
    fwd = (np.where(part == 0, np.cos(theta), -np.sin(theta)) * (a == a2)).reshape(k1n * 2 * lo, nz * lo)
    weight = np.where((np.arange(k1n) == 0) | (np.arange(k1n) == n1 // 2), 1.0, 2.0) / n
    inv = (fwd.reshape(k1n, 2 * lo * nz * lo) * weight.reshape(k1n, 1)).reshape(k1n * 2 * lo, nz * lo).T
    n2_idx = (lo * np.arange(tiles).reshape(tiles, 1, 1) + np.arange(lo).reshape(1, lo, 1)).astype(np.float64)
    ang = 2.0 * np.pi * n2_idx * np.arange(k1n).reshape(1, 1, k1n) / n
    tw_cos = np.zeros((tiles, lo, LANES), np.float64)
    tw_sin = np.zeros((tiles, lo, LANES), np.float64)
    tw_cos[:, :, :k1n] = np.cos(ang)
    tw_sin[:, :, :k1n] = np.sin(ang)
    phi = 2.0 * np.pi * np.outer(np.arange(n2), np.arange(n2)) / n2
    c, s = np.cos(phi), np.sin(phi)
    g_fwd = np.block([[c, s], [-s, c]])
    g_inv = np.block([[c, -s], [s, c]])
    f32 = lambda v: jnp.asarray(v.astype(np.float32))
    bf = lambda v: f32(v).astype(BF16)
    return bf(fwd), bf(inv), bf(g_fwd), bf(g_inv), f32(tw_cos), f32(tw_sin), (tiles, k1n, lo, nz)


def _hyena_body(filt_ref, u_ref, x0_ref, bias_ref, fa_ref, ia_ref, gf_ref, gi_ref, twc_ref, tws_ref,
                o_ref, z_ref, k_ref, *, tiles, k1n, lo, nz):
    s = pl.program_id(1)
    half = FFT_N2
    rows = nz * lo
    ct = HY_CT

    def twiddles(j):
        c, sn = twc_ref[j], tws_ref[j]
        bc = lambda t: jnp.stack([jnp.broadcast_to(t[:, k1:k1 + 1], (lo, ct)) for k1 in range(k1n)])
        return bc(c), bc(sn)

    def stage_a(src_ref):
        for j in range(tiles):
            tile = src_ref[:, j].reshape(rows, ct).astype(BF16)
            zb = _dot(fa_ref[...], tile).reshape(k1n, 2, lo, ct)
            zr, zi = zb[:, 0], zb[:, 1]
            c, sn = twiddles(j)
            z_ref[:, 0, j] = zr * c + zi * sn
            z_ref[:, 1, j] = zi * c - zr * sn

    def spectrum(k1):
        return _dot(gf_ref[...], z_ref[k1].reshape(2 * half, ct).astype(BF16))

    @pl.when(s == 0)
    def _():
        k_ref[...] = jnp.zeros(k_ref.shape, F32)

    @pl.when(s < 2)
    def _():
        stage_a(filt_ref)
        im_sign = jnp.where(s == 0, 1.0, -1.0).astype(F32)
        for k1 in range(k1n):
            x = spectrum(k1)
            k_ref[k1] += jnp.concatenate([x[:half], im_sign * x[half:]], axis=0)

    @pl.when(s >= 2)
    def _():
        stage_a(u_ref)
        for k1 in range(k1n):
            x = spectrum(k1)
            xr, xi = x[:half], x[half:]
            kr, ki = k_ref[k1, :half], k_ref[k1, half:]
            y = jnp.concatenate([xr * kr - xi * ki, xr * ki + xi * kr], axis=0).astype(BF16)
            z_ref[k1] = _dot(gi_ref[...], y).reshape(2, tiles, lo, ct)
        bias = bias_ref[...]
        for j in range(tiles):
            vr, vi = z_ref[:, 0, j], z_ref[:, 1, j]
            c, sn = twiddles(j)
            wr = vr * c - vi * sn
            wi = vi * c + vr * sn
            zt = jnp.stack([wr, wi], axis=1).reshape(k1n * 2 * lo, ct).astype(BF16)
            y = _dot(ia_ref[...], zt)
            uu = u_ref[:, j].reshape(rows, ct)
            x0 = x0_ref[:, j].reshape(rows, ct)
            o_ref[:, j] = (x0 * (y + bias * uu)).reshape(nz, lo, ct)


def _hyena_call(l, filt, u, x0, hy_bias, consts):
    fa, ia, gf, gi, twc, tws, (tiles, k1n, lo, nz) = consts
    B, nct, L, ct = u.shape
    view = lambda a: a.reshape(a.shape[:-2] + (nz, tiles, lo, ct))
    seq = (nz, tiles, lo, ct)
    bidx = lambda s: jnp.maximum(s - 2, 0)
    single = lambda shape: pl.BlockSpec(shape, lambda c, s: (0,) * len(shape), pipeline_mode=pl.Buffered(1))
    out = pl.pallas_call(
        functools.partial(_hyena_body, tiles=tiles, k1n=k1n, lo=lo, nz=nz),
        grid=(nct, B + 2),
        in_specs=[
            pl.BlockSpec((None, None, None) + seq, lambda c, s: (l, jnp.minimum(s, 1), c, 0, 0, 0, 0),
                         pipeline_mode=pl.Buffered(1)),
            pl.BlockSpec((None, None) + seq, lambda c, s: (bidx(s), c, 0, 0, 0, 0)),
            pl.BlockSpec((None, None) + seq, lambda c, s: (bidx(s), c, 0, 0, 0, 0)),
            pl.BlockSpec((None, None, 1, ct), lambda c, s: (l, c, 0, 0)),
            single(fa.shape), single(ia.shape), single(gf.shape), single(gi.shape),
            single(twc.shape), single(tws.shape),
        ],
        out_specs=pl.BlockSpec((None, None) + seq, lambda c, s: (bidx(s), c, 0, 0, 0, 0)),
        out_shape=jax.ShapeDtypeStruct((B, nct) + seq, F32),
        scratch_shapes=[
            pltpu.VMEM((k1n, 2, tiles, lo, ct), F32),
            pltpu.VMEM((k1n, 2 * FFT_N2, ct), F32),
        ],
        compiler_params=_cparams(("arbitrary", "arbitrary")),
        name="hyena_fft",
    )(view(filt), view(u), view(x0), hy_bias, fa, ia, gf, gi, twc, tws)
    return out.reshape(B, nct, L, HY_CT)


def _pack_bf16_pairs(x):
    n = x.shape[1] // 2
    return pltpu.pack_elementwise([x[:, :n], x[:, n:]], packed_dtype=BF16)


def _unpack_bf16_pairs(p):
    lo = pltpu.unpack_elementwise(p, index=0, packed_dtype=BF16, unpacked_dtype=F32)
    hi = pltpu.unpack_elementwise(p, index=1, packed_dtype=BF16, unpacked_dtype=F32)
    return jnp.concatenate([lo, hi], axis=1).astype(BF16)


def _mixer_c_body(*refs, n_x):
    x_refs, refs = refs[:n_x], refs[n_x:]
    (mp_ref, gh_ref, hy_ref, wbh_ref, wo_ref, g2_ref, wr_ref, x1_ref, hp_ref, afft_ref) = refs
    hy = jnp.concatenate([hy_ref[c] for c in range(HY_WIDTH // HY_CT)], axis=1).astype(BF16)
    merged = mp_ref[...].astype(F32) + gh_ref[...].astype(F32) * _dot(hy, wbh_ref[...])
    x1 = sum(r[...] for r in x_refs) + _dot(merged.astype(BF16), wo_ref[...])
    x1_ref[...] = x1
    h2 = _rms(x1, g2_ref[...])
    hp_ref[...] = _pack_bf16_pairs(h2)
    tm = h2.shape[0]
    hh, hl = _split_bf16(h2)
    wh, wl = _split_bf16(wr_ref[...])
    prod = _dot(jnp.concatenate([hh, hl], axis=0), jnp.concatenate([wh, wl], axis=1))
    logits = prod[:tm, :LANES] + prod[:tm, LANES:] + prod[tm:, :LANES]
    lane = lax.broadcasted_iota(jnp.int32, logits.shape, 1)
    logits = jnp.where(lane < N_EXPERTS, logits, -jnp.inf)
    e = jnp.exp(logits - jnp.max(logits, axis=-1, keepdims=True))
    aff = e / jnp.sum(e, axis=-1, keepdims=True)
    afft_ref[...] = aff.T[:N_EXPERTS, :]


def _mixer_c_call(l, xs, mp, gh, hy, w_branch_bf, w_out_bf, norm2_g, wr_pad):
    B, L, D = xs[0].shape
    tm = 512
    nct = HY_WIDTH // HY_CT
    const = lambda shape: pl.BlockSpec((None,) + shape, lambda b, j: (l,) + (0,) * len(shape),
                                       pipeline_mode=pl.Buffered(1))
    return pl.pallas_call(
        functools.partial(_mixer_c_body, n_x=len(xs)),
        grid=(B, L // tm),
        in_specs=[pl.BlockSpec((None, tm, D), lambda b, j: (b, j, 0))] * len(xs) + [
            pl.BlockSpec((None, tm, D), lambda b, j: (b, j, 0)),
            pl.BlockSpec((None, tm, D), lambda b, j: (b, j, 0)),
            pl.BlockSpec((None, nct, tm, HY_CT), lambda b, j: (b, 0, j, 0)),
            pl.BlockSpec((None, None, HY_WIDTH, D), lambda b, j: (l, 1, 0, 0), pipeline_mode=pl.Buffered(1)),
            const((D, D)),
            const((1, D)),
            const((D, LANES)),
        ],
        out_specs=[
            pl.BlockSpec((None, tm, D), lambda b, j: (b, j, 0)),
            pl.BlockSpec((None, tm, D // 2), lambda b, j: (b, j, 0)),
            pl.BlockSpec((None, N_EXPERTS, tm), lambda b, j: (b, 0, j)),
        ],
        out_shape=[
            jax.ShapeDtypeStruct((B, L, D), F32),
            jax.ShapeDtypeStruct((B, L, D // 2), jnp.uint32),
            jax.ShapeDtypeStruct((B, N_EXPERTS, L), F32),
        ],
        compiler_params=_cparams(("arbitrary", "arbitrary")),
        name="mixer_c",
    )(*xs, mp, gh, hy, w_branch_bf, w_out_bf, norm2_g, wr_pad)


def _cumsum_lanes(x, tri):
    outs = []
    carry = jnp.zeros((x.shape[0], 1), F32)
    for blk in range(x.shape[1] // LANES):
        loc = _dot(x[:, blk * LANES:(blk + 1) * LANES].astype(BF16), tri) + carry
        outs.append(loc)
        carry = loc[:, LANES - 1:LANES]
    return jnp.concatenate(outs, axis=1)


def _route_body(afft_ref, tri_ref, idx_ref, gate_ref, sel_ref, thr_ref, *, cap):
    b = pl.program_id(0)
    nb, ne, seq = afft_ref.shape

    @pl.when(b == 0)
    def _():
        a_all = afft_ref[...].reshape(nb * ne, seq)

        def bit_step(i, v):
            cand = v | jnp.left_shift(jnp.int32(1), 30 - i)
            cnt = jnp.sum((a_all >= lax.bitcast_convert_type(cand, F32)).astype(F32), axis=1, keepdims=True)
            return jnp.where(cnt >= cap, cand, v)

        bits = lax.fori_loop(0, 31, bit_step, jnp.zeros((nb * ne, 1), jnp.int32))
        thr_ref[...] = jnp.broadcast_to(lax.bitcast_convert_type(bits, F32), thr_ref.shape)

    a = afft_ref[b]
    thr = thr_ref[pl.ds(pl.multiple_of(b * ne, ne), ne), 0:1]
    gt = a > thr
    eq = a == thr
    need = cap - jnp.sum(gt.astype(F32), axis=1, keepdims=True)
    tri = tri_ref[...]
    cum_eq = _cumsum_lanes(eq.astype(F32), tri)
    sel = gt | (eq & (cum_eq <= need))
    sel_ref[...] = sel.astype(F32)

    nblk = seq // LANES
    f32c = lambda v: jnp.where(v, 1.0, 0.0)
    lane = lax.broadcasted_iota(jnp.int32, (cap, LANES), 1)
    slot = lax.broadcasted_iota(jnp.int32, (cap, LANES), 0).astype(F32)
    ones_row = jnp.ones((SUBLANES, LANES), BF16)
    bi = lax.broadcasted_iota(jnp.int32, (nblk, LANES), 0)
    bj = lax.broadcasted_iota(jnp.int32, (nblk, LANES), 1)
    upper = f32c((bi <= bj) | (bj >= nblk)).astype(BF16)
    lower = f32c(bj < bi)[:, :nblk].astype(BF16)
    half = float(cap // 2)
    experts = range(ne)
    selb = [jnp.concatenate([sel_ref[e:e + 1, j * LANES:(j + 1) * LANES] for j in range(nblk)], axis=0).astype(BF16)
            for e in experts]
    loc = [_dot(selb[e], tri) for e in experts]
    tot_row = [_dot_nt(ones_row, selb[e])[:, :nblk].astype(BF16) for e in experts]
    incl_row = [_dot(tot_row[e], upper)[0:1, :] for e in experts]
    excl = [_dot(lower, jnp.broadcast_to(loc[e][:, LANES - 1:LANES], (nblk, LANES)).astype(BF16)) for e in experts]
    jb = [jnp.sum(f32c(incl_row[e] <= slot), axis=1, keepdims=True) for e in experts]
    onehot = [f32c(jb[e].astype(jnp.int32) == lane).astype(BF16) for e in experts]
    def affinity_terms(e):
        blk = jnp.concatenate([afft_ref[b, e:e + 1, j * LANES:(j + 1) * LANES] for j in range(nblk)], axis=0)
        t1 = blk.astype(BF16)
        t2 = (blk - t1.astype(F32)).astype(BF16)
        t3 = (blk - t1.astype(F32) - t2.astype(F32)).astype(BF16)
        return [t1, t2, t3]

    ncol = 6 * LANES
    pad = jnp.zeros((LANES - nblk, ncol), BF16)
    table = [jnp.concatenate([jnp.concatenate([loc[e].astype(BF16), jnp.minimum(excl[e], half).astype(BF16),
                                               jnp.maximum(excl[e] - half, 0.0).astype(BF16)]
                                              + affinity_terms(e), axis=1), pad], axis=0)
             for e in experts]
    got = [_dot(onehot[e], table[e]) for e in experts]
    idx_t = jnp.zeros((cap, LANES), F32)
    gate_t = jnp.zeros((cap, LANES), F32)
    for e in experts:
        g = got[e]
        local = slot - (g[:, LANES:2 * LANES] + g[:, 2 * LANES:3 * LANES])
        pos = jnp.sum(f32c(g[:, :LANES] <= local), axis=1, keepdims=True)
        aff_row = g[:, 3 * LANES:4 * LANES] + g[:, 4 * LANES:5 * LANES] + g[:, 5 * LANES:]
        gate = jnp.sum(jnp.where(lane == pos.astype(jnp.int32), aff_row, 0.0), axis=1, keepdims=True)
        idx_t = idx_t + jnp.where(lane == e, jb[e] * float(LANES) + pos, 0.0)
        gate_t = gate_t + jnp.where(lane == e, gate, 0.0)
    idx_ref[...] = idx_t.T[:ne, :].astype(jnp.int32)
    gate_ref[...] = gate_t


def _route_call(afft, cap):
    B, ne, L = afft.shape
    tri = jnp.asarray(np.triu(np.ones((LANES, LANES), np.float32))).astype(BF16)
    return pl.pallas_call(
        functools.partial(_route_body, cap=cap),
        grid=(B,),
        in_specs=[
            pl.BlockSpec((B, ne, L), lambda b: (0, 0, 0)),
            pl.BlockSpec((LANES, LANES), lambda b: (0, 0)),
        ],
        out_specs=[pl.BlockSpec((None, ne, cap), lambda b: (b, 0, 0)),
                   pl.BlockSpec((None, cap, LANES), lambda b: (b, 0, 0))],
        out_shape=[jax.ShapeDtypeStruct((B, ne, cap), jnp.int32),
                   jax.ShapeDtypeStruct((B, cap, LANES), F32)],
        scratch_shapes=[pltpu.VMEM((ne, L), F32), pltpu.VMEM((B * ne, LANES), F32)],
        compiler_params=_cparams(("arbitrary",)),
        name="route",
    )(afft, tri)


def _moe_body(idx_ref, idxp_ref, hp_ref, gates_ref, w1_ref, w3_ref, w2_ref, out_hbm,
              acc_ref, xin_ref, y_ref, sem_ref, *, cap, ne, npairs):
    i = pl.program_id(0)
    slot = i % 2
    e = jnp.minimum(i, npairs - 1) % ne

    @pl.when(i == 0)
    def _():
        acc_ref[...] = jnp.zeros(acc_ref.shape, F32)
        y_ref[...] = jnp.zeros(y_ref.shape, F32)

    y_prev = y_ref.at[1 - slot]
    for g0 in range(0, cap, SUBLANES):
        y_tile = y_prev[pl.ds(g0, SUBLANES), :]
        for k0 in range(0, SUBLANES, MOE_RMW_ROWS):
            ks = range(k0, k0 + MOE_RMW_ROWS)
            toks = [idxp_ref[0, 0, g0 + k] for k in ks]
            vals = [acc_ref[pl.ds(t, 1), :] + y_tile[k:k + 1, :] for t, k in zip(toks, ks)]
            for t, v in zip(toks, vals):
                acc_ref[pl.ds(t, 1), :] = v

    for s in range(cap):
        xin_ref[pl.ds(s, 1), :] = hp_ref[pl.ds(idx_ref[0, 0, s], 1), :]

    gates = gates_ref[...]
    lane = lax.broadcasted_iota(jnp.int32, gates.shape, 1)
    gate = jnp.sum(jnp.where(lane == e, gates, 0.0), axis=1, keepdims=True)
    xb = _unpack_bf16_pairs(xin_ref[...])
    a1 = _dot(xb, w1_ref[...])
    a3 = _dot(xb, w3_ref[...])
    hid = (a1 * jax.nn.sigmoid(a1) * a3).astype(BF16)
    y_ref[slot] = _dot(hid, w2_ref[...]) * gate

    @pl.when((i > 0) & (i % ne == 0))
    def _():
        cp = pltpu.make_async_copy(acc_ref, out_hbm.at[i // ne - 1], sem_ref.at[0])
        cp.start()
        cp.wait()
        acc_ref[...] = jnp.zeros(acc_ref.shape, F32)


def _moe_call(idx, gates, hp, w1_rows, w3_rows, w2_rows):
    B, ne, cap = idx.shape
    L = hp.shape[1]
    ff = w1_rows.shape[-1]
    w1_bf, w3_bf = (w.reshape(ne, D_MODEL, ff) for w in (w1_rows, w3_rows))
    w2_bf = w2_rows.reshape(ne, ff, D_MODEL)
    npairs = B * ne
    idx3 = idx.reshape(npairs, 1, cap)
    pair = lambda i: jnp.minimum(i, npairs - 1)
    prev = lambda i: jnp.maximum(i - 1, 0)
    return pl.pallas_call(
        functools.partial(_moe_body, cap=cap, ne=ne, npairs=npairs),
        grid=(npairs + 1,),
        in_specs=[
            pl.BlockSpec((1, 1, cap), lambda i: (pair(i), 0, 0), memory_space=pltpu.SMEM),
            pl.BlockSpec((1, 1, cap), lambda i: (prev(i), 0, 0), memory_space=pltpu.SMEM),
            pl.BlockSpec((None, L, D_MODEL // 2), lambda i: (pair(i) // ne, 0, 0)),
            pl.BlockSpec((None, cap, LANES), lambda i: (pair(i) // ne, 0, 0)),
            pl.BlockSpec((None, D_MODEL, ff), lambda i: (pair(i) % ne, 0, 0)),
            pl.BlockSpec((None, D_MODEL, ff), lambda i: (pair(i) % ne, 0, 0)),
            pl.BlockSpec((None, ff, D_MODEL), lambda i: (pair(i) % ne, 0, 0)),
        ],
        out_specs=pl.BlockSpec(memory_space=pl.ANY),
        out_shape=jax.ShapeDtypeStruct((B, L, D_MODEL), F32),
        scratch_shapes=[
            pltpu.VMEM((L, D_MODEL), F32),
            pltpu.VMEM((cap, D_MODEL // 2), jnp.uint32),
            pltpu.VMEM((2, cap, D_MODEL), F32),
            pltpu.SemaphoreType.DMA((1,)),
        ],
        compiler_params=_cparams(("arbitrary",)),
        name="moe",
    )(idx3, idx3, hp, gates, w1_bf, w3_bf, w2_bf)


def _final_body(*refs):
    g_ref, o_ref = refs[-2:]
    o_ref[...] = _rms(sum(r[...] for r in refs[:-2]), g_ref[...])


def _final_call(xs, g):
    B, L, D = xs[0].shape
    tm = 1024
    return pl.pallas_call(
        _final_body,
        grid=(B, L // tm),
        in_specs=[pl.BlockSpec((None, tm, D), lambda b, j: (b, j, 0))] * len(xs)
        + [pl.BlockSpec((1, D), lambda b, j: (0, 0))],
        out_specs=pl.BlockSpec((None, tm, D), lambda b, j: (b, j, 0)),
        out_shape=jax.ShapeDtypeStruct((B, L, D), F32),
        compiler_params=_cparams(("arbitrary", "arbitrary")),
        name="final_norm",
    )(*xs, g.reshape(1, D))


def kernel(x, mem, norm1_g, w_in, gate_b, sgu_ln_g, sgu_ln_b, sgu_w, sgu_b, hy_conv_w, hy_conv_b, hy_f_w1, hy_f_b1, hy_f_w2, hy_f_b2, hy_f_w3, hy_f_b3, hy_f_w4, hy_f_freq, hy_bias, mem_norm_g, w_kv, w_branch, w_out, norm2_g, w_router, w1, w3, w2, final_g):
    B, L, D = x.shape
    depth = w_in.shape[0]
    cap = EC_CAPACITY * L // N_EXPERTS
    nct = HY_WIDTH // HY_CT
    row = lambda a: a.reshape(depth, 1, a.shape[-1])

    w_in_bf = w_in.astype(BF16)
    w_kv_bf = w_kv.astype(BF16)
    w_branch_bf = w_branch.astype(BF16)
    w_out_bf = w_out.astype(BF16)
    sgu_w_bf = sgu_w.astype(BF16)
    sgu_bt = jnp.repeat(jnp.swapaxes(sgu_b, 1, 2), SGU_WIDTH // SGU_GROUPS, axis=2)
    wr_pad = jnp.pad(w_router, ((0, 0), (0, 0), (0, LANES - N_EXPERTS)))
    hy_bias4 = hy_bias.reshape(depth, nct, 1, HY_CT)

    k_all, v_all = _kv_call(mem, mem_norm_g, w_kv_bf)
    filt = _filter_call(L, hy_f_w1, hy_f_b1, hy_f_w2, hy_f_b2, hy_f_w3, hy_f_b3, hy_f_w4, hy_f_freq)
    consts = _fft_constants(L)

    xs = [x]
    for l in range(depth):
        mp, gh, x0, u, w1_bf, w3_bf, w2_bf = _mixer_a_call(
            l, xs, row(norm1_g), w_in_bf, row(gate_b), row(sgu_ln_g), row(sgu_ln_b), sgu_w_bf, sgu_bt,
            hy_conv_w, row(hy_conv_b), k_all, v_all, w_branch_bf, (w1, w3, w2))
        hy = _hyena_call(l, filt, u, x0, hy_bias4, consts)
        x1, hp, afft = _mixer_c_call(l, xs, mp, gh, hy, w_branch_bf, w_out_bf, row(norm2_g), wr_pad)
        idx, gates = _route_call(afft, cap)
        xs = [x1, _moe_call(idx, gates, hp, w1_bf, w3_bf, w2_bf)]
    return _final_call(xs, final_g)
```

```python
import functools
import math

import numpy as np
import jax
import jax.numpy as jnp
from jax import lax
from jax.experimental import pallas as pl
from jax.experimental.pallas import tpu as pltpu

F32 = jnp.float32
BF16 = jnp.bfloat16

D_MODEL = 1024
SGU_WIDTH = 512
SGU_GROUPS = 4
SGU_CHUNK = 128
HY_WIDTH = 512
HY_EMB_DIM = 33
HY_BANDS = (HY_EMB_DIM - 1) // 2
HY_FILTER_HIDDEN = 64
HY_FAST_DECAY_PCT = 0.3
HY_SLOW_DECAY_PCT = 1.5
HY_DECAY_TARGET = 1e-2
XA_HEADS = 4
XA_HEAD_DIM = 128
XA_WIDTH = XA_HEADS * XA_HEAD_DIM
N_EXPERTS = 16
EC_CAPACITY = 2
NORM_EPS = 1e-6
LN_EPS = 1e-5

C_SGU = 2 * SGU_WIDTH
C_HY = C_SGU + 3 * HY_WIDTH
C_Q = C_HY + XA_WIDTH

LANES = 128
SUBLANES = 8
VMEM_LIMIT = 56 * 1024 * 1024

FFT_N2 = 128
FFT_ROWS = 256
HY_CT = 256

MOE_RMW_ROWS = 8


def _cparams(semantics, flags=None):
    return pltpu.CompilerParams(dimension_semantics=semantics, vmem_limit_bytes=VMEM_LIMIT, flags=flags)


def _rms(x, g):
    return x * lax.rsqrt(jnp.mean(x * x, axis=-1, keepdims=True) + NORM_EPS) * g


def _dot(a, b):
    return jnp.dot(a, b, preferred_element_type=F32)


def _dot_nt(a, b):
    return lax.dot_general(a, b, (((1,), (1,)), ((), ())), preferred_element_type=F32)


def _split_bf16(x):
    hi = x.astype(BF16)
    lo = (x - hi.astype(F32)).astype(BF16)
    return hi, lo


def _kv_body(mem_ref, g_ref, w_ref, k_ref, v_ref):
    m = _rms(mem_ref[...], g_ref[...]).astype(BF16)
    kv = _dot(m, w_ref[...])
    k_ref[...] = kv[:, :XA_WIDTH].astype(BF16)
    v_ref[...] = kv[:, XA_WIDTH:].astype(BF16)


def _kv_call(mem, mem_norm_g, w_kv_bf):
    B, M, D = mem.shape
    depth = w_kv_bf.shape[0]
    return pl.pallas_call(
        _kv_body,
        grid=(depth, B),
        in_specs=[
            pl.BlockSpec((None, M, D), lambda l, b: (b, 0, 0)),
            pl.BlockSpec((None, 1, D), lambda l, b: (l, 0, 0)),
            pl.BlockSpec((None, D, 2 * XA_WIDTH), lambda l, b: (l, 0, 0)),
        ],
        out_specs=[
            pl.BlockSpec((None, None, M, XA_WIDTH), lambda l, b: (l, b, 0, 0)),
            pl.BlockSpec((None, None, M, XA_WIDTH), lambda l, b: (l, b, 0, 0)),
        ],
        out_shape=[jax.ShapeDtypeStruct((depth, B, M, XA_WIDTH), BF16)] * 2,
        compiler_params=_cparams(("arbitrary", "arbitrary")),
        name="kv",
    )(mem, mem_norm_g.reshape(depth, 1, D), w_kv_bf)


FILT_GROUPS = LANES // HY_BANDS


def _filter_body(w1t_ref, w1c_ref, w1s_ref, b1_ref, w2_ref, b2_ref, w3_ref, b3_ref, w4_ref, fr_ref,
                 bands_ref, deltas_ref, win_ref, o_ref, winb_ref, *, seq_len, rows):
    winb_ref[...] = win_ref[...].astype(BF16)
    r0 = pl.program_id(1) * rows
    ng, hid = FILT_GROUPS, HY_FILTER_HIDDEN
    rg = rows // ng

    def lags(width, per_group):
        return (r0 + lax.broadcasted_iota(jnp.int32, (rg, width), 0)
                + (lax.broadcasted_iota(jnp.int32, (rg, width), 1) // per_group) * rg)

    pos_b = lags(LANES, HY_BANDS).astype(F32)
    lag = lags(ng * hid, hid)
    t = lag.astype(F32) * (1.0 / (seq_len - 1))
    ang = pos_b * (2.0 * math.pi / seq_len) * bands_ref[...]
    hp = lax.Precision.HIGHEST
    dot = lambda a, w_ref: jnp.dot(a, w_ref[...], precision=hp, preferred_element_type=F32)
    fr = fr_ref[...]
    h = jnp.sin(fr * (t * w1t_ref[...] + dot(jnp.cos(ang), w1c_ref) - dot(jnp.sin(ang), w1s_ref) + b1_ref[...]))
    h = jnp.sin(fr * (dot(h, w2_ref) + b2_ref[...]))
    h = jnp.sin(fr * (dot(h, w3_ref) + b3_ref[...]))
    for g in range(ng):
        og = dot(h[:, g * hid:(g + 1) * hid], w4_ref)
        decay = jnp.exp(-t[:, g * hid:g * hid + 1] * deltas_ref[...])
        fwd = og[:, :HY_WIDTH] * decay
        bwd = jnp.where(lag[:, g * hid:g * hid + 1] == 0, 0.0, og[:, HY_WIDTH:] * decay)
        for c in range(HY_WIDTH // HY_CT):
            o_ref[0, c, g * rg:(g + 1) * rg, :] = fwd[:, c * HY_CT:(c + 1) * HY_CT]
            o_ref[1, c, g * rg:(g + 1) * rg, :] = bwd[:, c * HY_CT:(c + 1) * HY_CT]


def _filter_call(seq_len, w1, b1, w2, b2, w3, b3, w4, freq, w_in):
    depth = w1.shape[0]
    H = HY_FILTER_HIDDEN
    ng = FILT_GROUPS
    rows = 1024
    nsteps = seq_len // rows
    win_rows, win_cols = w_in.shape[1], w_in.shape[2]
    slab = win_rows // nsteps
    assert slab * nsteps == win_rows and slab % (2 * SUBLANES) == 0
    nct = HY_WIDTH // HY_CT
    bands = np.tile(np.linspace(1e-4, HY_BANDS - 1, HY_BANDS), ng).astype(np.float32).reshape(1, LANES)
    max_decay = math.log(HY_DECAY_TARGET) / HY_FAST_DECAY_PCT
    min_decay = math.log(HY_DECAY_TARGET) / HY_SLOW_DECAY_PCT
    deltas = np.abs(np.linspace(min_decay, max_decay, HY_WIDTH)).astype(np.float32).reshape(1, HY_WIDTH)
    eye = jnp.eye(ng, dtype=F32)
    bdiag = lambda w: jnp.einsum("gh,lij->lgihj", eye, w).reshape(depth, ng * w.shape[1], ng * w.shape[2])
    tiled = lambda v: jnp.tile(v.reshape(depth, 1, H), (1, 1, ng))
    w1t = tiled(w1[:, 0, :])
    w1c = bdiag(w1[:, 1:1 + HY_BANDS, :])
    w1s = bdiag(w1[:, 1 + HY_BANDS:, :])
    vec = lambda n: pl.BlockSpec((None, 1, n), lambda l, r: (l, 0, 0))
    mat = lambda m, n: pl.BlockSpec((None, m, n), lambda l, r: (l, 0, 0))
    gh = ng * H
    return pl.pallas_call(
        functools.partial(_filter_body, seq_len=seq_len, rows=rows),
        grid=(depth, seq_len // rows),
        in_specs=[
            vec(gh), mat(LANES, gh), mat(LANES, gh), vec(gh), mat(gh, gh), vec(gh), mat(gh, gh), vec(gh),
            mat(H, 2 * HY_WIDTH),
            vec(gh),
            pl.BlockSpec((1, LANES), lambda l, r: (0, 0)),
            pl.BlockSpec((1, HY_WIDTH), lambda l, r: (0, 0)),
            pl.BlockSpec((None, slab, win_cols), lambda l, r: (l, r, 0)),
        ],
        out_specs=[pl.BlockSpec((None, 2, nct, rows, HY_CT), lambda l, r: (l, 0, 0, r, 0)),
                   pl.BlockSpec((None, slab, win_cols), lambda l, r: (l, r, 0))],
        out_shape=[jax.ShapeDtypeStruct((depth, 2, nct, seq_len, HY_CT), F32),
                   jax.ShapeDtypeStruct(w_in.shape, BF16)],
        compiler_params=_cparams(("arbitrary", "arbitrary")),
        name="hyena_filter",
    )(w1t, w1c, w1s, tiled(b1), bdiag(w2), tiled(b2), bdiag(w3), tiled(b3), w4, tiled(freq),
      jnp.asarray(bands), jnp.asarray(deltas), w_in)


def _mixer_a_body(*refs, tm, n_x):
    x_refs, refs = refs[:3 * n_x], refs[3 * n_x:]
    (g1_ref, win_ref, gb_ref, lng_ref, lnb_ref, sw_ref, sbt_ref, cw_ref, cb_ref, k_ref, v_ref, wba_ref, wbc_ref,
     we1_ref, we3_ref, we2_ref, mp_ref, gh_ref, x0_ref, u_ref, wb1_ref, wb3_ref, wb2_ref) = refs
    x_cur = sum(r[...] for r in x_refs[0::3])
    x_prev = sum(r[...] for r in x_refs[1::3])
    x_next = sum(r[...] for r in x_refs[2::3])
    j = pl.program_id(1)
    nj = pl.num_programs(1)
    g1 = g1_ref[...]
    h = _rms(x_cur, g1)
    hb = h.astype(BF16)

    zs = _dot(hb, win_ref[:, 0:C_SGU])
    zs = 0.5 * zs * (1.0 + lax.erf(zs * (1.0 / math.sqrt(2.0))))
    us = zs[:, :SGU_WIDTH]
    vs = zs[:, SGU_WIDTH:]
    mu = jnp.mean(vs, axis=-1, keepdims=True)
    vc = vs - mu
    var = jnp.mean(vc * vc, axis=-1, keepdims=True)
    vn = (vc * lax.rsqrt(var + LN_EPS) * lng_ref[...] + lnb_ref[...]).astype(BF16)
    gd = SGU_WIDTH // SGU_GROUPS
    chunks = []
    for c in range(tm // SGU_CHUNK):
        groups = []
        for g in range(SGU_GROUPS):
            blk = vn[c * SGU_CHUNK:(c + 1) * SGU_CHUNK, g * gd:(g + 1) * gd]
            groups.append(_dot(sw_ref[g], blk))
        chunks.append(jnp.concatenate(groups, axis=1) + sbt_ref[...])
    a = us * jnp.concatenate(chunks, axis=0)

    q = _dot(hb, win_ref[:, C_HY:C_Q])
    scale = 1.0 / math.sqrt(XA_HEAD_DIM)
    heads = []
    for hd in range(XA_HEADS):
        sl = slice(hd * XA_HEAD_DIM, (hd + 1) * XA_HEAD_DIM)
        s = _dot_nt(q[:, sl].astype(BF16), k_ref[:, sl]) * scale
        s = s - jnp.max(s, axis=-1, keepdims=True)
        p = jnp.exp(s)
        p = p / jnp.sum(p, axis=-1, keepdims=True)
        heads.append(_dot(p.astype(BF16), v_ref[:, sl]))
    c_att = jnp.concatenate(heads, axis=1)

    gates = jax.nn.sigmoid(_dot(hb, win_ref[:, C_Q:]) + gb_ref[...])
    pa = _dot(a.astype(BF16), wba_ref[...])
    pc = _dot(c_att.astype(BF16), wbc_ref[...])
    mp_ref[...] = (gates[:, 0:D_MODEL] * pa + gates[:, 2 * D_MODEL:] * pc).astype(BF16)
    gh_ref[...] = gates[:, D_MODEL:2 * D_MODEL].astype(BF16)

    hprev = _rms(x_prev, g1).astype(BF16)
    hnext = _rms(x_next, g1).astype(BF16)
    hext = jnp.concatenate([hprev, hb, hnext], axis=0)
    pe = _dot(hext, win_ref[:, C_SGU:C_HY])
    rows = lax.broadcasted_iota(jnp.int32, (tm, 1), 0)
    pm1 = jnp.where((rows == 0) & (j == 0), 0.0, pe[SUBLANES - 1:SUBLANES - 1 + tm])
    pp1 = jnp.where((rows == tm - 1) & (j == nj - 1), 0.0, pe[SUBLANES + 1:SUBLANES + 1 + tm])
    p0 = pe[SUBLANES:SUBLANES + tm]
    pconv = cw_ref[0:1, :] * pm1 + cw_ref[1:2, :] * p0 + cw_ref[2:3, :] * pp1 + cb_ref[...]
    x0 = pconv[:, :HY_WIDTH]
    u = pconv[:, HY_WIDTH:2 * HY_WIDTH] * pconv[:, 2 * HY_WIDTH:]
    for c in range(HY_WIDTH // HY_CT):
        x0_ref[c] = x0[:, c * HY_CT:(c + 1) * HY_CT]
        u_ref[c] = u[:, c * HY_CT:(c + 1) * HY_CT]

    for src, dst in ((we1_ref, wb1_ref), (we3_ref, wb3_ref), (we2_ref, wb2_ref)):
        dst[...] = src[...].astype(BF16)


def _mixer_a_call(l, xs, norm1_g, w_in_bf, gate_b, sgu_ln_g, sgu_ln_b, sgu_w_bf, sgu_bt, conv_w, conv_b,
                  k_all, v_all, w_branch_bf, expert_ws):
    B, L, D = xs[0].shape
    tm = 512
    nct = HY_WIDTH // HY_CT
    nsteps = B * (L // tm)
    e_rows = expert_ws[0].shape[1] * expert_ws[0].shape[2]
    e_cols = expert_ws[0].shape[3]
    assert all(w.shape[1] * w.shape[2] == e_rows and w.shape[3] == e_cols for w in expert_ws)
    slab = e_rows // nsteps
    assert slab * nsteps == e_rows and slab % (2 * SUBLANES) == 0
    e_flat = [w.reshape(w.shape[0], e_rows, e_cols) for w in expert_ws]
    step = lambda b, j: b * (L // tm) + j
    in_cols = w_in_bf.shape[-1]
    hb_per_tile = tm // SUBLANES
    n_hb = L // SUBLANES
    const = lambda shape: pl.BlockSpec((None,) + shape, lambda b, j: (l,) + (0,) * len(shape),
                                       pipeline_mode=pl.Buffered(1))
    x_specs = [
        pl.BlockSpec((None, tm, D), lambda b, j: (b, j, 0)),
        pl.BlockSpec((None, SUBLANES, D), lambda b, j: (b, jnp.maximum(j * hb_per_tile - 1, 0), 0)),
        pl.BlockSpec((None, SUBLANES, D), lambda b, j: (b, jnp.minimum((j + 1) * hb_per_tile, n_hb - 1), 0)),
    ]
    x_args = [a for x in xs for a in (x, x, x)]
    return pl.pallas_call(
        functools.partial(_mixer_a_body, tm=tm, n_x=len(xs)),
        grid=(B, L // tm),
        in_specs=x_specs * len(xs) + [
            const((1, D)),
            const((D, in_cols)),
            const((1, 3 * D)),
            const((1, SGU_WIDTH)),
            const((1, SGU_WIDTH)),
            const((SGU_GROUPS, SGU_CHUNK, SGU_CHUNK)),
            const((SGU_CHUNK, SGU_WIDTH)),
            const((3, 3 * HY_WIDTH)),
            const((1, 3 * HY_WIDTH)),
            pl.BlockSpec((None, None, k_all.shape[2], XA_WIDTH), lambda b, j: (l, b, 0, 0)),
            pl.BlockSpec((None, None, k_all.shape[2], XA_WIDTH), lambda b, j: (l, b, 0, 0)),
            pl.BlockSpec((None, None, SGU_WIDTH, D), lambda b, j: (l, 0, 0, 0), pipeline_mode=pl.Buffered(1)),
            pl.BlockSpec((None, None, XA_WIDTH, D), lambda b, j: (l, 2, 0, 0), pipeline_mode=pl.Buffered(1)),
        ] + [pl.BlockSpec((None, slab, e_cols), lambda b, j: (l, step(b, j), 0))] * len(e_flat),
        out_specs=[
            pl.BlockSpec((None, tm, D), lambda b, j: (b, j, 0)),
            pl.BlockSpec((None, tm, D), lambda b, j: (b, j, 0)),
            pl.BlockSpec((None, nct, tm, HY_CT), lambda b, j: (b, 0, j, 0)),
            pl.BlockSpec((None, nct, tm, HY_CT), lambda b, j: (b, 0, j, 0)),
        ] + [pl.BlockSpec((slab, e_cols), lambda b, j: (step(b, j), 0))] * len(e_flat),
        out_shape=[
            jax.ShapeDtypeStruct((B, L, D), BF16),
            jax.ShapeDtypeStruct((B, L, D), BF16),
            jax.ShapeDtypeStruct((B, nct, L, HY_CT), F32),
            jax.ShapeDtypeStruct((B, nct, L, HY_CT), F32),
        ] + [jax.ShapeDtypeStruct((e_rows, e_cols), BF16)] * len(e_flat),
        compiler_params=_cparams(("arbitrary", "arbitrary")),
        name="mixer_a",
    )(*x_args, norm1_g, w_in_bf, gate_b, sgu_ln_g, sgu_ln_b, sgu_w_bf, sgu_bt, conv_w, conv_b,
      k_all, v_all, w_branch_bf, w_branch_bf, *e_flat)


def _fft_constants(seq_len):
    n = 2 * seq_len
    n2 = FFT_N2
    n1 = n // n2
    nz = seq_len // n2
    lo = FFT_ROWS // nz
    tiles = n2 // lo
    k1n = n1 // 2 + 1
    assert lo == SUBLANES and nz * lo == FFT_ROWS
    k1 = np.arange(k1n).reshape(k1n, 1, 1, 1, 1).astype(np.float64)
    part = np.arange(2).reshape(1, 2, 1, 1, 1)
    a = np.arange(lo).reshape(1, 1, lo, 1, 1)
    m1 = np.arange(nz).reshape(1, 1, 1, nz, 1).astype(np.float64)
    a2 = np.arange(lo).reshape(1, 1, 1, 1, lo)
    theta = 2.0 * np.pi * m1 * k1 / n1
    fwd = (np.where(part == 0, np.cos(theta), -np.sin(theta)) * (a == a2)).reshape(k1n * 2 * lo, nz * lo)
    weight = np.where((np.arange(k1n) == 0) | (np.arange(k1n) == n1 // 2), 1.0, 2.0) / n
    inv = (fwd.reshape(k1n, 2 * lo * nz * lo) * weight.reshape(k1n, 1)).reshape(k1n * 2 * lo, nz * lo).T
    n2_idx = (lo * np.arange(tiles).reshape(tiles, 1, 1) + np.arange(lo).reshape(1, lo, 1)).astype(np.float64)
    ang = 2.0 * np.pi * n2_idx * np.arange(k1n).reshape(1, 1, k1n) / n
    tw_cos = np.zeros((tiles, lo, LANES), np.float64)
    tw_sin = np.zeros((tiles, lo, LANES), np.float64)
    tw_cos[:, :, :k1n] = np.cos(ang)
    tw_sin[:, :, :k1n] = np.sin(ang)
    phi = 2.0 * np.pi * np.outer(np.arange(n2), np.arange(n2)) / n2
    c, s = np.cos(phi), np.sin(phi)
    g_fwd = np.block([[c, s], [-s, c]])
    g_inv = np.block([[c, -s], [s, c]])
    f32 = lambda v: jnp.asarray(v.astype(np.float32))
    bf = lambda v: f32(v).astype(BF16)
    return bf(fwd), bf(inv), bf(g_fwd), bf(g_inv), f32(tw_cos), f32(tw_sin), (tiles, k1n, lo, nz)


def _hyena_body(filt_ref, u_ref, x0_ref, bias_ref, fa_ref, ia_ref, gf_ref, gi_ref, twc_ref, tws_ref,
                o_ref, z_ref, k_ref, *, tiles, k1n, lo, nz):
    s = pl.program_id(1)
    half = FFT_N2
    rows = nz * lo
    ct = HY_CT

    def twiddles(j):
        c, sn = twc_ref[j], tws_ref[j]
        bc = lambda t: jnp.stack([jnp.broadcast_to(t[:, k1:k1 + 1], (lo, ct)) for k1 in range(k1n)])
        return bc(c), bc(sn)

    def stage_a(src_ref):
        for j in range(tiles):
            tile = src_ref[:, j].reshape(rows, ct).astype(BF16)
            zb = _dot(fa_ref[...], tile).reshape(k1n, 2, lo, ct)
            zr, zi = zb[:, 0], zb[:, 1]
            c, sn = twiddles(j)
            z_ref[:, 0, j] = zr * c + zi * sn
            z_ref[:, 1, j] = zi * c - zr * sn

    def spectrum(k1):
        return _dot(gf_ref[...], z_ref[k1].reshape(2 * half, ct).astype(BF16))

    @pl.when(s == 0)
    def _():
        k_ref[...] = jnp.zeros(k_ref.shape, F32)

    @pl.when(s < 2)
    def _():
        stage_a(filt_ref)
        im_sign = jnp.where(s == 0, 1.0, -1.0).astype(F32)
        for k1 in range(k1n):
            x = spectrum(k1)
            k_ref[k1] += jnp.concatenate([x[:half], im_sign * x[half:]], axis=0)

    @pl.when(s >= 2)
    def _():
        stage_a(u_ref)
        for k1 in range(k1n):
            x = spectrum(k1)
            xr, xi = x[:half], x[half:]
            kr, ki = k_ref[k1, :half], k_ref[k1, half:]
            y = jnp.concatenate([xr * kr - xi * ki, xr * ki + xi * kr], axis=0).astype(BF16)
            z_ref[k1] = _dot(gi_ref[...], y).reshape(2, tiles, lo, ct)
        bias = bias_ref[...]
        for j in range(tiles):
            vr, vi = z_ref[:, 0, j], z_ref[:, 1, j]
            c, sn = twiddles(j)
            wr = vr * c - vi * sn
            wi = vi * c + vr * sn
            zt = jnp.stack([wr, wi], axis=1).reshape(k1n * 2 * lo, ct).astype(BF16)
            y = _dot(ia_ref[...], zt)
            uu = u_ref[:, j].reshape(rows, ct)
            x0 = x0_ref[:, j].reshape(rows, ct)
            o_ref[:, j] = (x0 * (y + bias * uu)).reshape(nz, lo, ct)


def _hyena_call(l, filt, u, x0, hy_bias, consts):
    fa, ia, gf, gi, twc, tws, (tiles, k1n, lo, nz) = consts
    B, nct, L, ct = u.shape
    view = lambda a: a.reshape(a.shape[:-2] + (nz, tiles, lo, ct))
    seq = (nz, tiles, lo, ct)
    bidx = lambda s: jnp.maximum(s - 2, 0)
    single = lambda shape: pl.BlockSpec(shape, lambda c, s: (0,) * len(shape), pipeline_mode=pl.Buffered(1))
    out = pl.pallas_call(
        functools.partial(_hyena_body, tiles=tiles, k1n=k1n, lo=lo, nz=nz),
        grid=(nct, B + 2),
        in_specs=[
            pl.BlockSpec((None, None, None) + seq, lambda c, s: (l, jnp.minimum(s, 1), c, 0, 0, 0, 0),
                         pipeline_mode=pl.Buffered(1)),
            pl.BlockSpec((None, None) + seq, lambda c, s: (bidx(s), c, 0, 0, 0, 0)),
            pl.BlockSpec((None, None) + seq, lambda c, s: (bidx(s), c, 0, 0, 0, 0)),
            pl.BlockSpec((None, None, 1, ct), lambda c, s: (l, c, 0, 0)),
            single(fa.shape), single(ia.shape), single(gf.shape), single(gi.shape),
            single(twc.shape), single(tws.shape),
        ],
        out_specs=pl.BlockSpec((None, None) + seq, lambda c, s: (bidx(s), c, 0, 0, 0, 0)),
        out_shape=jax.ShapeDtypeStruct((B, nct) + seq, F32),
        scratch_shapes=[
            pltpu.VMEM((k1n, 2, tiles, lo, ct), F32),
            pltpu.VMEM((k1n, 2 * FFT_N2, ct), F32),
        ],
        compiler_params=_cparams(("arbitrary", "arbitrary")),
        name="hyena_fft",
    )(view(filt), view(u), view(x0), hy_bias, fa, ia, gf, gi, twc, tws)
    return out.reshape(B, nct, L, HY_CT)


def _pack_bf16_pairs(x):
    n = x.shape[1] // 2
    return pltpu.pack_elementwise([x[:, :n], x[:, n:]], packed_dtype=BF16)


def _unpack_bf16_pairs(p):
    lo = pltpu.unpack_elementwise(p, index=0, packed_dtype=BF16, unpacked_dtype=F32)
    hi = pltpu.unpack_elementwise(p, index=1, packed_dtype=BF16, unpacked_dtype=F32)
    return jnp.concatenate([lo, hi], axis=1).astype(BF16)


def _mixer_c_body(*refs, n_x):
    x_refs, refs = refs[:n_x], refs[n_x:]
    (mp_ref, gh_ref, hy_ref, wbh_ref, wo_ref, g2_ref, wr_ref, x1_ref, hp_ref, afft_ref) = refs
    hy = jnp.concatenate([hy_ref[c] for c in range(HY_WIDTH // HY_CT)], axis=1).astype(BF16)
    merged = mp_ref[...].astype(F32) + gh_ref[...].astype(F32) * _dot(hy, wbh_ref[...])
    x1 = sum(r[...] for r in x_refs) + _dot(merged.astype(BF16), wo_ref[...])
    x1_ref[...] = x1
    h2 = _rms(x1, g2_ref[...])
    hp_ref[...] = _pack_bf16_pairs(h2)
    tm = h2.shape[0]
    hh, hl = _split_bf16(h2)
    wh, wl = _split_bf16(wr_ref[...])
    prod = _dot(jnp.concatenate([hh, hl], axis=0), jnp.concatenate([wh, wl], axis=1))
    logits = prod[:tm, :LANES] + prod[:tm, LANES:] + prod[tm:, :LANES]
    lane = lax.broadcasted_iota(jnp.int32, logits.shape, 1)
    logits = jnp.where(lane < N_EXPERTS, logits, -jnp.inf)
    e = jnp.exp(logits - jnp.max(logits, axis=-1, keepdims=True))
    aff = e / jnp.sum(e, axis=-1, keepdims=True)
    afft_ref[...] = aff.T[:N_EXPERTS, :]


def _mixer_c_call(l, xs, mp, gh, hy, w_branch_bf, w_out_bf, norm2_g, wr_pad):
    B, L, D = xs[0].shape
    tm = 1024
    nct = HY_WIDTH // HY_CT
    const = lambda shape: pl.BlockSpec((None,) + shape, lambda b, j: (l,) + (0,) * len(shape),
                                       pipeline_mode=pl.Buffered(1))
    return pl.pallas_call(
        functools.partial(_mixer_c_body, n_x=len(xs)),
        grid=(B, L // tm),
        in_specs=[pl.BlockSpec((None, tm, D), lambda b, j: (b, j, 0))] * len(xs) + [
            pl.BlockSpec((None, tm, D), lambda b, j: (b, j, 0)),
            pl.BlockSpec((None, tm, D), lambda b, j: (b, j, 0)),
            pl.BlockSpec((None, nct, tm, HY_CT), lambda b, j: (b, 0, j, 0)),
            pl.BlockSpec((None, None, HY_WIDTH, D), lambda b, j: (l, 1, 0, 0), pipeline_mode=pl.Buffered(1)),
            const((D, D)),
            const((1, D)),
            const((D, LANES)),
        ],
        out_specs=[
            pl.BlockSpec((None, tm, D), lambda b, j: (b, j, 0)),
            pl.BlockSpec((None, tm, D // 2), lambda b, j: (b, j, 0)),
            pl.BlockSpec((None, N_EXPERTS, tm), lambda b, j: (b, 0, j)),
        ],
        out_shape=[
            jax.ShapeDtypeStruct((B, L, D), F32),
            jax.ShapeDtypeStruct((B, L, D // 2), jnp.uint32),
            jax.ShapeDtypeStruct((B, N_EXPERTS, L), F32),
        ],
        compiler_params=_cparams(("arbitrary", "arbitrary")),
        name="mixer_c",
    )(*xs, mp, gh, hy, w_branch_bf, w_out_bf, norm2_g, wr_pad)


def _cumsum_lanes(x, tri):
    outs = []
    carry = jnp.zeros((x.shape[0], 1), F32)
    for blk in range(x.shape[1] // LANES):
        loc = _dot(x[:, blk * LANES:(blk + 1) * LANES].astype(BF16), tri) + carry
        outs.append(loc)
        carry = loc[:, LANES - 1:LANES]
    return jnp.concatenate(outs, axis=1)


def _route_body(afft_ref, tri_ref, idx_ref, gate_ref, sel_ref, thr_ref, *, cap):
    b = pl.program_id(0)
    nb, ne, seq = afft_ref.shape

    @pl.when(b == 0)
    def _():
        a_all = afft_ref[...].reshape(nb * ne, seq)

        def bit_step(i, v):
            cand = v | jnp.left_shift(jnp.int32(1), 30 - i)
            cnt = jnp.sum((a_all >= lax.bitcast_convert_type(cand, F32)).astype(F32), axis=1, keepdims=True)
            return jnp.where(cnt >= cap, cand, v)

        bits = lax.fori_loop(0, 31, bit_step, jnp.zeros((nb * ne, 1), jnp.int32))
        thr_ref[...] = jnp.broadcast_to(lax.bitcast_convert_type(bits, F32), thr_ref.shape)

    a = afft_ref[b]
    thr = thr_ref[pl.ds(pl.multiple_of(b * ne, ne), ne), 0:1]
    gt = a > thr
    eq = a == thr
    need = cap - jnp.sum(gt.astype(F32), axis=1, keepdims=True)
    tri = tri_ref[...]
    cum_eq = _cumsum_lanes(eq.astype(F32), tri)
    sel = gt | (eq & (cum_eq <= need))
    sel_ref[...] = sel.astype(F32)

    nblk = seq // LANES
    f32c = lambda v: jnp.where(v, 1.0, 0.0)
    lane = lax.broadcasted_iota(jnp.int32, (cap, LANES), 1)
    slot = lax.broadcasted_iota(jnp.int32, (cap, LANES), 0).astype(F32)
    ones_row = jnp.ones((SUBLANES, LANES), BF16)
    bi = lax.broadcasted_iota(jnp.int32, (nblk, LANES), 0)
    bj = lax.broadcasted_iota(jnp.int32, (nblk, LANES), 1)
    upper = f32c((bi <= bj) | (bj >= nblk)).astype(BF16)
    lower = f32c(bj < bi)[:, :nblk].astype(BF16)
    half = float(cap // 2)
    experts = range(ne)
    selb = [jnp.concatenate([sel_ref[e:e + 1, j * LANES:(j + 1) * LANES] for j in range(nblk)], axis=0).astype(BF16)
            for e in experts]
    loc = [_dot(selb[e], tri) for e in experts]
    tot_row = [_dot_nt(ones_row, selb[e])[:, :nblk].astype(BF16) for e in experts]
    incl_row = [_dot(tot_row[e], upper)[0:1, :] for e in experts]
    excl = [_dot(lower, jnp.broadcast_to(loc[e][:, LANES - 1:LANES], (nblk, LANES)).astype(BF16)) for e in experts]
    jb = [jnp.sum(f32c(incl_row[e] <= slot), axis=1, keepdims=True) for e in experts]
    onehot = [f32c(jb[e].astype(jnp.int32) == lane).astype(BF16) for e in experts]
    def affinity_terms(e):
        blk = jnp.concatenate([afft_ref[b, e:e + 1, j * LANES:(j + 1) * LANES] for j in range(nblk)], axis=0)
        t1 = blk.astype(BF16)
        t2 = (blk - t1.astype(F32)).astype(BF16)
        t3 = (blk - t1.astype(F32) - t2.astype(F32)).astype(BF16)
        return [t1, t2, t3]

    ncol = 6 * LANES
    pad = jnp.zeros((LANES - nblk, ncol), BF16)
    table = [jnp.concatenate([jnp.concatenate([loc[e].astype(BF16), jnp.minimum(excl[e], half).astype(BF16),
                                               jnp.maximum(excl[e] - half, 0.0).astype(BF16)]
                                              + affinity_terms(e), axis=1), pad], axis=0)
             for e in experts]
    got = [_dot(onehot[e], table[e]) for e in experts]
    idx_t = jnp.zeros((cap, LANES), F32)
    gate_t = jnp.zeros((cap, LANES), F32)
    for e in experts:
        g = got[e]
        local = slot - (g[:, LANES:2 * LANES] + g[:, 2 * LANES:3 * LANES])
        pos = jnp.sum(f32c(g[:, :LANES] <= local), axis=1, keepdims=True)
        aff_row = g[:, 3 * LANES:4 * LANES] + g[:, 4 * LANES:5 * LANES] + g[:, 5 * LANES:]
        gate = jnp.sum(jnp.where(lane == pos.astype(jnp.int32), aff_row, 0.0), axis=1, keepdims=True)
        idx_t = idx_t + jnp.where(lane == e, jb[e] * float(LANES) + pos, 0.0)
        gate_t = gate_t + jnp.where(lane == e, gate, 0.0)
    idx_ref[...] = idx_t.T[:ne, :].astype(jnp.int32)
    gate_ref[...] = gate_t


def _route_call(afft, cap):
    B, ne, L = afft.shape
    tri = jnp.asarray(np.triu(np.ones((LANES, LANES), np.float32))).astype(BF16)
    return pl.pallas_call(
        functools.partial(_route_body, cap=cap),
        grid=(B,),
        in_specs=[
            pl.BlockSpec((B, ne, L), lambda b: (0, 0, 0)),
            pl.BlockSpec((LANES, LANES), lambda b: (0, 0)),
        ],
        out_specs=[pl.BlockSpec((None, ne, cap), lambda b: (b, 0, 0)),
                   pl.BlockSpec((None, cap, LANES), lambda b: (b, 0, 0))],
        out_shape=[jax.ShapeDtypeStruct((B, ne, cap), jnp.int32),
                   jax.ShapeDtypeStruct((B, cap, LANES), F32)],
        scratch_shapes=[pltpu.VMEM((ne, L), F32), pltpu.VMEM((B * ne, LANES), F32)],
        compiler_params=_cparams(("arbitrary",)),
        name="route",
    )(afft, tri)


def _moe_body(idx_ref, idxp_ref, hp_ref, gates_ref, w1_ref, w3_ref, w2_ref, out_hbm,
              acc_ref, xin_ref, y_ref, sem_ref, *, cap, ne, npairs):
    i = pl.program_id(0)
    slot = i % 2
    e = jnp.minimum(i, npairs - 1) % ne

    @pl.when(i == 0)
    def _():
        acc_ref[...] = jnp.zeros(acc_ref.shape, F32)
        y_ref[...] = jnp.zeros(y_ref.shape, F32)

    y_prev = y_ref.at[1 - slot]
    for g0 in range(0, cap, SUBLANES):
        y_tile = y_prev[pl.ds(g0, SUBLANES), :]
        for k0 in range(0, SUBLANES, MOE_RMW_ROWS):
            ks = range(k0, k0 + MOE_RMW_ROWS)
            toks = [idxp_ref[0, 0, g0 + k] for k in ks]
            vals = [acc_ref[pl.ds(t, 1), :] + y_tile[k:k + 1, :] for t, k in zip(toks, ks)]
            for t, v in zip(toks, vals):
                acc_ref[pl.ds(t, 1), :] = v

    for s in range(cap):
        xin_ref[pl.ds(s, 1), :] = hp_ref[pl.ds(idx_ref[0, 0, s], 1), :]

    gates = gates_ref[...]
    lane = lax.broadcasted_iota(jnp.int32, gates.shape, 1)
    gate = jnp.sum(jnp.where(lane == e, gates, 0.0), axis=1, keepdims=True)
    xb = _unpack_bf16_pairs(xin_ref[...])
    a1 = _dot(xb, w1_ref[...])
    a3 = _dot(xb, w3_ref[...])
    hid = (a1 * jax.nn.sigmoid(a1) * a3).astype(BF16)
    y_ref[slot] = _dot(hid, w2_ref[...]) * gate

    @pl.when((i > 0) & (i % ne == 0))
    def _():
        cp = pltpu.make_async_copy(acc_ref, out_hbm.at[i // ne - 1], sem_ref.at[0])
        cp.start()
        cp.wait()
        acc_ref[...] = jnp.zeros(acc_ref.shape, F32)


def _moe_call(idx, gates, hp, w1_rows, w3_rows, w2_rows):
    B, ne, cap = idx.shape
    L = hp.shape[1]
    ff = w1_rows.shape[-1]
    w1_bf, w3_bf = (w.reshape(ne, D_MODEL, ff) for w in (w1_rows, w3_rows))
    w2_bf = w2_rows.reshape(ne, ff, D_MODEL)
    npairs = B * ne
    idx3 = idx.reshape(npairs, 1, cap)
    pair = lambda i: jnp.minimum(i, npairs - 1)
    prev = lambda i: jnp.maximum(i - 1, 0)
    return pl.pallas_call(
        functools.partial(_moe_body, cap=cap, ne=ne, npairs=npairs),
        grid=(npairs + 1,),
        in_specs=[
            pl.BlockSpec((1, 1, cap), lambda i: (pair(i), 0, 0), memory_space=pltpu.SMEM),
            pl.BlockSpec((1, 1, cap), lambda i: (prev(i), 0, 0), memory_space=pltpu.SMEM),
            pl.BlockSpec((None, L, D_MODEL // 2), lambda i: (pair(i) // ne, 0, 0)),
            pl.BlockSpec((None, cap, LANES), lambda i: (pair(i) // ne, 0, 0)),
            pl.BlockSpec((None, D_MODEL, ff), lambda i: (pair(i) % ne, 0, 0)),
            pl.BlockSpec((None, D_MODEL, ff), lambda i: (pair(i) % ne, 0, 0)),
            pl.BlockSpec((None, ff, D_MODEL), lambda i: (pair(i) % ne, 0, 0)),
        ],
        out_specs=pl.BlockSpec(memory_space=pl.ANY),
        out_shape=jax.ShapeDtypeStruct((B, L, D_MODEL), F32),
        scratch_shapes=[
            pltpu.VMEM((L, D_MODEL), F32),
            pltpu.VMEM((cap, D_MODEL // 2), jnp.uint32),
            pltpu.VMEM((2, cap, D_MODEL), F32),
            pltpu.SemaphoreType.DMA((1,)),
        ],
        compiler_params=_cparams(("arbitrary",)),
        name="moe",
    )(idx3, idx3, hp, gates, w1_bf, w3_bf, w2_bf)


def _final_body(*refs):
    g_ref, o_ref = refs[-2:]
    o_ref[...] = _rms(sum(r[...] for r in refs[:-2]), g_ref[...])


def _final_call(xs, g):
    B, L, D = xs[0].shape
    tm = 1024
    return pl.pallas_call(
        _final_body,
        grid=(B, L // tm),
        in_specs=[pl.BlockSpec((None, tm, D), lambda b, j: (b, j, 0))] * len(xs)
        + [pl.BlockSpec((1, D), lambda b, j: (0, 0))],
        out_specs=pl.BlockSpec((None, tm, D), lambda b, j: (b, j, 0)),
        out_shape=jax.ShapeDtypeStruct((B, L, D), F32),
        compiler_params=_cparams(("arbitrary", "arbitrary")),
        name="final_norm",
    )(*xs, g.reshape(1, D))


def kernel(x, mem, norm1_g, w_in, gate_b, sgu_ln_g, sgu_ln_b, sgu_w, sgu_b, hy_conv_w, hy_conv_b, hy_f_w1, hy_f_b1, hy_f_w2, hy_f_b2, hy_f_w3, hy_f_b3, hy_f_w4, hy_f_freq, hy_bias, mem_norm_g, w_kv, w_branch, w_out, norm2_g, w_router, w1, w3, w2, final_g):
    B, L, D = x.shape
    depth = w_in.shape[0]
    cap = EC_CAPACITY * L // N_EXPERTS
    nct = HY_WIDTH // HY_CT
    row = lambda a: a.reshape(depth, 1, a.shape[-1])

    w_kv_bf = w_kv.astype(BF16)
    w_branch_bf = w_branch.astype(BF16)
    w_out_bf = w_out.astype(BF16)
    sgu_w_bf = sgu_w.astype(BF16)
    sgu_bt = jnp.repeat(jnp.swapaxes(sgu_b, 1, 2), SGU_WIDTH // SGU_GROUPS, axis=2)
    wr_pad = jnp.pad(w_router, ((0, 0), (0, 0), (0, LANES - N_EXPERTS)))
    hy_bias4 = hy_bias.reshape(depth, nct, 1, HY_CT)

    k_all, v_all = _kv_call(mem, mem_norm_g, w_kv_bf)
    filt, w_in_bf = _filter_call(L, hy_f_w1, hy_f_b1, hy_f_w2, hy_f_b2, hy_f_w3, hy_f_b3, hy_f_w4, hy_f_freq, w_in)
    consts = _fft_constants(L)

    xs = [x]
    for l in range(depth):
        mp, gh, x0, u, w1_bf, w3_bf, w2_bf = _mixer_a_call(
            l, xs, row(norm1_g), w_in_bf, row(gate_b), row(sgu_ln_g), row(sgu_ln_b), sgu_w_bf, sgu_bt,
            hy_conv_w, row(hy_conv_b), k_all, v_all, w_branch_bf, (w1, w3, w2))
        hy = _hyena_call(l, filt, u, x0, hy_bias4, consts)
        x1, hp, afft = _mixer_c_call(l, xs, mp, gh, hy, w_branch_bf, w_out_bf, row(norm2_g), wr_pad)
        idx, gates = _route_call(afft, cap)
        xs = [x1, _moe_call(idx, gates, hp, w1_bf, w3_bf, w2_bf)]
    return _final_call(xs, final_g)
```

```python
import functools
import math

import numpy as np
import jax
import jax.numpy as jnp
from jax import lax
from jax.experimental import pallas as pl
from jax.experimental.pallas import tpu as pltpu

F32 = jnp.float32
BF16 = jnp.bfloat16

D_MODEL = 1024
SGU_WIDTH = 512
SGU_GROUPS = 4
SGU_CHUNK = 128
HY_WIDTH = 512
HY_EMB_DIM = 33
HY_BANDS = (HY_EMB_DIM - 1) // 2
HY_FILTER_HIDDEN = 64
HY_FAST_DECAY_PCT = 0.3
HY_SLOW_DECAY_PCT = 1.5
HY_DECAY_TARGET = 1e-2
XA_HEADS = 4
XA_HEAD_DIM = 128
XA_WIDTH = XA_HEADS * XA_HEAD_DIM
N_EXPERTS = 16
EC_CAPACITY = 2
NORM_EPS = 1e-6
LN_EPS = 1e-5

C_SGU = 2 * SGU_WIDTH
C_HY = C_SGU + 3 * HY_WIDTH
C_Q = C_HY + XA_WIDTH

LANES = 128
SUBLANES = 8
VMEM_LIMIT = 56 * 1024 * 1024

FFT_N2 = 128
FFT_ROWS = 256
HY_CT = 256

MOE_RMW_ROWS = 8


def _cparams(semantics, flags=None):
    return pltpu.CompilerParams(dimension_semantics=semantics, vmem_limit_bytes=VMEM_LIMIT, flags=flags)


def _rms(x, g):
    return x * lax.rsqrt(jnp.mean(x * x, axis=-1, keepdims=True) + NORM_EPS) * g


def _dot(a, b):
    return jnp.dot(a, b, preferred_element_type=F32)


def _dot_nt(a, b):
    return lax.dot_general(a, b, (((1,), (1,)), ((), ())), preferred_element_type=F32)


def _split_bf16(x):
    hi = x.astype(BF16)
    lo = (x - hi.astype(F32)).astype(BF16)
    return hi, lo


def _kv_body(mem_ref, g_ref, w_ref, k_ref, v_ref):
    m = _rms(mem_ref[...], g_ref[...]).astype(BF16)
    kv = _dot(m, w_ref[...])
    k_ref[...] = kv[:, :XA_WIDTH].astype(BF16)
    v_ref[...] = kv[:, XA_WIDTH:].astype(BF16)


def _kv_call(mem, mem_norm_g, w_kv_bf):
    B, M, D = mem.shape
    depth = w_kv_bf.shape[0]
    return pl.pallas_call(
        _kv_body,
        grid=(depth, B),
        in_specs=[
            pl.BlockSpec((None, M, D), lambda l, b: (b, 0, 0)),
            pl.BlockSpec((None, 1, D), lambda l, b: (l, 0, 0)),
            pl.BlockSpec((None, D, 2 * XA_WIDTH), lambda l, b: (l, 0, 0)),
        ],
        out_specs=[
            pl.BlockSpec((None, None, M, XA_WIDTH), lambda l, b: (l, b, 0, 0)),
            pl.BlockSpec((None, None, M, XA_WIDTH), lambda l, b: (l, b, 0, 0)),
        ],
        out_shape=[jax.ShapeDtypeStruct((depth, B, M, XA_WIDTH), BF16)] * 2,
        compiler_params=_cparams(("arbitrary", "arbitrary")),
        name="kv",
    )(mem, mem_norm_g.reshape(depth, 1, D), w_kv_bf)


FILT_GROUPS = LANES // HY_BANDS


def _filter_body(w1t_ref, w1c_ref, w1s_ref, b1_ref, w2_ref, b2_ref, w3_ref, b3_ref, w4_ref, fr_ref,
                 bands_ref, deltas_ref, win_ref, o_ref, winb_ref, *, seq_len, rows):
    winb_ref[...] = win_ref[...].astype(BF16)
    r0 = pl.program_id(1) * rows
    ng, hid = FILT_GROUPS, HY_FILTER_HIDDEN
    rg = rows // ng

    def lags(width, per_group):
        return (r0 + lax.broadcasted_iota(jnp.int32, (rg, width), 0)
                + (lax.broadcasted_iota(jnp.int32, (rg, width), 1) // per_group) * rg)

    pos_b = lags(LANES, HY_BANDS).astype(F32)
    lag = lags(ng * hid, hid)
    t = lag.astype(F32) * (1.0 / (seq_len - 1))
    ang = pos_b * (2.0 * math.pi / seq_len) * bands_ref[...]
    hp = lax.Precision.HIGHEST
    dot = lambda a, w_ref: jnp.dot(a, w_ref[...], precision=hp, preferred_element_type=F32)
    fr = fr_ref[...]
    h = jnp.sin(fr * (t * w1t_ref[...] + dot(jnp.cos(ang), w1c_ref) - dot(jnp.sin(ang), w1s_ref) + b1_ref[...]))
    h = jnp.sin(fr * (dot(h, w2_ref) + b2_ref[...]))
    h = jnp.sin(fr * (dot(h, w3_ref) + b3_ref[...]))
    for g in range(ng):
        og = dot(h[:, g * hid:(g + 1) * hid], w4_ref)
        decay = jnp.exp(-t[:, g * hid:g * hid + 1] * deltas_ref[...])
        fwd = og[:, :HY_WIDTH] * decay
        bwd = jnp.where(lag[:, g * hid:g * hid + 1] == 0, 0.0, og[:, HY_WIDTH:] * decay)
        for c in range(HY_WIDTH // HY_CT):
            o_ref[0, c, g * rg:(g + 1) * rg, :] = fwd[:, c * HY_CT:(c + 1) * HY_CT]
            o_ref[1, c, g * rg:(g + 1) * rg, :] = bwd[:, c * HY_CT:(c + 1) * HY_CT]


def _filter_call(seq_len, w1, b1, w2, b2, w3, b3, w4, freq, w_in):
    depth = w1.shape[0]
    H = HY_FILTER_HIDDEN
    ng = FILT_GROUPS
    rows = 1024
    nsteps = seq_len // rows
    win_rows, win_cols = w_in.shape[1], w_in.shape[2]
    slab = win_rows // nsteps
    assert slab * nsteps == win_rows and slab % (2 * SUBLANES) == 0
    nct = HY_WIDTH // HY_CT
    bands = np.tile(np.linspace(1e-4, HY_BANDS - 1, HY_BANDS), ng).astype(np.float32).reshape(1, LANES)
    max_decay = math.log(HY_DECAY_TARGET) / HY_FAST_DECAY_PCT
    min_decay = math.log(HY_DECAY_TARGET) / HY_SLOW_DECAY_PCT
    deltas = np.abs(np.linspace(min_decay, max_decay, HY_WIDTH)).astype(np.float32).reshape(1, HY_WIDTH)
    eye = jnp.eye(ng, dtype=F32)
    bdiag = lambda w: jnp.einsum("gh,lij->lgihj", eye, w).reshape(depth, ng * w.shape[1], ng * w.shape[2])
    tiled = lambda v: jnp.tile(v.reshape(depth, 1, H), (1, 1, ng))
    w1t = tiled(w1[:, 0, :])
    w1c = bdiag(w1[:, 1:1 + HY_BANDS, :])
    w1s = bdiag(w1[:, 1 + HY_BANDS:, :])
    vec = lambda n: pl.BlockSpec((None, 1, n), lambda l, r: (l, 0, 0))
    mat = lambda m, n: pl.BlockSpec((None, m, n), lambda l, r: (l, 0, 0))
    gh = ng * H
    return pl.pallas_call(
        functools.partial(_filter_body, seq_len=seq_len, rows=rows),
        grid=(depth, seq_len // rows),
        in_specs=[
            vec(gh), mat(LANES, gh), mat(LANES, gh), vec(gh), mat(gh, gh), vec(gh), mat(gh, gh), vec(gh),
            mat(H, 2 * HY_WIDTH),
            vec(gh),
            pl.BlockSpec((1, LANES), lambda l, r: (0, 0)),
            pl.BlockSpec((1, HY_WIDTH), lambda l, r: (0, 0)),
            pl.BlockSpec((None, slab, win_cols), lambda l, r: (l, r, 0)),
        ],
        out_specs=[pl.BlockSpec((None, 2, nct, rows, HY_CT), lambda l, r: (l, 0, 0, r, 0)),
                   pl.BlockSpec((None, slab, win_cols), lambda l, r: (l, r, 0))],
        out_shape=[jax.ShapeDtypeStruct((depth, 2, nct, seq_len, HY_CT), F32),
                   jax.ShapeDtypeStruct(w_in.shape, BF16)],
        compiler_params=_cparams(("arbitrary", "arbitrary")),
        name="hyena_filter",
    )(w1t, w1c, w1s, tiled(b1), bdiag(w2), tiled(b2), bdiag(w3), tiled(b3), w4, tiled(freq),
      jnp.asarray(bands), jnp.asarray(deltas), w_in)


def _mixer_a_body(*refs, tm, n_x):
    x_refs, refs = refs[:3 * n_x], refs[3 * n_x:]
    (g1_ref, win_ref, gb_ref, lng_ref, lnb_ref, sw_ref, sbt_ref, cw_ref, cb_ref, k_ref, v_ref, wba_ref, wbc_ref,
     we1_ref, we3_ref, we2_ref, mp_ref, gh_ref, x0_ref, u_ref, wb1_ref, wb3_ref, wb2_ref, *xsum_ref) = refs
    x_cur = sum(r[...] for r in x_refs[0::3])
    if xsum_ref:
        xsum_ref[0][...] = x_cur
    x_prev = sum(r[...] for r in x_refs[1::3])
    x_next = sum(r[...] for r in x_refs[2::3])
    j = pl.program_id(1)
    nj = pl.num_programs(1)
    g1 = g1_ref[...]
    h = _rms(x_cur, g1)
    hb = h.astype(BF16)

    zs = _dot(hb, win_ref[:, 0:C_SGU])
    zs = 0.5 * zs * (1.0 + lax.erf(zs * (1.0 / math.sqrt(2.0))))
    us = zs[:, :SGU_WIDTH]
    vs = zs[:, SGU_WIDTH:]
    mu = jnp.mean(vs, axis=-1, keepdims=True)
    vc = vs - mu
    var = jnp.mean(vc * vc, axis=-1, keepdims=True)
    vn = (vc * lax.rsqrt(var + LN_EPS) * lng_ref[...] + lnb_ref[...]).astype(BF16)
    gd = SGU_WIDTH // SGU_GROUPS
    chunks = []
    for c in range(tm // SGU_CHUNK):
        groups = []
        for g in range(SGU_GROUPS):
            blk = vn[c * SGU_CHUNK:(c + 1) * SGU_CHUNK, g * gd:(g + 1) * gd]
            groups.append(_dot(sw_ref[g], blk))
        chunks.append(jnp.concatenate(groups, axis=1) + sbt_ref[...])
    a = us * jnp.concatenate(chunks, axis=0)

    q = _dot(hb, win_ref[:, C_HY:C_Q])
    scale = 1.0 / math.sqrt(XA_HEAD_DIM)
    heads = []
    for hd in range(XA_HEADS):
        sl = slice(hd * XA_HEAD_DIM, (hd + 1) * XA_HEAD_DIM)
        s = _dot_nt(q[:, sl].astype(BF16), k_ref[:, sl]) * scale
        s = s - jnp.max(s, axis=-1, keepdims=True)
        p = jnp.exp(s)
        p = p / jnp.sum(p, axis=-1, keepdims=True)
        heads.append(_dot(p.astype(BF16), v_ref[:, sl]))
    c_att = jnp.concatenate(heads, axis=1)

    gates = jax.nn.sigmoid(_dot(hb, win_ref[:, C_Q:]) + gb_ref[...])
    pa = _dot(a.astype(BF16), wba_ref[...])
    pc = _dot(c_att.astype(BF16), wbc_ref[...])
    mp_ref[...] = (gates[:, 0:D_MODEL] * pa + gates[:, 2 * D_MODEL:] * pc).astype(BF16)
    gh_ref[...] = gates[:, D_MODEL:2 * D_MODEL].astype(BF16)

    hprev = _rms(x_prev, g1).astype(BF16)
    hnext = _rms(x_next, g1).astype(BF16)
    hext = jnp.concatenate([hprev, hb, hnext], axis=0)
    pe = _dot(hext, win_ref[:, C_SGU:C_HY])
    rows = lax.broadcasted_iota(jnp.int32, (tm, 1), 0)
    pm1 = jnp.where((rows == 0) & (j == 0), 0.0, pe[SUBLANES - 1:SUBLANES - 1 + tm])
    pp1 = jnp.where((rows == tm - 1) & (j == nj - 1), 0.0, pe[SUBLANES + 1:SUBLANES + 1 + tm])
    p0 = pe[SUBLANES:SUBLANES + tm]
    pconv = cw_ref[0:1, :] * pm1 + cw_ref[1:2, :] * p0 + cw_ref[2:3, :] * pp1 + cb_ref[...]
    x0 = pconv[:, :HY_WIDTH]
    u = pconv[:, HY_WIDTH:2 * HY_WIDTH] * pconv[:, 2 * HY_WIDTH:]
    for c in range(HY_WIDTH // HY_CT):
        x0_ref[c] = x0[:, c * HY_CT:(c + 1) * HY_CT]
        u_ref[c] = u[:, c * HY_CT:(c + 1) * HY_CT]

    for src, dst in ((we1_ref, wb1_ref), (we3_ref, wb3_ref), (we2_ref, wb2_ref)):
        dst[...] = src[...].astype(BF16)


def _mixer_a_call(l, xs, norm1_g, w_in_bf, gate_b, sgu_ln_g, sgu_ln_b, sgu_w_bf, sgu_bt, conv_w, conv_b,
                  k_all, v_all, w_branch_bf, expert_ws):
    B, L, D = xs[0].shape
    tm = 512
    nct = HY_WIDTH // HY_CT
    nsteps = B * (L // tm)
    e_rows = expert_ws[0].shape[1] * expert_ws[0].shape[2]
    e_cols = expert_ws[0].shape[3]
    assert all(w.shape[1] * w.shape[2] == e_rows and w.shape[3] == e_cols for w in expert_ws)
    slab = e_rows // nsteps
    assert slab * nsteps == e_rows and slab % (2 * SUBLANES) == 0
    e_flat = [w.reshape(w.shape[0], e_rows, e_cols) for w in expert_ws]
    step = lambda b, j: b * (L // tm) + j
    in_cols = w_in_bf.shape[-1]
    hb_per_tile = tm // SUBLANES
    n_hb = L // SUBLANES
    const = lambda shape: pl.BlockSpec((None,) + shape, lambda b, j: (l,) + (0,) * len(shape),
                                       pipeline_mode=pl.Buffered(1))
    x_specs = [
        pl.BlockSpec((None, tm, D), lambda b, j: (b, j, 0)),
        pl.BlockSpec((None, SUBLANES, D), lambda b, j: (b, jnp.maximum(j * hb_per_tile - 1, 0), 0)),
        pl.BlockSpec((None, SUBLANES, D), lambda b, j: (b, jnp.minimum((j + 1) * hb_per_tile, n_hb - 1), 0)),
    ]
    x_args = [a for x in xs for a in (x, x, x)]
    return pl.pallas_call(
        functools.partial(_mixer_a_body, tm=tm, n_x=len(xs)),
        grid=(B, L // tm),
        in_specs=x_specs * len(xs) + [
            const((1, D)),
            const((D, in_cols)),
            const((1, 3 * D)),
            const((1, SGU_WIDTH)),
            const((1, SGU_WIDTH)),
            const((SGU_GROUPS, SGU_CHUNK, SGU_CHUNK)),
            const((SGU_CHUNK, SGU_WIDTH)),
            const((3, 3 * HY_WIDTH)),
            const((1, 3 * HY_WIDTH)),
            pl.BlockSpec((None, None, k_all.shape[2], XA_WIDTH), lambda b, j: (l, b, 0, 0)),
            pl.BlockSpec((None, None, k_all.shape[2], XA_WIDTH), lambda b, j: (l, b, 0, 0)),
            pl.BlockSpec((None, None, SGU_WIDTH, D), lambda b, j: (l, 0, 0, 0), pipeline_mode=pl.Buffered(1)),
            pl.BlockSpec((None, None, XA_WIDTH, D), lambda b, j: (l, 2, 0, 0), pipeline_mode=pl.Buffered(1)),
        ] + [pl.BlockSpec((None, slab, e_cols), lambda b, j: (l, step(b, j), 0))] * len(e_flat),
        out_specs=[
            pl.BlockSpec((None, tm, D), lambda b, j: (b, j, 0)),
            pl.BlockSpec((None, tm, D), lambda b, j: (b, j, 0)),
            pl.BlockSpec((None, nct, tm, HY_CT), lambda b, j: (b, 0, j, 0)),
            pl.BlockSpec((None, nct, tm, HY_CT), lambda b, j: (b, 0, j, 0)),
        ] + [pl.BlockSpec((slab, e_cols), lambda b, j: (step(b, j), 0))] * len(e_flat)
        + [pl.BlockSpec((None, tm, D), lambda b, j: (b, j, 0))] * (len(xs) > 1),
        out_shape=[
            jax.ShapeDtypeStruct((B, L, D), BF16),
            jax.ShapeDtypeStruct((B, L, D), BF16),
            jax.ShapeDtypeStruct((B, nct, L, HY_CT), F32),
            jax.ShapeDtypeStruct((B, nct, L, HY_CT), F32),
        ] + [jax.ShapeDtypeStruct((e_rows, e_cols), BF16)] * len(e_flat)
        + [jax.ShapeDtypeStruct((B, L, D), F32)] * (len(xs) > 1),
        compiler_params=_cparams(("arbitrary", "arbitrary")),
        name="mixer_a",
    )(*x_args, norm1_g, w_in_bf, gate_b, sgu_ln_g, sgu_ln_b, sgu_w_bf, sgu_bt, conv_w, conv_b,
      k_all, v_all, w_branch_bf, w_branch_bf, *e_flat)


def _fft_constants(seq_len):
    n = 2 * seq_len
    n2 = FFT_N2
    n1 = n // n2
    nz = seq_len // n2
    lo = FFT_ROWS // nz
    tiles = n2 // lo
    k1n = n1 // 2 + 1
    assert lo == SUBLANES and nz * lo == FFT_ROWS
    k1 = np.arange(k1n).reshape(k1n, 1, 1, 1, 1).astype(np.float64)
    part = np.arange(2).reshape(1, 2, 1, 1, 1)
    a = np.arange(lo).reshape(1, 1, lo, 1, 1)
    m1 = np.arange(nz).reshape(1, 1, 1, nz, 1).astype(np.float64)
    a2 = np.arange(lo).reshape(1, 1, 1, 1, lo)
    theta = 2.0 * np.pi * m1 * k1 / n1
    fwd = (np.where(part == 0, np.cos(theta), -np.sin(theta)) * (a == a2)).reshape(k1n * 2 * lo, nz * lo)
    weight = np.where((np.arange(k1n) == 0) | (np.arange(k1n) == n1 // 2), 1.0, 2.0) / n
    inv = (fwd.reshape(k1n, 2 * lo * nz * lo) * weight.reshape(k1n, 1)).reshape(k1n * 2 * lo, nz * lo).T
    n2_idx = (lo * np.arange(tiles).reshape(tiles, 1, 1) + np.arange(lo).reshape(1, lo, 1)).astype(np.float64)
    ang = 2.0 * np.pi * n2_idx * np.arange(k1n).reshape(1, 1, k1n) / n
    tw_cos = np.zeros((tiles, lo, LANES), np.float64)
    tw_sin = np.zeros((tiles, lo, LANES), np.float64)
    tw_cos[:, :, :k1n] = np.cos(ang)
    tw_sin[:, :, :k1n] = np.sin(ang)
    phi = 2.0 * np.pi * np.outer(np.arange(n2), np.arange(n2)) / n2
    c, s = np.cos(phi), np.sin(phi)
    g_fwd = np.block([[c, s], [-s, c]])
    g_inv = np.block([[c, -s], [s, c]])
    f32 = lambda v: jnp.asarray(v.astype(np.float32))
    bf = lambda v: f32(v).astype(BF16)
    return bf(fwd), bf(inv), bf(g_fwd), bf(g_inv), f32(tw_cos), f32(tw_sin), (tiles, k1n, lo, nz)


def _hyena_body(filt_ref, u_ref, x0_ref, bias_ref, fa_ref, ia_ref, gf_ref, gi_ref, twc_ref, tws_ref,
                o_ref, z_ref, k_ref, *, tiles, k1n, lo, nz):
    s = pl.program_id(1)
    half = FFT_N2
    rows = nz * lo
    ct = HY_CT

    def twiddles(j):
        c, sn = twc_ref[j], tws_ref[j]
        bc = lambda t: jnp.stack([jnp.broadcast_to(t[:, k1:k1 + 1], (lo, ct)) for k1 in range(k1n)])
        return bc(c), bc(sn)

    def stage_a(src_ref):
        for j in range(tiles):
            tile = src_ref[:, j].reshape(rows, ct).astype(BF16)
            zb = _dot(fa_ref[...], tile).reshape(k1n, 2, lo, ct)
            zr, zi = zb[:, 0], zb[:, 1]
            c, sn = twiddles(j)
            z_ref[:, 0, j] = zr * c + zi * sn
            z_ref[:, 1, j] = zi * c - zr * sn

    def spectrum(k1):
        return _dot(gf_ref[...], z_ref[k1].reshape(2 * half, ct).astype(BF16))

    @pl.when(s == 0)
    def _():
        k_ref[...] = jnp.zeros(k_ref.shape, F32)

    @pl.when(s < 2)
    def _():
        stage_a(filt_ref)
        im_sign = jnp.where(s == 0, 1.0, -1.0).astype(F32)
        for k1 in range(k1n):
            x = spectrum(k1)
            k_ref[k1] += jnp.concatenate([x[:half], im_sign * x[half:]], axis=0)

    @pl.when(s >= 2)
    def _():
        stage_a(u_ref)
        for k1 in range(k1n):
            x = spectrum(k1)
            xr, xi = x[:half], x[half:]
            kr, ki = k_ref[k1, :half], k_ref[k1, half:]
            y = jnp.concatenate([xr * kr - xi * ki, xr * ki + xi * kr], axis=0).astype(BF16)
            z_ref[k1] = _dot(gi_ref[...], y).reshape(2, tiles, lo, ct)
        bias = bias_ref[...]
        for j in range(tiles):
            vr, vi = z_ref[:, 0, j], z_ref[:, 1, j]
            c, sn = twiddles(j)
            wr = vr * c - vi * sn
            wi = vi * c + vr * sn
            zt = jnp.stack([wr, wi], axis=1).reshape(k1n * 2 * lo, ct).astype(BF16)
            y = _dot(ia_ref[...], zt)
            uu = u_ref[:, j].reshape(rows, ct)
            x0 = x0_ref[:, j].reshape(rows, ct)
            o_ref[:, j] = (x0 * (y + bias * uu)).reshape(nz, lo, ct)


def _hyena_call(l, filt, u, x0, hy_bias, consts):
    fa, ia, gf, gi, twc, tws, (tiles, k1n, lo, nz) = consts
    B, nct, L, ct = u.shape
    view = lambda a: a.reshape(a.shape[:-2] + (nz, tiles, lo, ct))
    seq = (nz, tiles, lo, ct)
    bidx = lambda s: jnp.maximum(s - 2, 0)
    single = lambda shape: pl.BlockSpec(shape, lambda c, s: (0,) * len(shape), pipeline_mode=pl.Buffered(1))
    out = pl.pallas_call(
        functools.partial(_hyena_body, tiles=tiles, k1n=k1n, lo=lo, nz=nz),
        grid=(nct, B + 2),
        in_specs=[
            pl.BlockSpec((None, None, None) + seq, lambda c, s: (l, jnp.minimum(s, 1), c, 0, 0, 0, 0),
                         pipeline_mode=pl.Buffered(1)),
            pl.BlockSpec((None, None) + seq, lambda c, s: (bidx(s), c, 0, 0, 0, 0)),
            pl.BlockSpec((None, None) + seq, lambda c, s: (bidx(s), c, 0, 0, 0, 0)),
            pl.BlockSpec((None, None, 1, ct), lambda c, s: (l, c, 0, 0)),
            single(fa.shape), single(ia.shape), single(gf.shape), single(gi.shape),
            single(twc.shape), single(tws.shape),
        ],
        out_specs=pl.BlockSpec((None, None) + seq, lambda c, s: (bidx(s), c, 0, 0, 0, 0)),
        out_shape=jax.ShapeDtypeStruct((B, nct) + seq, F32),
        scratch_shapes=[
            pltpu.VMEM((k1n, 2, tiles, lo, ct), F32),
            pltpu.VMEM((k1n, 2 * FFT_N2, ct), F32),
        ],
        compiler_params=_cparams(("arbitrary", "arbitrary")),
        name="hyena_fft",
    )(view(filt), view(u), view(x0), hy_bias, fa, ia, gf, gi, twc, tws)
    return out.reshape(B, nct, L, HY_CT)


def _pack_bf16_pairs(x):
    n = x.shape[1] // 2
    return pltpu.pack_elementwise([x[:, :n], x[:, n:]], packed_dtype=BF16)


def _unpack_bf16_pairs(p):
    lo = pltpu.unpack_elementwise(p, index=0, packed_dtype=BF16, unpacked_dtype=F32)
    hi = pltpu.unpack_elementwise(p, index=1, packed_dtype=BF16, unpacked_dtype=F32)
    return jnp.concatenate([lo, hi], axis=1).astype(BF16)


def _mixer_c_body(*refs, n_x):
    x_refs, refs = refs[:n_x], refs[n_x:]
    (mp_ref, gh_ref, hy_ref, wbh_ref, wo_ref, g2_ref, wr_ref, x1_ref, hp_ref, afft_ref) = refs
    hy = jnp.concatenate([hy_ref[c] for c in range(HY_WIDTH // HY_CT)], axis=1).astype(BF16)
    merged = mp_ref[...].astype(F32) + gh_ref[...].astype(F32) * _dot(hy, wbh_ref[...])
    x1 = sum(r[...] for r in x_refs) + _dot(merged.astype(BF16), wo_ref[...])
    x1_ref[...] = x1
    h2 = _rms(x1, g2_ref[...])
    hp_ref[...] = _pack_bf16_pairs(h2)
    tm = h2.shape[0]
    hh, hl = _split_bf16(h2)
    wh, wl = _split_bf16(wr_ref[...])
    prod = _dot(jnp.concatenate([hh, hl], axis=0), jnp.concatenate([wh, wl], axis=1))
    logits = prod[:tm, :LANES] + prod[:tm, LANES:] + prod[tm:, :LANES]
    lane = lax.broadcasted_iota(jnp.int32, logits.shape, 1)
    logits = jnp.where(lane < N_EXPERTS, logits, -jnp.inf)
    e = jnp.exp(logits - jnp.max(logits, axis=-1, keepdims=True))
    aff = e / jnp.sum(e, axis=-1, keepdims=True)
    afft_ref[...] = aff.T[:N_EXPERTS, :]


def _mixer_c_call(l, xs, mp, gh, hy, w_branch_bf, w_out_bf, norm2_g, wr_pad):
    B, L, D = xs[0].shape
    tm = 1024
    nct = HY_WIDTH // HY_CT
    const = lambda shape: pl.BlockSpec((None,) + shape, lambda b, j: (l,) + (0,) * len(shape),
                                       pipeline_mode=pl.Buffered(1))
    return pl.pallas_call(
        functools.partial(_mixer_c_body, n_x=len(xs)),
        grid=(B, L // tm),
        in_specs=[pl.BlockSpec((None, tm, D), lambda b, j: (b, j, 0))] * len(xs) + [
            pl.BlockSpec((None, tm, D), lambda b, j: (b, j, 0)),
            pl.BlockSpec((None, tm, D), lambda b, j: (b, j, 0)),
            pl.BlockSpec((None, nct, tm, HY_CT), lambda b, j: (b, 0, j, 0)),
            pl.BlockSpec((None, None, HY_WIDTH, D), lambda b, j: (l, 1, 0, 0), pipeline_mode=pl.Buffered(1)),
            const((D, D)),
            const((1, D)),
            const((D, LANES)),
        ],
        out_specs=[
            pl.BlockSpec((None, tm, D), lambda b, j: (b, j, 0)),
            pl.BlockSpec((None, tm, D // 2), lambda b, j: (b, j, 0)),
            pl.BlockSpec((None, N_EXPERTS, tm), lambda b, j: (b, 0, j)),
        ],
        out_shape=[
            jax.ShapeDtypeStruct((B, L, D), F32),
            jax.ShapeDtypeStruct((B, L, D // 2), jnp.uint32),
            jax.ShapeDtypeStruct((B, N_EXPERTS, L), F32),
        ],
        compiler_params=_cparams(("arbitrary", "arbitrary")),
        name="mixer_c",
    )(*xs, mp, gh, hy, w_branch_bf, w_out_bf, norm2_g, wr_pad)


def _cumsum_lanes(x, tri):
    outs = []
    carry = jnp.zeros((x.shape[0], 1), F32)
    for blk in range(x.shape[1] // LANES):
        loc = _dot(x[:, blk * LANES:(blk + 1) * LANES].astype(BF16), tri) + carry
        outs.append(loc)
        carry = loc[:, LANES - 1:LANES]
    return jnp.concatenate(outs, axis=1)


def _route_body(afft_ref, tri_ref, idx_ref, gate_ref, sel_ref, thr_ref, *, cap):
    b = pl.program_id(0)
    nb, ne, seq = afft_ref.shape

    @pl.when(b == 0)
    def _():
        a_all = afft_ref[...].reshape(nb * ne, seq)

        def bit_step(i, v):
            cand = v | jnp.left_shift(jnp.int32(1), 30 - i)
            cnt = jnp.sum((a_all >= lax.bitcast_convert_type(cand, F32)).astype(F32), axis=1, keepdims=True)
            return jnp.where(cnt >= cap, cand, v)

        bits = lax.fori_loop(0, 31, bit_step, jnp.zeros((nb * ne, 1), jnp.int32))
        thr_ref[...] = jnp.broadcast_to(lax.bitcast_convert_type(bits, F32), thr_ref.shape)

    a = afft_ref[b]
    thr = thr_ref[pl.ds(pl.multiple_of(b * ne, ne), ne), 0:1]
    gt = a > thr
    eq = a == thr
    need = cap - jnp.sum(gt.astype(F32), axis=1, keepdims=True)
    tri = tri_ref[...]
    cum_eq = _cumsum_lanes(eq.astype(F32), tri)
    sel = gt | (eq & (cum_eq <= need))
    sel_ref[...] = sel.astype(F32)

    nblk = seq // LANES
    f32c = lambda v: jnp.where(v, 1.0, 0.0)
    lane = lax.broadcasted_iota(jnp.int32, (cap, LANES), 1)
    slot = lax.broadcasted_iota(jnp.int32, (cap, LANES), 0).astype(F32)
    ones_row = jnp.ones((SUBLANES, LANES), BF16)
    bi = lax.broadcasted_iota(jnp.int32, (nblk, LANES), 0)
    bj = lax.broadcasted_iota(jnp.int32, (nblk, LANES), 1)
    upper = f32c((bi <= bj) | (bj >= nblk)).astype(BF16)
    lower = f32c(bj < bi)[:, :nblk].astype(BF16)
    half = float(cap // 2)
    experts = range(ne)
    selb = [jnp.concatenate([sel_ref[e:e + 1, j * LANES:(j + 1) * LANES] for j in range(nblk)], axis=0).astype(BF16)
            for e in experts]
    loc = [_dot(selb[e], tri) for e in experts]
    tot_row = [_dot_nt(ones_row, selb[e])[:, :nblk].astype(BF16) for e in experts]
    incl_row = [_dot(tot_row[e], upper)[0:1, :] for e in experts]
    excl = [_dot(lower, jnp.broadcast_to(loc[e][:, LANES - 1:LANES], (nblk, LANES)).astype(BF16)) for e in experts]
    jb = [jnp.sum(f32c(incl_row[e] <= slot), axis=1, keepdims=True) for e in experts]
    onehot = [f32c(jb[e].astype(jnp.int32) == lane).astype(BF16) for e in experts]
    def affinity_terms(e):
        blk = jnp.concatenate([afft_ref[b, e:e + 1, j * LANES:(j + 1) * LANES] for j in range(nblk)], axis=0)
        t1 = blk.astype(BF16)
        t2 = (blk - t1.astype(F32)).astype(BF16)
        t3 = (blk - t1.astype(F32) - t2.astype(F32)).astype(BF16)
        return [t1, t2, t3]

    ncol = 6 * LANES
    pad = jnp.zeros((LANES - nblk, ncol), BF16)
    table = [jnp.concatenate([jnp.concatenate([loc[e].astype(BF16), jnp.minimum(excl[e], half).astype(BF16),
                                               jnp.maximum(excl[e] - half, 0.0).astype(BF16)]
                                              + affinity_terms(e), axis=1), pad], axis=0)
             for e in experts]
    got = [_dot(onehot[e], table[e]) for e in experts]
    idx_t = jnp.zeros((cap, LANES), F32)
    gate_t = jnp.zeros((cap, LANES), F32)
    for e in experts:
        g = got[e]
        local = slot - (g[:, LANES:2 * LANES] + g[:, 2 * LANES:3 * LANES])
        pos = jnp.sum(f32c(g[:, :LANES] <= local), axis=1, keepdims=True)
        aff_row = g[:, 3 * LANES:4 * LANES] + g[:, 4 * LANES:5 * LANES] + g[:, 5 * LANES:]
        gate = jnp.sum(jnp.where(lane == pos.astype(jnp.int32), aff_row, 0.0), axis=1, keepdims=True)
        idx_t = idx_t + jnp.where(lane == e, jb[e] * float(LANES) + pos, 0.0)
        gate_t = gate_t + jnp.where(lane == e, gate, 0.0)
    idx_ref[...] = idx_t.T[:ne, :].astype(jnp.int32)
    gate_ref[...] = gate_t


def _route_call(afft, cap):
    B, ne, L = afft.shape
    tri = jnp.asarray(np.triu(np.ones((LANES, LANES), np.float32))).astype(BF16)
    return pl.pallas_call(
        functools.partial(_route_body, cap=cap),
        grid=(B,),
        in_specs=[
            pl.BlockSpec((B, ne, L), lambda b: (0, 0, 0)),
            pl.BlockSpec((LANES, LANES), lambda b: (0, 0)),
        ],
        out_specs=[pl.BlockSpec((None, ne, cap), lambda b: (b, 0, 0)),
                   pl.BlockSpec((None, cap, LANES), lambda b: (b, 0, 0))],
        out_shape=[jax.ShapeDtypeStruct((B, ne, cap), jnp.int32),
                   jax.ShapeDtypeStruct((B, cap, LANES), F32)],
        scratch_shapes=[pltpu.VMEM((ne, L), F32), pltpu.VMEM((B * ne, LANES), F32)],
        compiler_params=_cparams(("arbitrary",)),
        name="route",
    )(afft, tri)


def _moe_body(idx_ref, idxp_ref, hp_ref, gates_ref, w1_ref, w3_ref, w2_ref, out_hbm,
              acc_ref, xin_ref, y_ref, sem_ref, *, cap, ne, npairs):
    i = pl.program_id(0)
    slot = i % 2
    e = jnp.minimum(i, npairs - 1) % ne

    @pl.when(i == 0)
    def _():
        acc_ref[...] = jnp.zeros(acc_ref.shape, F32)
        y_ref[...] = jnp.zeros(y_ref.shape, F32)

    y_prev = y_ref.at[1 - slot]
    for g0 in range(0, cap, SUBLANES):
        y_tile = y_prev[pl.ds(g0, SUBLANES), :]
        for k0 in range(0, SUBLANES, MOE_RMW_ROWS):
            ks = range(k0, k0 + MOE_RMW_ROWS)
            toks = [idxp_ref[0, 0, g0 + k] for k in ks]
            vals = [acc_ref[pl.ds(t, 1), :] + y_tile[k:k + 1, :] for t, k in zip(toks, ks)]
            for t, v in zip(toks, vals):
                acc_ref[pl.ds(t, 1), :] = v

    for s in range(cap):
        xin_ref[pl.ds(s, 1), :] = hp_ref[pl.ds(idx_ref[0, 0, s], 1), :]

    gates = gates_ref[...]
    lane = lax.broadcasted_iota(jnp.int32, gates.shape, 1)
    gate = jnp.sum(jnp.where(lane == e, gates, 0.0), axis=1, keepdims=True)
    xb = _unpack_bf16_pairs(xin_ref[...])
    a1 = _dot(xb, w1_ref[...])
    a3 = _dot(xb, w3_ref[...])
    hid = (a1 * jax.nn.sigmoid(a1) * a3).astype(BF16)
    y_ref[slot] = _dot(hid, w2_ref[...]) * gate

    @pl.when((i > 0) & (i % ne == 0))
    def _():
        cp = pltpu.make_async_copy(acc_ref, out_hbm.at[i // ne - 1], sem_ref.at[0])
        cp.start()
        cp.wait()
        acc_ref[...] = jnp.zeros(acc_ref.shape, F32)


def _moe_call(idx, gates, hp, w1_rows, w3_rows, w2_rows):
    B, ne, cap = idx.shape
    L = hp.shape[1]
    ff = w1_rows.shape[-1]
    w1_bf, w3_bf = (w.reshape(ne, D_MODEL, ff) for w in (w1_rows, w3_rows))
    w2_bf = w2_rows.reshape(ne, ff, D_MODEL)
    npairs = B * ne
    idx3 = idx.reshape(npairs, 1, cap)
    pair = lambda i: jnp.minimum(i, npairs - 1)
    prev = lambda i: jnp.maximum(i - 1, 0)
    return pl.pallas_call(
        functools.partial(_moe_body, cap=cap, ne=ne, npairs=npairs),
        grid=(npairs + 1,),
        in_specs=[
            pl.BlockSpec((1, 1, cap), lambda i: (pair(i), 0, 0), memory_space=pltpu.SMEM),
            pl.BlockSpec((1, 1, cap), lambda i: (prev(i), 0, 0), memory_space=pltpu.SMEM),
            pl.BlockSpec((None, L, D_MODEL // 2), lambda i: (pair(i) // ne, 0, 0)),
            pl.BlockSpec((None, cap, LANES), lambda i: (pair(i) // ne, 0, 0)),
            pl.BlockSpec((None, D_MODEL, ff), lambda i: (pair(i) % ne, 0, 0)),
            pl.BlockSpec((None, D_MODEL, ff), lambda i: (pair(i) % ne, 0, 0)),
            pl.BlockSpec((None, ff, D_MODEL), lambda i: (pair(i) % ne, 0, 0)),
        ],
        out_specs=pl.BlockSpec(memory_space=pl.ANY),
        out_shape=jax.ShapeDtypeStruct((B, L, D_MODEL), F32),
        scratch_shapes=[
            pltpu.VMEM((L, D_MODEL), F32),
            pltpu.VMEM((cap, D_MODEL // 2), jnp.uint32),
            pltpu.VMEM((2, cap, D_MODEL), F32),
            pltpu.SemaphoreType.DMA((1,)),
        ],
        compiler_params=_cparams(("arbitrary",)),
        name="moe",
    )(idx3, idx3, hp, gates, w1_bf, w3_bf, w2_bf)


def _final_body(*refs):
    g_ref, o_ref = refs[-2:]
    o_ref[...] = _rms(sum(r[...] for r in refs[:-2]), g_ref[...])


def _final_call(xs, g):
    B, L, D = xs[0].shape
    tm = 1024
    return pl.pallas_call(
        _final_body,
        grid=(B, L // tm),
        in_specs=[pl.BlockSpec((None, tm, D), lambda b, j: (b, j, 0))] * len(xs)
        + [pl.BlockSpec((1, D), lambda b, j: (0, 0))],
        out_specs=pl.BlockSpec((None, tm, D), lambda b, j: (b, j, 0)),
        out_shape=jax.ShapeDtypeStruct((B, L, D), F32),
        compiler_params=_cparams(("arbitrary", "arbitrary")),
        name="final_norm",
    )(*xs, g.reshape(1, D))


def kernel(x, mem, norm1_g, w_in, gate_b, sgu_ln_g, sgu_ln_b, sgu_w, sgu_b, hy_conv_w, hy_conv_b, hy_f_w1, hy_f_b1, hy_f_w2, hy_f_b2, hy_f_w3, hy_f_b3, hy_f_w4, hy_f_freq, hy_bias, mem_norm_g, w_kv, w_branch, w_out, norm2_g, w_router, w1, w3, w2, final_g):
    B, L, D = x.shape
    depth = w_in.shape[0]
    cap = EC_CAPACITY * L // N_EXPERTS
    nct = HY_WIDTH // HY_CT
    row = lambda a: a.reshape(depth, 1, a.shape[-1])

    w_kv_bf = w_kv.astype(BF16)
    w_branch_bf = w_branch.astype(BF16)
    w_out_bf = w_out.astype(BF16)
    sgu_w_bf = sgu_w.astype(BF16)
    sgu_bt = jnp.repeat(jnp.swapaxes(sgu_b, 1, 2), SGU_WIDTH // SGU_GROUPS, axis=2)
    wr_pad = jnp.pad(w_router, ((0, 0), (0, 0), (0, LANES - N_EXPERTS)))
    hy_bias4 = hy_bias.reshape(depth, nct, 1, HY_CT)

    k_all, v_all = _kv_call(mem, mem_norm_g, w_kv_bf)
    filt, w_in_bf = _filter_call(L, hy_f_w1, hy_f_b1, hy_f_w2, hy_f_b2, hy_f_w3, hy_f_b3, hy_f_w4, hy_f_freq, w_in)
    consts = _fft_constants(L)

    xs = [x]
    for l in range(depth):
        mp, gh, x0, u, w1_bf, w3_bf, w2_bf, *xsum = _mixer_a_call(
            l, xs, row(norm1_g), w_in_bf, row(gate_b), row(sgu_ln_g), row(sgu_ln_b), sgu_w_bf, sgu_bt,
            hy_conv_w, row(hy_conv_b), k_all, v_all, w_branch_bf, (w1, w3, w2))
        xs = xsum or xs
        hy = _hyena_call(l, filt, u, x0, hy_bias4, consts)
        x1, hp, afft = _mixer_c_call(l, xs, mp, gh, hy, w_branch_bf, w_out_bf, row(norm2_g), wr_pad)
        idx, gates = _route_call(afft, cap)
        xs = [x1, _moe_call(idx, gates, hp, w1_bf, w3_bf, w2_bf)]
    return _final_call(xs, final_g)
```

```python
import functools
import math

import numpy as np
import jax
import jax.numpy as jnp
from jax import lax
from jax.experimental import pallas as pl
from jax.experimental.pallas import tpu as pltpu

F32 = jnp.float32
BF16 = jnp.bfloat16

D_MODEL = 1024
SGU_WIDTH = 512
SGU_GROUPS = 4
SGU_CHUNK = 128
HY_WIDTH = 512
HY_EMB_DIM = 33
HY_BANDS = (HY_EMB_DIM - 1) // 2
HY_FILTER_HIDDEN = 64
HY_FAST_DECAY_PCT = 0.3
HY_SLOW_DECAY_PCT = 1.5
HY_DECAY_TARGET = 1e-2
XA_HEADS = 4
XA_HEAD_DIM = 128
XA_WIDTH = XA_HEADS * XA_HEAD_DIM
N_EXPERTS = 16
EC_CAPACITY = 2
NORM_EPS = 1e-6
LN_EPS = 1e-5

C_SGU = 2 * SGU_WIDTH
C_HY = C_SGU + 3 * HY_WIDTH
C_Q = C_HY + XA_WIDTH

LANES = 128
SUBLANES = 8
VMEM_LIMIT = 56 * 1024 * 1024

FFT_N2 = 128
FFT_ROWS = 256
HY_CT = 256


def _cparams(semantics):
    return pltpu.CompilerParams(dimension_semantics=semantics, vmem_limit_bytes=VMEM_LIMIT)


def _rms(x, g):
    return x * lax.rsqrt(jnp.mean(x * x, axis=-1, keepdims=True) + NORM_EPS) * g


def _dot(a, b):
    return jnp.dot(a, b, preferred_element_type=F32)


def _dot_nt(a, b):
    return lax.dot_general(a, b, (((1,), (1,)), ((), ())), preferred_element_type=F32)


def _split_bf16(x):
    hi = x.astype(BF16)
    lo = (x - hi.astype(F32)).astype(BF16)
    return hi, lo


def _kv_body(mem_ref, g_ref, w_ref, k_ref, v_ref):
    m = _rms(mem_ref[...], g_ref[...]).astype(BF16)
    kv = _dot(m, w_ref[...])
    k_ref[...] = kv[:, :XA_WIDTH].astype(BF16)
    v_ref[...] = kv[:, XA_WIDTH:].astype(BF16)


def _kv_call(mem, mem_norm_g, w_kv_bf):
    B, M, D = mem.shape
    depth = w_kv_bf.shape[0]
    return pl.pallas_call(
        _kv_body,
        grid=(depth, B),
        in_specs=[
            pl.BlockSpec((None, M, D), lambda l, b: (b, 0, 0)),
            pl.BlockSpec((None, 1, D), lambda l, b: (l, 0, 0)),
            pl.BlockSpec((None, D, 2 * XA_WIDTH), lambda l, b: (l, 0, 0)),
        ],
        out_specs=[
            pl.BlockSpec((None, None, M, XA_WIDTH), lambda l, b: (l, b, 0, 0)),
            pl.BlockSpec((None, None, M, XA_WIDTH), lambda l, b: (l, b, 0, 0)),
        ],
        out_shape=[jax.ShapeDtypeStruct((depth, B, M, XA_WIDTH), BF16)] * 2,
        compiler_params=_cparams(("arbitrary", "arbitrary")),
        name="kv",
    )(mem, mem_norm_g.reshape(depth, 1, D), w_kv_bf)


FILT_GROUPS = LANES // HY_BANDS


def _filter_body(w1t_ref, w1c_ref, w1s_ref, b1_ref, w2_ref, b2_ref, w3_ref, b3_ref, w4_ref, fr_ref,
                 bands_ref, deltas_ref, win_ref, o_ref, winb_ref, *, seq_len, rows):
    winb_ref[...] = win_ref[...].astype(BF16)
    r0 = pl.program_id(1) * rows
    ng, hid = FILT_GROUPS, HY_FILTER_HIDDEN
    rg = rows // ng

    def lags(width, per_group):
        return (r0 + lax.broadcasted_iota(jnp.int32, (rg, width), 0)
                + (lax.broadcasted_iota(jnp.int32, (rg, width), 1) // per_group) * rg)

    pos_b = lags(LANES, HY_BANDS).astype(F32)
    lag = lags(ng * hid, hid)
    t = lag.astype(F32) * (1.0 / (seq_len - 1))
    ang = pos_b * (2.0 * math.pi / seq_len) * bands_ref[...]
    hp = lax.Precision.HIGHEST
    dot = lambda a, w_ref: jnp.dot(a, w_ref[...], precision=hp, preferred_element_type=F32)
    fr = fr_ref[...]
    h = jnp.sin(fr * (t * w1t_ref[...] + dot(jnp.cos(ang), w1c_ref) - dot(jnp.sin(ang), w1s_ref) + b1_ref[...]))
    h = jnp.sin(fr * (dot(h, w2_ref) + b2_ref[...]))
    h = jnp.sin(fr * (dot(h, w3_ref) + b3_ref[...]))
    for g in range(ng):
        og = dot(h[:, g * hid:(g + 1) * hid], w4_ref)
        decay = jnp.exp(-t[:, g * hid:g * hid + 1] * deltas_ref[...])
        fwd = og[:, :HY_WIDTH] * decay
        bwd = jnp.where(lag[:, g * hid:g * hid + 1] == 0, 0.0, og[:, HY_WIDTH:] * decay)
        for c in range(HY_WIDTH // HY_CT):
            o_ref[0, c, g * rg:(g + 1) * rg, :] = fwd[:, c * HY_CT:(c + 1) * HY_CT]
            o_ref[1, c, g * rg:(g + 1) * rg, :] = bwd[:, c * HY_CT:(c + 1) * HY_CT]


def _filter_call(seq_len, w1, b1, w2, b2, w3, b3, w4, freq, w_in):
    depth = w1.shape[0]
    H = HY_FILTER_HIDDEN
    ng = FILT_GROUPS
    rows = 1024
    nsteps = seq_len // rows
    win_rows, win_cols = w_in.shape[1], w_in.shape[2]
    slab = win_rows // nsteps
    assert slab * nsteps == win_rows and slab % (2 * SUBLANES) == 0
    nct = HY_WIDTH // HY_CT
    bands = np.tile(np.linspace(1e-4, HY_BANDS - 1, HY_BANDS), ng).astype(np.float32).reshape(1, LANES)
    max_decay = math.log(HY_DECAY_TARGET) / HY_FAST_DECAY_PCT
    min_decay = math.log(HY_DECAY_TARGET) / HY_SLOW_DECAY_PCT
    deltas = np.abs(np.linspace(min_decay, max_decay, HY_WIDTH)).astype(np.float32).reshape(1, HY_WIDTH)
    eye = jnp.eye(ng, dtype=F32)
    bdiag = lambda w: jnp.einsum("gh,lij->lgihj", eye, w).reshape(depth, ng * w.shape[1], ng * w.shape[2])
    tiled = lambda v: jnp.tile(v.reshape(depth, 1, H), (1, 1, ng))
    w1t = tiled(w1[:, 0, :])
    w1c = bdiag(w1[:, 1:1 + HY_BANDS, :])
    w1s = bdiag(w1[:, 1 + HY_BANDS:, :])
    vec = lambda n: pl.BlockSpec((None, 1, n), lambda l, r: (l, 0, 0))
    mat = lambda m, n: pl.BlockSpec((None, m, n), lambda l, r: (l, 0, 0))
    gh = ng * H
    return pl.pallas_call(
        functools.partial(_filter_body, seq_len=seq_len, rows=rows),
        grid=(depth, seq_len // rows),
        in_specs=[
            vec(gh), mat(LANES, gh), mat(LANES, gh), vec(gh), mat(gh, gh), vec(gh), mat(gh, gh), vec(gh),
            mat(H, 2 * HY_WIDTH),
            vec(gh),
            pl.BlockSpec((1, LANES), lambda l, r: (0, 0)),
            pl.BlockSpec((1, HY_WIDTH), lambda l, r: (0, 0)),
            pl.BlockSpec((None, slab, win_cols), lambda l, r: (l, r, 0)),
        ],
        out_specs=[pl.BlockSpec((None, 2, nct, rows, HY_CT), lambda l, r: (l, 0, 0, r, 0)),
                   pl.BlockSpec((None, slab, win_cols), lambda l, r: (l, r, 0))],
        out_shape=[jax.ShapeDtypeStruct((depth, 2, nct, seq_len, HY_CT), F32),
                   jax.ShapeDtypeStruct(w_in.shape, BF16)],
        compiler_params=_cparams(("arbitrary", "arbitrary")),
        name="hyena_filter",
    )(w1t, w1c, w1s, tiled(b1), bdiag(w2), tiled(b2), bdiag(w3), tiled(b3), w4, tiled(freq),
      jnp.asarray(bands), jnp.asarray(deltas), w_in)


def _mixer_a_body(*refs, tm, n_x):
    x_refs, refs = refs[:3 * n_x], refs[3 * n_x:]
    (g1_ref, win_ref, gb_ref, lng_ref, lnb_ref, sw_ref, sbt_ref, cw_ref, cb_ref, k_ref, v_ref, wba_ref, wbc_ref,
     we1_ref, we3_ref, we2_ref, mp_ref, gh_ref, x0_ref, u_ref, wb1_ref, wb3_ref, wb2_ref, *xsum_ref) = refs
    x_cur = sum(r[...] for r in x_refs[0::3])
    if xsum_ref:
        xsum_ref[0][...] = x_cur
    x_prev = sum(r[...] for r in x_refs[1::3])
    x_next = sum(r[...] for r in x_refs[2::3])
    j = pl.program_id(1)
    nj = pl.num_programs(1)
    g1 = g1_ref[...]
    h = _rms(x_cur, g1)
    hb = h.astype(BF16)

    zs = _dot(hb, win_ref[:, 0:C_SGU])
    zs = 0.5 * zs * (1.0 + lax.erf(zs * (1.0 / math.sqrt(2.0))))
    us = zs[:, :SGU_WIDTH]
    vs = zs[:, SGU_WIDTH:]
    mu = jnp.mean(vs, axis=-1, keepdims=True)
    vc = vs - mu
    var = jnp.mean(vc * vc, axis=-1, keepdims=True)
    vn = (vc * lax.rsqrt(var + LN_EPS) * lng_ref[...] + lnb_ref[...]).astype(BF16)
    gd = SGU_WIDTH // SGU_GROUPS
    chunks = []
    for c in range(tm // SGU_CHUNK):
        groups = []
        for g in range(SGU_GROUPS):
            blk = vn[c * SGU_CHUNK:(c + 1) * SGU_CHUNK, g * gd:(g + 1) * gd]
            groups.append(_dot(sw_ref[g], blk))
        chunks.append(jnp.concatenate(groups, axis=1) + sbt_ref[...])
    a = us * jnp.concatenate(chunks, axis=0)

    q = _dot(hb, win_ref[:, C_HY:C_Q])
    scale = 1.0 / math.sqrt(XA_HEAD_DIM)
    heads = []
    for hd in range(XA_HEADS):
        sl = slice(hd * XA_HEAD_DIM, (hd + 1) * XA_HEAD_DIM)
        s = _dot_nt(q[:, sl].astype(BF16), k_ref[:, sl]) * scale
        s = s - jnp.max(s, axis=-1, keepdims=True)
        p = jnp.exp(s)
        p = p / jnp.sum(p, axis=-1, keepdims=True)
        heads.append(_dot(p.astype(BF16), v_ref[:, sl]))
    c_att = jnp.concatenate(heads, axis=1)

    gates = jax.nn.sigmoid(_dot(hb, win_ref[:, C_Q:]) + gb_ref[...])
    pa = _dot(a.astype(BF16), wba_ref[...])
    pc = _dot(c_att.astype(BF16), wbc_ref[...])
    mp_ref[...] = (gates[:, 0:D_MODEL] * pa + gates[:, 2 * D_MODEL:] * pc).astype(BF16)
    gh_ref[...] = gates[:, D_MODEL:2 * D_MODEL].astype(BF16)

    hprev = _rms(x_prev, g1).astype(BF16)
    hnext = _rms(x_next, g1).astype(BF16)
    hext = jnp.concatenate([hprev, hb, hnext], axis=0)
    pe = _dot(hext, win_ref[:, C_SGU:C_HY])
    rows = lax.broadcasted_iota(jnp.int32, (tm, 1), 0)
    pm1 = jnp.where((rows == 0) & (j == 0), 0.0, pe[SUBLANES - 1:SUBLANES - 1 + tm])
    pp1 = jnp.where((rows == tm - 1) & (j == nj - 1), 0.0, pe[SUBLANES + 1:SUBLANES + 1 + tm])
    p0 = pe[SUBLANES:SUBLANES + tm]
    pconv = cw_ref[0:1, :] * pm1 + cw_ref[1:2, :] * p0 + cw_ref[2:3, :] * pp1 + cb_ref[...]
    x0 = pconv[:, :HY_WIDTH]
    u = pconv[:, HY_WIDTH:2 * HY_WIDTH] * pconv[:, 2 * HY_WIDTH:]
    for c in range(HY_WIDTH // HY_CT):
        x0_ref[c] = x0[:, c * HY_CT:(c + 1) * HY_CT]
        u_ref[c] = u[:, c * HY_CT:(c + 1) * HY_CT]

    for src, dst in ((we1_ref, wb1_ref), (we3_ref, wb3_ref), (we2_ref, wb2_ref)):
        dst[...] = src[...].astype(BF16)


def _mixer_a_call(l, xs, norm1_g, w_in_bf, gate_b, sgu_ln_g, sgu_ln_b, sgu_w_bf, sgu_bt, conv_w, conv_b,
                  k_all, v_all, w_branch_bf, expert_ws):
    B, L, D = xs[0].shape
    tm = 512
    nct = HY_WIDTH // HY_CT
    nsteps = B * (L // tm)
    e_rows = expert_ws[0].shape[1] * expert_ws[0].shape[2]
    e_cols = expert_ws[0].shape[3]
    assert all(w.shape[1] * w.shape[2] == e_rows and w.shape[3] == e_cols for w in expert_ws)
    slab = e_rows // nsteps
    assert slab * nsteps == e_rows and slab % (2 * SUBLANES) == 0
    e_flat = [w.reshape(w.shape[0], e_rows, e_cols) for w in expert_ws]
    step = lambda b, j: b * (L // tm) + j
    in_cols = w_in_bf.shape[-1]
    hb_per_tile = tm // SUBLANES
    n_hb = L // SUBLANES
    const = lambda shape: pl.BlockSpec((None,) + shape, lambda b, j: (l,) + (0,) * len(shape),
                                       pipeline_mode=pl.Buffered(1))
    x_specs = [
        pl.BlockSpec((None, tm, D), lambda b, j: (b, j, 0)),
        pl.BlockSpec((None, SUBLANES, D), lambda b, j: (b, jnp.maximum(j * hb_per_tile - 1, 0), 0)),
        pl.BlockSpec((None, SUBLANES, D), lambda b, j: (b, jnp.minimum((j + 1) * hb_per_tile, n_hb - 1), 0)),
    ]
    x_args = [a for x in xs for a in (x, x, x)]
    return pl.pallas_call(
        functools.partial(_mixer_a_body, tm=tm, n_x=len(xs)),
        grid=(B, L // tm),
        in_specs=x_specs * len(xs) + [
            const((1, D)),
            const((D, in_cols)),
            const((1, 3 * D)),
            const((1, SGU_WIDTH)),
            const((1, SGU_WIDTH)),
            const((SGU_GROUPS, SGU_CHUNK, SGU_CHUNK)),
            const((SGU_CHUNK, SGU_WIDTH)),
            const((3, 3 * HY_WIDTH)),
            const((1, 3 * HY_WIDTH)),
            pl.BlockSpec((None, None, k_all.shape[2], XA_WIDTH), lambda b, j: (l, b, 0, 0)),
            pl.BlockSpec((None, None, k_all.shape[2], XA_WIDTH), lambda b, j: (l, b, 0, 0)),
            pl.BlockSpec((None, None, SGU_WIDTH, D), lambda b, j: (l, 0, 0, 0), pipeline_mode=pl.Buffered(1)),
            pl.BlockSpec((None, None, XA_WIDTH, D), lambda b, j: (l, 2, 0, 0), pipeline_mode=pl.Buffered(1)),
        ] + [pl.BlockSpec((None, slab, e_cols), lambda b, j: (l, step(b, j), 0))] * len(e_flat),
        out_specs=[
            pl.BlockSpec((None, tm, D), lambda b, j: (b, j, 0)),
            pl.BlockSpec((None, tm, D), lambda b, j: (b, j, 0)),
            pl.BlockSpec((None, nct, tm, HY_CT), lambda b, j: (b, 0, j, 0)),
            pl.BlockSpec((None, nct, tm, HY_CT), lambda b, j: (b, 0, j, 0)),
        ] + [pl.BlockSpec((slab, e_cols), lambda b, j: (step(b, j), 0))] * len(e_flat)
        + [pl.BlockSpec((None, tm, D), lambda b, j: (b, j, 0))] * (len(xs) > 1),
        out_shape=[
            jax.ShapeDtypeStruct((B, L, D), BF16),
            jax.ShapeDtypeStruct((B, L, D), BF16),
            jax.ShapeDtypeStruct((B, nct, L, HY_CT), F32),
            jax.ShapeDtypeStruct((B, nct, L, HY_CT), F32),
        ] + [jax.ShapeDtypeStruct((e_rows, e_cols), BF16)] * len(e_flat)
        + [jax.ShapeDtypeStruct((B, L, D), F32)] * (len(xs) > 1),
        compiler_params=_cparams(("arbitrary", "arbitrary")),
        name="mixer_a",
    )(*x_args, norm1_g, w_in_bf, gate_b, sgu_ln_g, sgu_ln_b, sgu_w_bf, sgu_bt, conv_w, conv_b,
      k_all, v_all, w_branch_bf, w_branch_bf, *e_flat)


def _fft_constants(seq_len):
    n = 2 * seq_len
    n2 = FFT_N2
    n1 = n // n2
    nz = seq_len // n2
    lo = FFT_ROWS // nz
    tiles = n2 // lo
    k1n = n1 // 2 + 1
    assert lo == SUBLANES and nz * lo == FFT_ROWS
    k1 = np.arange(k1n).reshape(k1n, 1, 1, 1, 1).astype(np.float64)
    part = np.arange(2).reshape(1, 2, 1, 1, 1)
    a = np.arange(lo).reshape(1, 1, lo, 1, 1)
    m1 = np.arange(nz).reshape(1, 1, 1, nz, 1).astype(np.float64)
    a2 = np.arange(lo).reshape(1, 1, 1, 1, lo)
    theta = 2.0 * np.pi * m1 * k1 / n1
    fwd = (np.where(part == 0, np.cos(theta), -np.sin(theta)) * (a == a2)).reshape(k1n * 2 * lo, nz * lo)
    weight = np.where((np.arange(k1n) == 0) | (np.arange(k1n) == n1 // 2), 1.0, 2.0) / n
    inv = (fwd.reshape(k1n, 2 * lo * nz * lo) * weight.reshape(k1n, 1)).reshape(k1n * 2 * lo, nz * lo).T
    n2_idx = (lo * np.arange(tiles).reshape(tiles, 1, 1) + np.arange(lo).reshape(1, lo, 1)).astype(np.float64)
    ang = 2.0 * np.pi * n2_idx * np.arange(k1n).reshape(1, 1, k1n) / n
    tw_cos = np.zeros((tiles, lo, LANES), np.float64)
    tw_sin = np.zeros((tiles, lo, LANES), np.float64)
    tw_cos[:, :, :k1n] = np.cos(ang)
    tw_sin[:, :, :k1n] = np.sin(ang)
    phi = 2.0 * np.pi * np.outer(np.arange(n2), np.arange(n2)) / n2
    c, s = np.cos(phi), np.sin(phi)
    g_fwd = np.block([[c, s], [-s, c]])
    g_inv = np.block([[c, -s], [s, c]])
    f32 = lambda v: jnp.asarray(v.astype(np.float32))
    bf = lambda v: f32(v).astype(BF16)
    return bf(fwd), bf(inv), bf(g_fwd), bf(g_inv), f32(tw_cos), f32(tw_sin), (tiles, k1n, lo, nz)


def _hyena_body(filt_ref, u_ref, x0_ref, bias_ref, fa_ref, ia_ref, gf_ref, gi_ref, twc_ref, tws_ref,
                o_ref, z_ref, k_ref, *, tiles, k1n, lo, nz):
    s = pl.program_id(1)
    half = FFT_N2
    rows = nz * lo
    ct = HY_CT

    def twiddles(j):
        c, sn = twc_ref[j], tws_ref[j]
        bc = lambda t: jnp.stack([jnp.broadcast_to(t[:, k1:k1 + 1], (lo, ct)) for k1 in range(k1n)])
        return bc(c), bc(sn)

    def stage_a(src_ref):
        for j in range(tiles):
            tile = src_ref[:, j].reshape(rows, ct).astype(BF16)
            zb = _dot(fa_ref[...], tile).reshape(k1n, 2, lo, ct)
            zr, zi = zb[:, 0], zb[:, 1]
            c, sn = twiddles(j)
            z_ref[:, 0, j] = zr * c + zi * sn
            z_ref[:, 1, j] = zi * c - zr * sn

    def spectrum(k1):
        return _dot(gf_ref[...], z_ref[k1].reshape(2 * half, ct).astype(BF16))

    @pl.when(s == 0)
    def _():
        k_ref[...] = jnp.zeros(k_ref.shape, F32)

    @pl.when(s < 2)
    def _():
        stage_a(filt_ref)
        im_sign = jnp.where(s == 0, 1.0, -1.0).astype(F32)
        for k1 in range(k1n):
            x = spectrum(k1)
            k_ref[k1] += jnp.concatenate([x[:half], im_sign * x[half:]], axis=0)

    @pl.when(s >= 2)
    def _():
        stage_a(u_ref)
        for k1 in range(k1n):
            x = spectrum(k1)
            xr, xi = x[:half], x[half:]
            kr, ki = k_ref[k1, :half], k_ref[k1, half:]
            y = jnp.concatenate([xr * kr - xi * ki, xr * ki + xi * kr], axis=0).astype(BF16)
            z_ref[k1] = _dot(gi_ref[...], y).reshape(2, tiles, lo, ct)
        bias = bias_ref[...]
        for j in range(tiles):
            vr, vi = z_ref[:, 0, j], z_ref[:, 1, j]
            c, sn = twiddles(j)
            wr = vr * c - vi * sn
            wi = vi * c + vr * sn
            zt = jnp.stack([wr, wi], axis=1).reshape(k1n * 2 * lo, ct).astype(BF16)
            y = _dot(ia_ref[...], zt)
            uu = u_ref[:, j].reshape(rows, ct)
            x0 = x0_ref[:, j].reshape(rows, ct)
            o_ref[:, j] = (x0 * (y + bias * uu)).reshape(nz, lo, ct)


def _hyena_call(l, filt, u, x0, hy_bias, consts):
    fa, ia, gf, gi, twc, tws, (tiles, k1n, lo, nz) = consts
    B, nct, L, ct = u.shape
    view = lambda a: a.reshape(a.shape[:-2] + (nz, tiles, lo, ct))
    seq = (nz, tiles, lo, ct)
    bidx = lambda s: jnp.maximum(s - 2, 0)
    single = lambda shape: pl.BlockSpec(shape, lambda c, s: (0,) * len(shape), pipeline_mode=pl.Buffered(1))
    out = pl.pallas_call(
        functools.partial(_hyena_body, tiles=tiles, k1n=k1n, lo=lo, nz=nz),
        grid=(nct, B + 2),
        in_specs=[
            pl.BlockSpec((None, None, None) + seq, lambda c, s: (l, jnp.minimum(s, 1), c, 0, 0, 0, 0),
                         pipeline_mode=pl.Buffered(1)),
            pl.BlockSpec((None, None) + seq, lambda c, s: (bidx(s), c, 0, 0, 0, 0)),
            pl.BlockSpec((None, None) + seq, lambda c, s: (bidx(s), c, 0, 0, 0, 0)),
            pl.BlockSpec((None, None, 1, ct), lambda c, s: (l, c, 0, 0)),
            single(fa.shape), single(ia.shape), single(gf.shape), single(gi.shape),
            single(twc.shape), single(tws.shape),
        ],
        out_specs=pl.BlockSpec((None, None) + seq, lambda c, s: (bidx(s), c, 0, 0, 0, 0)),
        out_shape=jax.ShapeDtypeStruct((B, nct) + seq, F32),
        scratch_shapes=[
            pltpu.VMEM((k1n, 2, tiles, lo, ct), F32),
            pltpu.VMEM((k1n, 2 * FFT_N2, ct), F32),
        ],
        compiler_params=_cparams(("arbitrary", "arbitrary")),
        name="hyena_fft",
    )(view(filt), view(u), view(x0), hy_bias, fa, ia, gf, gi, twc, tws)
    return out.reshape(B, nct, L, HY_CT)


def _pack_bf16_pairs(x):
    n = x.shape[1] // 2
    return pltpu.pack_elementwise([x[:, :n], x[:, n:]], packed_dtype=BF16)


def _unpack_bf16_pairs(p):
    lo = pltpu.unpack_elementwise(p, index=0, packed_dtype=BF16, unpacked_dtype=F32)
    hi = pltpu.unpack_elementwise(p, index=1, packed_dtype=BF16, unpacked_dtype=F32)
    return jnp.concatenate([lo, hi], axis=1).astype(BF16)


def _mixer_c_body(*refs, n_x):
    x_refs, refs = refs[:n_x], refs[n_x:]
    (mp_ref, gh_ref, hy_ref, wbh_ref, wo_ref, g2_ref, wr_ref, x1_ref, hp_ref, afft_ref) = refs
    hy = jnp.concatenate([hy_ref[c] for c in range(HY_WIDTH // HY_CT)], axis=1).astype(BF16)
    merged = mp_ref[...].astype(F32) + gh_ref[...].astype(F32) * _dot(hy, wbh_ref[...])
    x1 = sum(r[...] for r in x_refs) + _dot(merged.astype(BF16), wo_ref[...])
    x1_ref[...] = x1
    h2 = _rms(x1, g2_ref[...])
    hp_ref[...] = _pack_bf16_pairs(h2)
    tm = h2.shape[0]
    hh, hl = _split_bf16(h2)
    wh, wl = _split_bf16(wr_ref[...])
    prod = _dot(jnp.concatenate([hh, hl], axis=0), jnp.concatenate([wh, wl], axis=1))
    logits = prod[:tm, :LANES] + prod[:tm, LANES:] + prod[tm:, :LANES]
    lane = lax.broadcasted_iota(jnp.int32, logits.shape, 1)
    logits = jnp.where(lane < N_EXPERTS, logits, -jnp.inf)
    e = jnp.exp(logits - jnp.max(logits, axis=-1, keepdims=True))
    aff = e / jnp.sum(e, axis=-1, keepdims=True)
    afft_ref[...] = aff.T[:N_EXPERTS, :]


def _mixer_c_call(l, xs, mp, gh, hy, w_branch_bf, w_out_bf, norm2_g, wr_pad):
    B, L, D = xs[0].shape
    tm = 1024
    nct = HY_WIDTH // HY_CT
    const = lambda shape: pl.BlockSpec((None,) + shape, lambda b, j: (l,) + (0,) * len(shape),
                                       pipeline_mode=pl.Buffered(1))
    return pl.pallas_call(
        functools.partial(_mixer_c_body, n_x=len(xs)),
        grid=(B, L // tm),
        in_specs=[pl.BlockSpec((None, tm, D), lambda b, j: (b, j, 0))] * len(xs) + [
            pl.BlockSpec((None, tm, D), lambda b, j: (b, j, 0)),
            pl.BlockSpec((None, tm, D), lambda b, j: (b, j, 0)),
            pl.BlockSpec((None, nct, tm, HY_CT), lambda b, j: (b, 0, j, 0)),
            pl.BlockSpec((None, None, HY_WIDTH, D), lambda b, j: (l, 1, 0, 0), pipeline_mode=pl.Buffered(1)),
            const((D, D)),
            const((1, D)),
            const((D, LANES)),
        ],
        out_specs=[
            pl.BlockSpec((None, tm, D), lambda b, j: (b, j, 0)),
            pl.BlockSpec((None, tm, D // 2), lambda b, j: (b, j, 0)),
            pl.BlockSpec((None, N_EXPERTS, tm), lambda b, j: (b, 0, j)),
        ],
        out_shape=[
            jax.ShapeDtypeStruct((B, L, D), F32),
            jax.ShapeDtypeStruct((B, L, D // 2), jnp.uint32),
            jax.ShapeDtypeStruct((B, N_EXPERTS, L), F32),
        ],
        compiler_params=_cparams(("arbitrary", "arbitrary")),
        name="mixer_c",
    )(*xs, mp, gh, hy, w_branch_bf, w_out_bf, norm2_g, wr_pad)


def _cumsum_lanes(x, tri):
    outs = []
    carry = jnp.zeros((x.shape[0], 1), F32)
    for blk in range(x.shape[1] // LANES):
        loc = _dot(x[:, blk * LANES:(blk + 1) * LANES].astype(BF16), tri) + carry
        outs.append(loc)
        carry = loc[:, LANES - 1:LANES]
    return jnp.concatenate(outs, axis=1)


def _route_body(afft_ref, tri_ref, idx_ref, gate_ref, sel_ref, thr_ref, *, cap):
    b = pl.program_id(0)
    nb, ne, seq = afft_ref.shape

    @pl.when(b == 0)
    def _():
        a_all = afft_ref[...].reshape(nb * ne, seq)

        def bit_step(i, v):
            cand = v | jnp.left_shift(jnp.int32(1), 30 - i)
            cnt = jnp.sum((a_all >= lax.bitcast_convert_type(cand, F32)).astype(F32), axis=1, keepdims=True)
            return jnp.where(cnt >= cap, cand, v)

        bits = lax.fori_loop(0, 31, bit_step, jnp.zeros((nb * ne, 1), jnp.int32))
        thr_ref[...] = jnp.broadcast_to(lax.bitcast_convert_type(bits, F32), thr_ref.shape)

    a = afft_ref[b]
    thr = thr_ref[pl.ds(pl.multiple_of(b * ne, ne), ne), 0:1]
    gt = a > thr
    eq = a == thr
    need = cap - jnp.sum(gt.astype(F32), axis=1, keepdims=True)
    tri = tri_ref[...]
    cum_eq = _cumsum_lanes(eq.astype(F32), tri)
    sel = gt | (eq & (cum_eq <= need))
    sel_ref[...] = sel.astype(F32)

    nblk = seq // LANES
    f32c = lambda v: jnp.where(v, 1.0, 0.0)
    lane = lax.broadcasted_iota(jnp.int32, (cap, LANES), 1)
    slot = lax.broadcasted_iota(jnp.int32, (cap, LANES), 0).astype(F32)
    ones_row = jnp.ones((SUBLANES, LANES), BF16)
    bi = lax.broadcasted_iota(jnp.int32, (nblk, LANES), 0)
    bj = lax.broadcasted_iota(jnp.int32, (nblk, LANES), 1)
    upper = f32c((bi <= bj) | (bj >= nblk)).astype(BF16)
    lower = f32c(bj < bi)[:, :nblk].astype(BF16)
    half = float(cap // 2)
    experts = range(ne)
    selb = [jnp.concatenate([sel_ref[e:e + 1, j * LANES:(j + 1) * LANES] for j in range(nblk)], axis=0).astype(BF16)
            for e in experts]
    loc = [_dot(selb[e], tri) for e in experts]
    tot_row = [_dot_nt(ones_row, selb[e])[:, :nblk].astype(BF16) for e in experts]
    incl_row = [_dot(tot_row[e], upper)[0:1, :] for e in experts]
    excl = [_dot(lower, jnp.broadcast_to(loc[e][:, LANES - 1:LANES], (nblk, LANES)).astype(BF16)) for e in experts]
    jb = [jnp.sum(f32c(incl_row[e] <= slot), axis=1, keepdims=True) for e in experts]
    onehot = [f32c(jb[e].astype(jnp.int32) == lane).astype(BF16) for e in experts]
    def affinity_terms(e):
        blk = jnp.concatenate([afft_ref[b, e:e + 1, j * LANES:(j + 1) * LANES] for j in range(nblk)], axis=0)
        t1 = blk.astype(BF16)
        t2 = (blk - t1.astype(F32)).astype(BF16)
        t3 = (blk - t1.astype(F32) - t2.astype(F32)).astype(BF16)
        return [t1, t2, t3]

    ncol = 6 * LANES
    pad = jnp.zeros((LANES - nblk, ncol), BF16)
    table = [jnp.concatenate([jnp.concatenate([loc[e].astype(BF16), jnp.minimum(excl[e], half).astype(BF16),
                                               jnp.maximum(excl[e] - half, 0.0).astype(BF16)]
                                              + affinity_terms(e), axis=1), pad], axis=0)
             for e in experts]
    got = [_dot(onehot[e], table[e]) for e in experts]
    idx_t = jnp.zeros((cap, LANES), F32)
    gate_t = jnp.zeros((cap, LANES), F32)
    for e in experts:
        g = got[e]
        local = slot - (g[:, LANES:2 * LANES] + g[:, 2 * LANES:3 * LANES])
        pos = jnp.sum(f32c(g[:, :LANES] <= local), axis=1, keepdims=True)
        aff_row = g[:, 3 * LANES:4 * LANES] + g[:, 4 * LANES:5 * LANES] + g[:, 5 * LANES:]
        gate = jnp.sum(jnp.where(lane == pos.astype(jnp.int32), aff_row, 0.0), axis=1, keepdims=True)
        idx_t = idx_t + jnp.where(lane == e, jb[e] * float(LANES) + pos, 0.0)
        gate_t = gate_t + jnp.where(lane == e, gate, 0.0)
    idx_ref[...] = idx_t.T[:ne, :].astype(jnp.int32)
    gate_ref[...] = gate_t


def _route_call(afft, cap):
    B, ne, L = afft.shape
    tri = jnp.asarray(np.triu(np.ones((LANES, LANES), np.float32))).astype(BF16)
    return pl.pallas_call(
        functools.partial(_route_body, cap=cap),
        grid=(B,),
        in_specs=[
            pl.BlockSpec((B, ne, L), lambda b: (0, 0, 0)),
            pl.BlockSpec((LANES, LANES), lambda b: (0, 0)),
        ],
        out_specs=[pl.BlockSpec((None, ne, cap), lambda b: (b, 0, 0)),
                   pl.BlockSpec((None, cap, LANES), lambda b: (b, 0, 0))],
        out_shape=[jax.ShapeDtypeStruct((B, ne, cap), jnp.int32),
                   jax.ShapeDtypeStruct((B, cap, LANES), F32)],
        scratch_shapes=[pltpu.VMEM((ne, L), F32), pltpu.VMEM((B * ne, LANES), F32)],
        compiler_params=_cparams(("arbitrary",)),
        name="route",
    )(afft, tri)


def _moe_body(idx_ref, idxp_ref, hp_ref, gates_ref, w1_ref, w3_ref, w2_ref, out_hbm,
              acc_ref, xin_ref, y_ref, sem_ref, *, cap, ne, npairs):
    i = pl.program_id(0)
    slot = i % 2
    e = jnp.minimum(i, npairs - 1) % ne

    @pl.when(i == 0)
    def _():
        acc_ref[...] = jnp.zeros(acc_ref.shape, F32)
        y_ref[...] = jnp.zeros(y_ref.shape, F32)

    y_prev = y_ref.at[1 - slot]
    for g0 in range(0, cap, SUBLANES):
        y_tile = y_prev[pl.ds(g0, SUBLANES), :]
        toks = [idxp_ref[0, 0, g0 + k] for k in range(SUBLANES)]
        vals = [acc_ref[pl.ds(t, 1), :] + y_tile[k:k + 1, :] for k, t in enumerate(toks)]
        for t, v in zip(toks, vals):
            acc_ref[pl.ds(t, 1), :] = v

    for s in range(cap):
        xin_ref[pl.ds(s, 1), :] = hp_ref[pl.ds(idx_ref[0, 0, s], 1), :]

    gates = gates_ref[...]
    lane = lax.broadcasted_iota(jnp.int32, gates.shape, 1)
    gate = jnp.sum(jnp.where(lane == e, gates, 0.0), axis=1, keepdims=True)
    xb = _unpack_bf16_pairs(xin_ref[...])
    a1 = _dot(xb, w1_ref[...])
    a3 = _dot(xb, w3_ref[...])
    hid = (a1 * jax.nn.sigmoid(a1) * a3).astype(BF16)
    y_ref[slot] = _dot(hid, w2_ref[...]) * gate

    @pl.when((i > 0) & (i % ne == 0))
    def _():
        cp = pltpu.make_async_copy(acc_ref, out_hbm.at[i // ne - 1], sem_ref.at[0])
        cp.start()
        cp.wait()
        acc_ref[...] = jnp.zeros(acc_ref.shape, F32)


def _moe_call(idx, gates, hp, w1_rows, w3_rows, w2_rows):
    B, ne, cap = idx.shape
    L = hp.shape[1]
    ff = w1_rows.shape[-1]
    w1_bf, w3_bf = (w.reshape(ne, D_MODEL, ff) for w in (w1_rows, w3_rows))
    w2_bf = w2_rows.reshape(ne, ff, D_MODEL)
    npairs = B * ne
    idx3 = idx.reshape(npairs, 1, cap)
    pair = lambda i: jnp.minimum(i, npairs - 1)
    prev = lambda i: jnp.maximum(i - 1, 0)
    return pl.pallas_call(
        functools.partial(_moe_body, cap=cap, ne=ne, npairs=npairs),
        grid=(npairs + 1,),
        in_specs=[
            pl.BlockSpec((1, 1, cap), lambda i: (pair(i), 0, 0), memory_space=pltpu.SMEM),
            pl.BlockSpec((1, 1, cap), lambda i: (prev(i), 0, 0), memory_space=pltpu.SMEM),
            pl.BlockSpec((None, L, D_MODEL // 2), lambda i: (pair(i) // ne, 0, 0)),
            pl.BlockSpec((None, cap, LANES), lambda i: (pair(i) // ne, 0, 0)),
            pl.BlockSpec((None, D_MODEL, ff), lambda i: (pair(i) % ne, 0, 0)),
            pl.BlockSpec((None, D_MODEL, ff), lambda i: (pair(i) % ne, 0, 0)),
            pl.BlockSpec((None, ff, D_MODEL), lambda i: (pair(i) % ne, 0, 0)),
        ],
        out_specs=pl.BlockSpec(memory_space=pl.ANY),
        out_shape=jax.ShapeDtypeStruct((B, L, D_MODEL), F32),
        scratch_shapes=[
            pltpu.VMEM((L, D_MODEL), F32),
            pltpu.VMEM((cap, D_MODEL // 2), jnp.uint32),
            pltpu.VMEM((2, cap, D_MODEL), F32),
            pltpu.SemaphoreType.DMA((1,)),
        ],
        compiler_params=_cparams(("arbitrary",)),
        name="moe",
    )(idx3, idx3, hp, gates, w1_bf, w3_bf, w2_bf)


def _final_body(*refs):
    g_ref, o_ref = refs[-2:]
    o_ref[...] = _rms(sum(r[...] for r in refs[:-2]), g_ref[...])


def _final_call(xs, g):
    B, L, D = xs[0].shape
    tm = 1024
    return pl.pallas_call(
        _final_body,
        grid=(B, L // tm),
        in_specs=[pl.BlockSpec((None, tm, D), lambda b, j: (b, j, 0))] * len(xs)
        + [pl.BlockSpec((1, D), lambda b, j: (0, 0))],
        out_specs=pl.BlockSpec((None, tm, D), lambda b, j: (b, j, 0)),
        out_shape=jax.ShapeDtypeStruct((B, L, D), F32),
        compiler_params=_cparams(("arbitrary", "arbitrary")),
        name="final_norm",
    )(*xs, g.reshape(1, D))


def kernel(x, mem, norm1_g, w_in, gate_b, sgu_ln_g, sgu_ln_b, sgu_w, sgu_b, hy_conv_w, hy_conv_b, hy_f_w1, hy_f_b1, hy_f_w2, hy_f_b2, hy_f_w3, hy_f_b3, hy_f_w4, hy_f_freq, hy_bias, mem_norm_g, w_kv, w_branch, w_out, norm2_g, w_router, w1, w3, w2, final_g):
    B, L, D = x.shape
    depth = w_in.shape[0]
    cap = EC_CAPACITY * L // N_EXPERTS
    nct = HY_WIDTH // HY_CT
    row = lambda a: a.reshape(depth, 1, a.shape[-1])

    w_kv_bf = w_kv.astype(BF16)
    w_branch_bf = w_branch.astype(BF16)
    w_out_bf = w_out.astype(BF16)
    sgu_w_bf = sgu_w.astype(BF16)
    sgu_bt = jnp.repeat(jnp.swapaxes(sgu_b, 1, 2), SGU_WIDTH // SGU_GROUPS, axis=2)
    wr_pad = jnp.pad(w_router, ((0, 0), (0, 0), (0, LANES - N_EXPERTS)))
    hy_bias4 = hy_bias.reshape(depth, nct, 1, HY_CT)

    k_all, v_all = _kv_call(mem, mem_norm_g, w_kv_bf)
    filt, w_in_bf = _filter_call(L, hy_f_w1, hy_f_b1, hy_f_w2, hy_f_b2, hy_f_w3, hy_f_b3, hy_f_w4, hy_f_freq, w_in)
    consts = _fft_constants(L)

    xs = [x]
    for l in range(depth):
        mp, gh, x0, u, w1_bf, w3_bf, w2_bf, *xsum = _mixer_a_call(
            l, xs, row(norm1_g), w_in_bf, row(gate_b), row(sgu_ln_g), row(sgu_ln_b), sgu_w_bf, sgu_bt,
            hy_conv_w, row(hy_conv_b), k_all, v_all, w_branch_bf, (w1, w3, w2))
        xs = xsum or xs
        hy = _hyena_call(l, filt, u, x0, hy_bias4, consts)
        x1, hp, afft = _mixer_c_call(l, xs, mp, gh, hy, w_branch_bf, w_out_bf, row(norm2_g), wr_pad)
        idx, gates = _route_call(afft, cap)
        xs = [x1, _moe_call(idx, gates, hp, w1_bf, w3_bf, w2_bf)]
    return _final_call(xs, final_g)
```

```python
import functools
import math

import numpy as np
import jax
import jax.numpy as jnp
from jax import lax
from jax.experimental import pallas as pl
from jax.experimental.pallas import tpu as pltpu

F32 = jnp.float32
BF16 = jnp.bfloat16

D_MODEL = 1024
SGU_WIDTH = 512
SGU_GROUPS = 4
SGU_CHUNK = 128
HY_WIDTH = 512
HY_EMB_DIM = 33
HY_BANDS = (HY_EMB_DIM - 1) // 2
HY_FILTER_HIDDEN = 64
HY_FAST_DECAY_PCT = 0.3
HY_SLOW_DECAY_PCT = 1.5
HY_DECAY_TARGET = 1e-2
XA_HEADS = 4
XA_HEAD_DIM = 128
XA_WIDTH = XA_HEADS * XA_HEAD_DIM
N_EXPERTS = 16
EC_CAPACITY = 2
NORM_EPS = 1e-6
LN_EPS = 1e-5

C_SGU = 2 * SGU_WIDTH
C_HY = C_SGU + 3 * HY_WIDTH
C_Q = C_HY + XA_WIDTH

LANES = 128
SUBLANES = 8
VMEM_LIMIT = 56 * 1024 * 1024

FFT_N2 = 128
FFT_ROWS = 256
HY_CT = 256


def _cparams(semantics):
    return pltpu.CompilerParams(dimension_semantics=semantics, vmem_limit_bytes=VMEM_LIMIT)


def _rms(x, g):
    return x * lax.rsqrt(jnp.mean(x * x, axis=-1, keepdims=True) + NORM_EPS) * g


def _sigmoid(x):
    return 0.5 * jnp.tanh(0.5 * x) + 0.5


def _dot(a, b):
    return jnp.dot(a, b, preferred_element_type=F32)


def _dot_nt(a, b):
    return lax.dot_general(a, b, (((1,), (1,)), ((), ())), preferred_element_type=F32)


def _split_bf16(x):
    hi = x.astype(BF16)
    lo = (x - hi.astype(F32)).astype(BF16)
    return hi, lo


def _kv_body(mem_ref, g_ref, w_ref, k_ref, v_ref):
    m = _rms(mem_ref[...], g_ref[...]).astype(BF16)
    kv = _dot(m, w_ref[...])
    k_ref[...] = kv[:, :XA_WIDTH].astype(BF16)
    v_ref[...] = kv[:, XA_WIDTH:].astype(BF16)


def _kv_call(mem, mem_norm_g, w_kv_bf):
    B, M, D = mem.shape
    depth = w_kv_bf.shape[0]
    return pl.pallas_call(
        _kv_body,
        grid=(depth, B),
        in_specs=[
            pl.BlockSpec((None, M, D), lambda l, b: (b, 0, 0)),
            pl.BlockSpec((None, 1, D), lambda l, b: (l, 0, 0)),
            pl.BlockSpec((None, D, 2 * XA_WIDTH), lambda l, b: (l, 0, 0)),
        ],
        out_specs=[
            pl.BlockSpec((None, None, M, XA_WIDTH), lambda l, b: (l, b, 0, 0)),
            pl.BlockSpec((None, None, M, XA_WIDTH), lambda l, b: (l, b, 0, 0)),
        ],
        out_shape=[jax.ShapeDtypeStruct((depth, B, M, XA_WIDTH), BF16)] * 2,
        compiler_params=_cparams(("arbitrary", "arbitrary")),
        name="kv",
    )(mem, mem_norm_g.reshape(depth, 1, D), w_kv_bf)


FILT_GROUPS = LANES // HY_BANDS


def _filter_body(w1t_ref, w1c_ref, w1s_ref, b1_ref, w2_ref, b2_ref, w3_ref, b3_ref, w4_ref, fr_ref,
                 bands_ref, deltas_ref, win_ref, o_ref, winb_ref, *, seq_len, rows):
    winb_ref[...] = win_ref[...].astype(BF16)
    r0 = pl.program_id(1) * rows
    ng, hid = FILT_GROUPS, HY_FILTER_HIDDEN
    rg = rows // ng

    def lags(width, per_group):
        return (r0 + lax.broadcasted_iota(jnp.int32, (rg, width), 0)
                + (lax.broadcasted_iota(jnp.int32, (rg, width), 1) // per_group) * rg)

    pos_b = lags(LANES, HY_BANDS).astype(F32)
    lag = lags(ng * hid, hid)
    t = lag.astype(F32) * (1.0 / (seq_len - 1))
    ang = pos_b * (2.0 * math.pi / seq_len) * bands_ref[...]
    hp = lax.Precision.HIGHEST
    dot = lambda a, w_ref: jnp.dot(a, w_ref[...], precision=hp, preferred_element_type=F32)
    fr = fr_ref[...]
    h = jnp.sin(fr * (t * w1t_ref[...] + dot(jnp.cos(ang), w1c_ref) - dot(jnp.sin(ang), w1s_ref) + b1_ref[...]))
    h = jnp.sin(fr * (dot(h, w2_ref) + b2_ref[...]))
    h = jnp.sin(fr * (dot(h, w3_ref) + b3_ref[...]))
    for g in range(ng):
        og = dot(h[:, g * hid:(g + 1) * hid], w4_ref)
        decay = jnp.exp(-t[:, g * hid:g * hid + 1] * deltas_ref[...])
        fwd = og[:, :HY_WIDTH] * decay
        bwd = jnp.where(lag[:, g * hid:g * hid + 1] == 0, 0.0, og[:, HY_WIDTH:] * decay)
        for c in range(HY_WIDTH // HY_CT):
            o_ref[0, c, g * rg:(g + 1) * rg, :] = fwd[:, c * HY_CT:(c + 1) * HY_CT]
            o_ref[1, c, g * rg:(g + 1) * rg, :] = bwd[:, c * HY_CT:(c + 1) * HY_CT]


def _filter_call(seq_len, w1, b1, w2, b2, w3, b3, w4, freq, w_in):
    depth = w1.shape[0]
    H = HY_FILTER_HIDDEN
    ng = FILT_GROUPS
    rows = 1024
    nsteps = seq_len // rows
    win_rows, win_cols = w_in.shape[1], w_in.shape[2]
    slab = win_rows // nsteps
    assert slab * nsteps == win_rows and slab % (2 * SUBLANES) == 0
    nct = HY_WIDTH // HY_CT
    bands = np.tile(np.linspace(1e-4, HY_BANDS - 1, HY_BANDS), ng).astype(np.float32).reshape(1, LANES)
    max_decay = math.log(HY_DECAY_TARGET) / HY_FAST_DECAY_PCT
    min_decay = math.log(HY_DECAY_TARGET) / HY_SLOW_DECAY_PCT
    deltas = np.abs(np.linspace(min_decay, max_decay, HY_WIDTH)).astype(np.float32).reshape(1, HY_WIDTH)
    eye = jnp.eye(ng, dtype=F32)
    bdiag = lambda w: jnp.einsum("gh,lij->lgihj", eye, w).reshape(depth, ng * w.shape[1], ng * w.shape[2])
    tiled = lambda v: jnp.tile(v.reshape(depth, 1, H), (1, 1, ng))
    w1t = tiled(w1[:, 0, :])
    w1c = bdiag(w1[:, 1:1 + HY_BANDS, :])
    w1s = bdiag(w1[:, 1 + HY_BANDS:, :])
    vec = lambda n: pl.BlockSpec((None, 1, n), lambda l, r: (l, 0, 0))
    mat = lambda m, n: pl.BlockSpec((None, m, n), lambda l, r: (l, 0, 0))
    gh = ng * H
    return pl.pallas_call(
        functools.partial(_filter_body, seq_len=seq_len, rows=rows),
        grid=(depth, seq_len // rows),
        in_specs=[
            vec(gh), mat(LANES, gh), mat(LANES, gh), vec(gh), mat(gh, gh), vec(gh), mat(gh, gh), vec(gh),
            mat(H, 2 * HY_WIDTH),
            vec(gh),
            pl.BlockSpec((1, LANES), lambda l, r: (0, 0)),
            pl.BlockSpec((1, HY_WIDTH), lambda l, r: (0, 0)),
            pl.BlockSpec((None, slab, win_cols), lambda l, r: (l, r, 0)),
        ],
        out_specs=[pl.BlockSpec((None, 2, nct, rows, HY_CT), lambda l, r: (l, 0, 0, r, 0)),
                   pl.BlockSpec((None, slab, win_cols), lambda l, r: (l, r, 0))],
        out_shape=[jax.ShapeDtypeStruct((depth, 2, nct, seq_len, HY_CT), F32),
                   jax.ShapeDtypeStruct(w_in.shape, BF16)],
        compiler_params=_cparams(("arbitrary", "arbitrary")),
        name="hyena_filter",
    )(w1t, w1c, w1s, tiled(b1), bdiag(w2), tiled(b2), bdiag(w3), tiled(b3), w4, tiled(freq),
      jnp.asarray(bands), jnp.asarray(deltas), w_in)


def _mixer_a_body(*refs, tm, n_x):
    x_refs, refs = refs[:3 * n_x], refs[3 * n_x:]
    (g1_ref, win_ref, gb_ref, lng_ref, lnb_ref, sw_ref, sbt_ref, cw_ref, cb_ref, k_ref, v_ref, wba_ref, wbc_ref,
     we1_ref, we3_ref, we2_ref, mp_ref, gh_ref, x0_ref, u_ref, wb1_ref, wb3_ref, wb2_ref, *xsum_ref) = refs
    x_cur = sum(r[...] for r in x_refs[0::3])
    if xsum_ref:
        xsum_ref[0][...] = x_cur
    x_prev = sum(r[...] for r in x_refs[1::3])
    x_next = sum(r[...] for r in x_refs[2::3])
    j = pl.program_id(1)
    nj = pl.num_programs(1)
    g1 = g1_ref[...]
    h = _rms(x_cur, g1)
    hb = h.astype(BF16)

    zs = _dot(hb, win_ref[:, 0:C_SGU])
    zs = 0.5 * zs * (1.0 + lax.erf(zs * (1.0 / math.sqrt(2.0))))
    us = zs[:, :SGU_WIDTH]
    vs = zs[:, SGU_WIDTH:]
    mu = jnp.mean(vs, axis=-1, keepdims=True)
    vc = vs - mu
    var = jnp.mean(vc * vc, axis=-1, keepdims=True)
    vn = (vc * lax.rsqrt(var + LN_EPS) * lng_ref[...] + lnb_ref[...]).astype(BF16)
    gd = SGU_WIDTH // SGU_GROUPS
    chunks = []
    for c in range(tm // SGU_CHUNK):
        groups = []
        for g in range(SGU_GROUPS):
            blk = vn[c * SGU_CHUNK:(c + 1) * SGU_CHUNK, g * gd:(g + 1) * gd]
            groups.append(_dot(sw_ref[g], blk))
        chunks.append(jnp.concatenate(groups, axis=1) + sbt_ref[...])
    a = us * jnp.concatenate(chunks, axis=0)

    q = _dot(hb, win_ref[:, C_HY:C_Q])
    scale = 1.0 / math.sqrt(XA_HEAD_DIM)
    heads = []
    for hd in range(XA_HEADS):
        sl = slice(hd * XA_HEAD_DIM, (hd + 1) * XA_HEAD_DIM)
        s = _dot_nt(q[:, sl].astype(BF16), k_ref[:, sl]) * scale
        s = s - jnp.max(s, axis=-1, keepdims=True)
        p = jnp.exp(s)
        p = p / jnp.sum(p, axis=-1, keepdims=True)
        heads.append(_dot(p.astype(BF16), v_ref[:, sl]))
    c_att = jnp.concatenate(heads, axis=1)

    gates = _sigmoid(_dot(hb, win_ref[:, C_Q:]) + gb_ref[...])
    pa = _dot(a.astype(BF16), wba_ref[...])
    pc = _dot(c_att.astype(BF16), wbc_ref[...])
    mp_ref[...] = (gates[:, 0:D_MODEL] * pa + gates[:, 2 * D_MODEL:] * pc).astype(BF16)
    gh_ref[...] = gates[:, D_MODEL:2 * D_MODEL].astype(BF16)

    hprev = _rms(x_prev, g1).astype(BF16)
    hnext = _rms(x_next, g1).astype(BF16)
    hext = jnp.concatenate([hprev, hb, hnext], axis=0)
    pe = _dot(hext, win_ref[:, C_SGU:C_HY])
    rows = lax.broadcasted_iota(jnp.int32, (tm, 1), 0)
    pm1 = jnp.where((rows == 0) & (j == 0), 0.0, pe[SUBLANES - 1:SUBLANES - 1 + tm])
    pp1 = jnp.where((rows == tm - 1) & (j == nj - 1), 0.0, pe[SUBLANES + 1:SUBLANES + 1 + tm])
    p0 = pe[SUBLANES:SUBLANES + tm]
    pconv = cw_ref[0:1, :] * pm1 + cw_ref[1:2, :] * p0 + cw_ref[2:3, :] * pp1 + cb_ref[...]
    x0 = pconv[:, :HY_WIDTH]
    u = pconv[:, HY_WIDTH:2 * HY_WIDTH] * pconv[:, 2 * HY_WIDTH:]
    for c in range(HY_WIDTH // HY_CT):
        x0_ref[c] = x0[:, c * HY_CT:(c + 1) * HY_CT]
        u_ref[c] = u[:, c * HY_CT:(c + 1) * HY_CT]

    for src, dst in ((we1_ref, wb1_ref), (we3_ref, wb3_ref), (we2_ref, wb2_ref)):
        dst[...] = src[...].astype(BF16)


def _mixer_a_call(l, xs, norm1_g, w_in_bf, gate_b, sgu_ln_g, sgu_ln_b, sgu_w_bf, sgu_bt, conv_w, conv_b,
                  k_all, v_all, w_branch_bf, expert_ws):
    B, L, D = xs[0].shape
    tm = 512
    nct = HY_WIDTH // HY_CT
    nsteps = B * (L // tm)
    e_rows = expert_ws[0].shape[1] * expert_ws[0].shape[2]
    e_cols = expert_ws[0].shape[3]
    assert all(w.shape[1] * w.shape[2] == e_rows and w.shape[3] == e_cols for w in expert_ws)
    slab = e_rows // nsteps
    assert slab * nsteps == e_rows and slab % (2 * SUBLANES) == 0
    e_flat = [w.reshape(w.shape[0], e_rows, e_cols) for w in expert_ws]
    step = lambda b, j: b * (L // tm) + j
    in_cols = w_in_bf.shape[-1]
    hb_per_tile = tm // SUBLANES
    n_hb = L // SUBLANES
    const = lambda shape: pl.BlockSpec((None,) + shape, lambda b, j: (l,) + (0,) * len(shape),
                                       pipeline_mode=pl.Buffered(1))
    x_specs = [
        pl.BlockSpec((None, tm, D), lambda b, j: (b, j, 0)),
        pl.BlockSpec((None, SUBLANES, D), lambda b, j: (b, jnp.maximum(j * hb_per_tile - 1, 0), 0)),
        pl.BlockSpec((None, SUBLANES, D), lambda b, j: (b, jnp.minimum((j + 1) * hb_per_tile, n_hb - 1), 0)),
    ]
    x_args = [a for x in xs for a in (x, x, x)]
    return pl.pallas_call(
        functools.partial(_mixer_a_body, tm=tm, n_x=len(xs)),
        grid=(B, L // tm),
        in_specs=x_specs * len(xs) + [
            const((1, D)),
            const((D, in_cols)),
            const((1, 3 * D)),
            const((1, SGU_WIDTH)),
            const((1, SGU_WIDTH)),
            const((SGU_GROUPS, SGU_CHUNK, SGU_CHUNK)),
            const((SGU_CHUNK, SGU_WIDTH)),
            const((3, 3 * HY_WIDTH)),
            const((1, 3 * HY_WIDTH)),
            pl.BlockSpec((None, None, k_all.shape[2], XA_WIDTH), lambda b, j: (l, b, 0, 0)),
            pl.BlockSpec((None, None, k_all.shape[2], XA_WIDTH), lambda b, j: (l, b, 0, 0)),
            pl.BlockSpec((None, None, SGU_WIDTH, D), lambda b, j: (l, 0, 0, 0), pipeline_mode=pl.Buffered(1)),
            pl.BlockSpec((None, None, XA_WIDTH, D), lambda b, j: (l, 2, 0, 0), pipeline_mode=pl.Buffered(1)),
        ] + [pl.BlockSpec((None, slab, e_cols), lambda b, j: (l, step(b, j), 0))] * len(e_flat),
        out_specs=[
            pl.BlockSpec((None, tm, D), lambda b, j: (b, j, 0)),
            pl.BlockSpec((None, tm, D), lambda b, j: (b, j, 0)),
            pl.BlockSpec((None, nct, tm, HY_CT), lambda b, j: (b, 0, j, 0)),
            pl.BlockSpec((None, nct, tm, HY_CT), lambda b, j: (b, 0, j, 0)),
        ] + [pl.BlockSpec((slab, e_cols), lambda b, j: (step(b, j), 0))] * len(e_flat)
        + [pl.BlockSpec((None, tm, D), lambda b, j: (b, j, 0))] * (len(xs) > 1),
        out_shape=[
            jax.ShapeDtypeStruct((B, L, D), BF16),
            jax.ShapeDtypeStruct((B, L, D), BF16),
            jax.ShapeDtypeStruct((B, nct, L, HY_CT), F32),
            jax.ShapeDtypeStruct((B, nct, L, HY_CT), F32),
        ] + [jax.ShapeDtypeStruct((e_rows, e_cols), BF16)] * len(e_flat)
        + [jax.ShapeDtypeStruct((B, L, D), F32)] * (len(xs) > 1),
        compiler_params=_cparams(("arbitrary", "arbitrary")),
        name="mixer_a",
    )(*x_args, norm1_g, w_in_bf, gate_b, sgu_ln_g, sgu_ln_b, sgu_w_bf, sgu_bt, conv_w, conv_b,
      k_all, v_all, w_branch_bf, w_branch_bf, *e_flat)


def _fft_constants(seq_len):
    n = 2 * seq_len
    n2 = FFT_N2
    n1 = n // n2
    nz = seq_len // n2
    lo = FFT_ROWS // nz
    tiles = n2 // lo
    k1n = n1 // 2 + 1
    assert lo == SUBLANES and nz * lo == FFT_ROWS
    k1 = np.arange(k1n).reshape(k1n, 1, 1, 1, 1).astype(np.float64)
    part = np.arange(2).reshape(1, 2, 1, 1, 1)
    a = np.arange(lo).reshape(1, 1, lo, 1, 1)
    m1 = np.arange(nz).reshape(1, 1, 1, nz, 1).astype(np.float64)
    a2 = np.arange(lo).reshape(1, 1, 1, 1, lo)
    theta = 2.0 * np.pi * m1 * k1 / n1
    fwd = (np.where(part == 0, np.cos(theta), -np.sin(theta)) * (a == a2)).reshape(k1n * 2 * lo, nz * lo)
    weight = np.where((np.arange(k1n) == 0) | (np.arange(k1n) == n1 // 2), 1.0, 2.0) / n
    inv = (fwd.reshape(k1n, 2 * lo * nz * lo) * weight.reshape(k1n, 1)).reshape(k1n * 2 * lo, nz * lo).T
    n2_idx = (lo * np.arange(tiles).reshape(tiles, 1, 1) + np.arange(lo).reshape(1, lo, 1)).astype(np.float64)
    ang = 2.0 * np.pi * n2_idx * np.arange(k1n).reshape(1, 1, k1n) / n
    tw_cos = np.zeros((tiles, lo, LANES), np.float64)
    tw_sin = np.zeros((tiles, lo, LANES), np.float64)
    tw_cos[:, :, :k1n] = np.cos(ang)
    tw_sin[:, :, :k1n] = np.sin(ang)
    phi = 2.0 * np.pi * np.outer(np.arange(n2), np.arange(n2)) / n2
    c, s = np.cos(phi), np.sin(phi)
    g_fwd = np.block([[c, s], [-s, c]])
    g_inv = np.block([[c, -s], [s, c]])
    f32 = lambda v: jnp.asarray(v.astype(np.float32))
    bf = lambda v: f32(v).astype(BF16)
    return bf(fwd), bf(inv), bf(g_fwd), bf(g_inv), f32(tw_cos), f32(tw_sin), (tiles, k1n, lo, nz)


def _hyena_body(filt_ref, u_ref, x0_ref, bias_ref, fa_ref, ia_ref, gf_ref, gi_ref, twc_ref, tws_ref,
                o_ref, z_ref, k_ref, *, tiles, k1n, lo, nz):
    s = pl.program_id(1)
    half = FFT_N2
    rows = nz * lo
    ct = HY_CT

    def twiddles(j):
        c, sn = twc_ref[j], tws_ref[j]
        bc = lambda t: jnp.stack([jnp.broadcast_to(t[:, k1:k1 + 1], (lo, ct)) for k1 in range(k1n)])
        return bc(c), bc(sn)

    def stage_a(src_ref):
        for j in range(tiles):
            tile = src_ref[:, j].reshape(rows, ct).astype(BF16)
            zb = _dot(fa_ref[...], tile).reshape(k1n, 2, lo, ct)
            zr, zi = zb[:, 0], zb[:, 1]
            c, sn = twiddles(j)
            z_ref[:, 0, j] = zr * c + zi * sn
            z_ref[:, 1, j] = zi * c - zr * sn

    def spectrum(k1):
        return _dot(gf_ref[...], z_ref[k1].reshape(2 * half, ct).astype(BF16))

    @pl.when(s == 0)
    def _():
        k_ref[...] = jnp.zeros(k_ref.shape, F32)

    @pl.when(s < 2)
    def _():
        stage_a(filt_ref)
        im_sign = jnp.where(s == 0, 1.0, -1.0).astype(F32)
        for k1 in range(k1n):
            x = spectrum(k1)
            k_ref[k1] += jnp.concatenate([x[:half], im_sign * x[half:]], axis=0)

    @pl.when(s >= 2)
    def _():
        stage_a(u_ref)
        for k1 in range(k1n):
            x = spectrum(k1)
            xr, xi = x[:half], x[half:]
            kr, ki = k_ref[k1, :half], k_ref[k1, half:]
            y = jnp.concatenate([xr * kr - xi * ki, xr * ki + xi * kr], axis=0).astype(BF16)
            z_ref[k1] = _dot(gi_ref[...], y).reshape(2, tiles, lo, ct)
        bias = bias_ref[...]
        for j in range(tiles):
            vr, vi = z_ref[:, 0, j], z_ref[:, 1, j]
            c, sn = twiddles(j)
            wr = vr * c - vi * sn
            wi = vi * c + vr * sn
            zt = jnp.stack([wr, wi], axis=1).reshape(k1n * 2 * lo, ct).astype(BF16)
            y = _dot(ia_ref[...], zt)
            uu = u_ref[:, j].reshape(rows, ct)
            x0 = x0_ref[:, j].reshape(rows, ct)
            o_ref[:, j] = (x0 * (y + bias * uu)).reshape(nz, lo, ct)


def _hyena_call(l, filt, u, x0, hy_bias, consts):
    fa, ia, gf, gi, twc, tws, (tiles, k1n, lo, nz) = consts
    B, nct, L, ct = u.shape
    view = lambda a: a.reshape(a.shape[:-2] + (nz, tiles, lo, ct))
    seq = (nz, tiles, lo, ct)
    bidx = lambda s: jnp.maximum(s - 2, 0)
    single = lambda shape: pl.BlockSpec(shape, lambda c, s: (0,) * len(shape), pipeline_mode=pl.Buffered(1))
    out = pl.pallas_call(
        functools.partial(_hyena_body, tiles=tiles, k1n=k1n, lo=lo, nz=nz),
        grid=(nct, B + 2),
        in_specs=[
            pl.BlockSpec((None, None, None) + seq, lambda c, s: (l, jnp.minimum(s, 1), c, 0, 0, 0, 0),
                         pipeline_mode=pl.Buffered(1)),
            pl.BlockSpec((None, None) + seq, lambda c, s: (bidx(s), c, 0, 0, 0, 0)),
            pl.BlockSpec((None, None) + seq, lambda c, s: (bidx(s), c, 0, 0, 0, 0)),
            pl.BlockSpec((None, None, 1, ct), lambda c, s: (l, c, 0, 0)),
            single(fa.shape), single(ia.shape), single(gf.shape), single(gi.shape),
            single(twc.shape), single(tws.shape),
        ],
        out_specs=pl.BlockSpec((None, None) + seq, lambda c, s: (bidx(s), c, 0, 0, 0, 0)),
        out_shape=jax.ShapeDtypeStruct((B, nct) + seq, F32),
        scratch_shapes=[
            pltpu.VMEM((k1n, 2, tiles, lo, ct), F32),
            pltpu.VMEM((k1n, 2 * FFT_N2, ct), F32),
        ],
        compiler_params=_cparams(("arbitrary", "arbitrary")),
        name="hyena_fft",
    )(view(filt), view(u), view(x0), hy_bias, fa, ia, gf, gi, twc, tws)
    return out.reshape(B, nct, L, HY_CT)


def _pack_bf16_pairs(x):
    n = x.shape[1] // 2
    return pltpu.pack_elementwise([x[:, :n], x[:, n:]], packed_dtype=BF16)


def _unpack_bf16_pairs(p):
    lo = pltpu.unpack_elementwise(p, index=0, packed_dtype=BF16, unpacked_dtype=F32)
    hi = pltpu.unpack_elementwise(p, index=1, packed_dtype=BF16, unpacked_dtype=F32)
    return jnp.concatenate([lo, hi], axis=1).astype(BF16)


def _mixer_c_body(*refs, n_x):
    x_refs, refs = refs[:n_x], refs[n_x:]
    (mp_ref, gh_ref, hy_ref, wbh_ref, wo_ref, g2_ref, wr_ref, x1_ref, hp_ref, afft_ref) = refs
    hy = jnp.concatenate([hy_ref[c] for c in range(HY_WIDTH // HY_CT)], axis=1).astype(BF16)
    merged = mp_ref[...].astype(F32) + gh_ref[...].astype(F32) * _dot(hy, wbh_ref[...])
    x1 = sum(r[...] for r in x_refs) + _dot(merged.astype(BF16), wo_ref[...])
    x1_ref[...] = x1
    h2 = _rms(x1, g2_ref[...])
    hp_ref[...] = _pack_bf16_pairs(h2)
    tm = h2.shape[0]
    hh, hl = _split_bf16(h2)
    wh, wl = _split_bf16(wr_ref[...])
    prod = _dot(jnp.concatenate([hh, hl], axis=0), jnp.concatenate([wh, wl], axis=1))
    logits = prod[:tm, :LANES] + prod[:tm, LANES:] + prod[tm:, :LANES]
    lane = lax.broadcasted_iota(jnp.int32, logits.shape, 1)
    logits = jnp.where(lane < N_EXPERTS, logits, -jnp.inf)
    e = jnp.exp(logits - jnp.max(logits, axis=-1, keepdims=True))
    aff = e / jnp.sum(e, axis=-1, keepdims=True)
    afft_ref[...] = aff.T[:N_EXPERTS, :]


def _mixer_c_call(l, xs, mp, gh, hy, w_branch_bf, w_out_bf, norm2_g, wr_pad):
    B, L, D = xs[0].shape
    tm = 1024
    nct = HY_WIDTH // HY_CT
    const = lambda shape: pl.BlockSpec((None,) + shape, lambda b, j: (l,) + (0,) * len(shape),
                                       pipeline_mode=pl.Buffered(1))
    return pl.pallas_call(
        functools.partial(_mixer_c_body, n_x=len(xs)),
        grid=(B, L // tm),
        in_specs=[pl.BlockSpec((None, tm, D), lambda b, j: (b, j, 0))] * len(xs) + [
            pl.BlockSpec((None, tm, D), lambda b, j: (b, j, 0)),
            pl.BlockSpec((None, tm, D), lambda b, j: (b, j, 0)),
            pl.BlockSpec((None, nct, tm, HY_CT), lambda b, j: (b, 0, j, 0)),
            pl.BlockSpec((None, None, HY_WIDTH, D), lambda b, j: (l, 1, 0, 0), pipeline_mode=pl.Buffered(1)),
            const((D, D)),
            const((1, D)),
            const((D, LANES)),
        ],
        out_specs=[
            pl.BlockSpec((None, tm, D), lambda b, j: (b, j, 0)),
            pl.BlockSpec((None, tm, D // 2), lambda b, j: (b, j, 0)),
            pl.BlockSpec((None, N_EXPERTS, tm), lambda b, j: (b, 0, j)),
        ],
        out_shape=[
            jax.ShapeDtypeStruct((B, L, D), F32),
            jax.ShapeDtypeStruct((B, L, D // 2), jnp.uint32),
            jax.ShapeDtypeStruct((B, N_EXPERTS, L), F32),
        ],
        compiler_params=_cparams(("arbitrary", "arbitrary")),
        name="mixer_c",
    )(*xs, mp, gh, hy, w_branch_bf, w_out_bf, norm2_g, wr_pad)


def _cumsum_lanes(x, tri):
    outs = []
    carry = jnp.zeros((x.shape[0], 1), F32)
    for blk in range(x.shape[1] // LANES):
        loc = _dot(x[:, blk * LANES:(blk + 1) * LANES].astype(BF16), tri) + carry
        outs.append(loc)
        carry = loc[:, LANES - 1:LANES]
    return jnp.concatenate(outs, axis=1)


def _route_body(afft_ref, tri_ref, idx_ref, gate_ref, sel_ref, thr_ref, *, cap):
    b = pl.program_id(0)
    nb, ne, seq = afft_ref.shape

    @pl.when(b == 0)
    def _():
        a_all = afft_ref[...].reshape(nb * ne, seq)

        def bit_step(i, v):
            cand = v | jnp.left_shift(jnp.int32(1), 30 - i)
            cnt = jnp.sum((a_all >= lax.bitcast_convert_type(cand, F32)).astype(F32), axis=1, keepdims=True)
            return jnp.where(cnt >= cap, cand, v)

        bits = lax.fori_loop(0, 31, bit_step, jnp.zeros((nb * ne, 1), jnp.int32))
        thr_ref[...] = jnp.broadcast_to(lax.bitcast_convert_type(bits, F32), thr_ref.shape)

    a = afft_ref[b]
    thr = thr_ref[pl.ds(pl.multiple_of(b * ne, ne), ne), 0:1]
    gt = a > thr
    eq = a == thr
    need = cap - jnp.sum(gt.astype(F32), axis=1, keepdims=True)
    tri = tri_ref[...]
    cum_eq = _cumsum_lanes(eq.astype(F32), tri)
    sel = gt | (eq & (cum_eq <= need))
    sel_ref[...] = sel.astype(F32)

    nblk = seq // LANES
    f32c = lambda v: jnp.where(v, 1.0, 0.0)
    lane = lax.broadcasted_iota(jnp.int32, (cap, LANES), 1)
    slot = lax.broadcasted_iota(jnp.int32, (cap, LANES), 0).astype(F32)
    ones_row = jnp.ones((SUBLANES, LANES), BF16)
    bi = lax.broadcasted_iota(jnp.int32, (nblk, LANES), 0)
    bj = lax.broadcasted_iota(jnp.int32, (nblk, LANES), 1)
    upper = f32c((bi <= bj) | (bj >= nblk)).astype(BF16)
    lower = f32c(bj < bi)[:, :nblk].astype(BF16)
    half = float(cap // 2)
    experts = range(ne)
    selb = [jnp.concatenate([sel_ref[e:e + 1, j * LANES:(j + 1) * LANES] for j in range(nblk)], axis=0).astype(BF16)
            for e in experts]
    loc = [_dot(selb[e], tri) for e in experts]
    tot_row = [_dot_nt(ones_row, selb[e])[:, :nblk].astype(BF16) for e in experts]
    incl_row = [_dot(tot_row[e], upper)[0:1, :] for e in experts]
    excl = [_dot(lower, jnp.broadcast_to(loc[e][:, LANES - 1:LANES], (nblk, LANES)).astype(BF16)) for e in experts]
    jb = [jnp.sum(f32c(incl_row[e] <= slot), axis=1, keepdims=True) for e in experts]
    onehot = [f32c(jb[e].astype(jnp.int32) == lane).astype(BF16) for e in experts]
    def affinity_terms(e):
        blk = jnp.concatenate([afft_ref[b, e:e + 1, j * LANES:(j + 1) * LANES] for j in range(nblk)], axis=0)
        t1 = blk.astype(BF16)
        t2 = (blk - t1.astype(F32)).astype(BF16)
        t3 = (blk - t1.astype(F32) - t2.astype(F32)).astype(BF16)
        return [t1, t2, t3]

    ncol = 6 * LANES
    pad = jnp.zeros((LANES - nblk, ncol), BF16)
    table = [jnp.concatenate([jnp.concatenate([loc[e].astype(BF16), jnp.minimum(excl[e], half).astype(BF16),
                                               jnp.maximum(excl[e] - half, 0.0).astype(BF16)]
                                              + affinity_terms(e), axis=1), pad], axis=0)
             for e in experts]
    got = [_dot(onehot[e], table[e]) for e in experts]
    idx_t = jnp.zeros((cap, LANES), F32)
    gate_t = jnp.zeros((cap, LANES), F32)
    for e in experts:
        g = got[e]
        local = slot - (g[:, LANES:2 * LANES] + g[:, 2 * LANES:3 * LANES])
        pos = jnp.sum(f32c(g[:, :LANES] <= local), axis=1, keepdims=True)
        aff_row = g[:, 3 * LANES:4 * LANES] + g[:, 4 * LANES:5 * LANES] + g[:, 5 * LANES:]
        gate = jnp.sum(jnp.where(lane == pos.astype(jnp.int32), aff_row, 0.0), axis=1, keepdims=True)
        idx_t = idx_t + jnp.where(lane == e, jb[e] * float(LANES) + pos, 0.0)
        gate_t = gate_t + jnp.where(lane == e, gate, 0.0)
    idx_ref[...] = idx_t.T[:ne, :].astype(jnp.int32)
    gate_ref[...] = gate_t


def _route_call(afft, cap):
    B, ne, L = afft.shape
    tri = jnp.asarray(np.triu(np.ones((LANES, LANES), np.float32))).astype(BF16)
    return pl.pallas_call(
        functools.partial(_route_body, cap=cap),
        grid=(B,),
        in_specs=[
            pl.BlockSpec((B, ne, L), lambda b: (0, 0, 0)),
            pl.BlockSpec((LANES, LANES), lambda b: (0, 0)),
        ],
        out_specs=[pl.BlockSpec((None, ne, cap), lambda b: (b, 0, 0)),
                   pl.BlockSpec((None, cap, LANES), lambda b: (b, 0, 0))],
        out_shape=[jax.ShapeDtypeStruct((B, ne, cap), jnp.int32),
                   jax.ShapeDtypeStruct((B, cap, LANES), F32)],
        scratch_shapes=[pltpu.VMEM((ne, L), F32), pltpu.VMEM((B * ne, LANES), F32)],
        compiler_params=_cparams(("arbitrary",)),
        name="route",
    )(afft, tri)


def _moe_body(idx_ref, idxp_ref, hp_ref, gates_ref, w1_ref, w3_ref, w2_ref, out_hbm,
              acc_ref, xin_ref, y_ref, sem_ref, *, cap, ne, npairs):
    i = pl.program_id(0)
    slot = i % 2
    e = jnp.minimum(i, npairs - 1) % ne

    @pl.when(i == 0)
    def _():
        acc_ref[...] = jnp.zeros(acc_ref.shape, F32)
        y_ref[...] = jnp.zeros(y_ref.shape, F32)

    y_prev = y_ref.at[1 - slot]
    for g0 in range(0, cap, SUBLANES):
        y_tile = y_prev[pl.ds(g0, SUBLANES), :]
        toks = [idxp_ref[0, 0, g0 + k] for k in range(SUBLANES)]
        vals = [acc_ref[pl.ds(t, 1), :] + y_tile[k:k + 1, :] for k, t in enumerate(toks)]
        for t, v in zip(toks, vals):
            acc_ref[pl.ds(t, 1), :] = v

    for s in range(cap):
        xin_ref[pl.ds(s, 1), :] = hp_ref[pl.ds(idx_ref[0, 0, s], 1), :]

    gates = gates_ref[...]
    lane = lax.broadcasted_iota(jnp.int32, gates.shape, 1)
    gate = jnp.sum(jnp.where(lane == e, gates, 0.0), axis=1, keepdims=True)
    xb = _unpack_bf16_pairs(xin_ref[...])
    a1 = _dot(xb, w1_ref[...])
    a3 = _dot(xb, w3_ref[...])
    hid = (a1 * _sigmoid(a1) * a3).astype(BF16)
    y_ref[slot] = _dot(hid, w2_ref[...]) * gate

    @pl.when((i > 0) & (i % ne == 0))
    def _():
        cp = pltpu.make_async_copy(acc_ref, out_hbm.at[i // ne - 1], sem_ref.at[0])
        cp.start()
        cp.wait()
        acc_ref[...] = jnp.zeros(acc_ref.shape, F32)


def _moe_call(idx, gates, hp, w1_rows, w3_rows, w2_rows):
    B, ne, cap = idx.shape
    L = hp.shape[1]
    ff = w1_rows.shape[-1]
    w1_bf, w3_bf = (w.reshape(ne, D_MODEL, ff) for w in (w1_rows, w3_rows))
    w2_bf = w2_rows.reshape(ne, ff, D_MODEL)
    npairs = B * ne
    idx3 = idx.reshape(npairs, 1, cap)
    pair = lambda i: jnp.minimum(i, npairs - 1)
    prev = lambda i: jnp.maximum(i - 1, 0)
    return pl.pallas_call(
        functools.partial(_moe_body, cap=cap, ne=ne, npairs=npairs),
        grid=(npairs + 1,),
        in_specs=[
            pl.BlockSpec((1, 1, cap), lambda i: (pair(i), 0, 0), memory_space=pltpu.SMEM),
            pl.BlockSpec((1, 1, cap), lambda i: (prev(i), 0, 0), memory_space=pltpu.SMEM),
            pl.BlockSpec((None, L, D_MODEL // 2), lambda i: (pair(i) // ne, 0, 0)),
            pl.BlockSpec((None, cap, LANES), lambda i: (pair(i) // ne, 0, 0)),
            pl.BlockSpec((None, D_MODEL, ff), lambda i: (pair(i) % ne, 0, 0)),
            pl.BlockSpec((None, D_MODEL, ff), lambda i: (pair(i) % ne, 0, 0)),
            pl.BlockSpec((None, ff, D_MODEL), lambda i: (pair(i) % ne, 0, 0)),
        ],
        out_specs=pl.BlockSpec(memory_space=pl.ANY),
        out_shape=jax.ShapeDtypeStruct((B, L, D_MODEL), F32),
        scratch_shapes=[
            pltpu.VMEM((L, D_MODEL), F32),
            pltpu.VMEM((cap, D_MODEL // 2), jnp.uint32),
            pltpu.VMEM((2, cap, D_MODEL), F32),
            pltpu.SemaphoreType.DMA((1,)),
        ],
        compiler_params=_cparams(("arbitrary",)),
        name="moe",
    )(idx3, idx3, hp, gates, w1_bf, w3_bf, w2_bf)


def _final_body(*refs):
    g_ref, o_ref = refs[-2:]
    o_ref[...] = _rms(sum(r[...] for r in refs[:-2]), g_ref[...])


def _final_call(xs, g):
    B, L, D = xs[0].shape
    tm = 1024
    return pl.pallas_call(
        _final_body,
        grid=(B, L // tm),
        in_specs=[pl.BlockSpec((None, tm, D), lambda b, j: (b, j, 0))] * len(xs)
        + [pl.BlockSpec((1, D), lambda b, j: (0, 0))],
        out_specs=pl.BlockSpec((None, tm, D), lambda b, j: (b, j, 0)),
        out_shape=jax.ShapeDtypeStruct((B, L, D), F32),
        compiler_params=_cparams(("arbitrary", "arbitrary")),
        name="final_norm",
    )(*xs, g.reshape(1, D))


def kernel(x, mem, norm1_g, w_in, gate_b, sgu_ln_g, sgu_ln_b, sgu_w, sgu_b, hy_conv_w, hy_conv_b, hy_f_w1, hy_f_b1, hy_f_w2, hy_f_b2, hy_f_w3, hy_f_b3, hy_f_w4, hy_f_freq, hy_bias, mem_norm_g, w_kv, w_branch, w_out, norm2_g, w_router, w1, w3, w2, final_g):
    B, L, D = x.shape
    depth = w_in.shape[0]
    cap = EC_CAPACITY * L // N_EXPERTS
    nct = HY_WIDTH // HY_CT
    row = lambda a: a.reshape(depth, 1, a.shape[-1])

    w_kv_bf = w_kv.astype(BF16)
    w_branch_bf = w_branch.astype(BF16)
    w_out_bf = w_out.astype(BF16)
    sgu_w_bf = sgu_w.astype(BF16)
    sgu_bt = jnp.repeat(jnp.swapaxes(sgu_b, 1, 2), SGU_WIDTH // SGU_GROUPS, axis=2)
    wr_pad = jnp.pad(w_router, ((0, 0), (0, 0), (0, LANES - N_EXPERTS)))
    hy_bias4 = hy_bias.reshape(depth, nct, 1, HY_CT)

    k_all, v_all = _kv_call(mem, mem_norm_g, w_kv_bf)
    filt, w_in_bf = _filter_call(L, hy_f_w1, hy_f_b1, hy_f_w2, hy_f_b2, hy_f_w3, hy_f_b3, hy_f_w4, hy_f_freq, w_in)
    consts = _fft_constants(L)

    xs = [x]
    for l in range(depth):
        mp, gh, x0, u, w1_bf, w3_bf, w2_bf, *xsum = _mixer_a_call(
            l, xs, row(norm1_g), w_in_bf, row(gate_b), row(sgu_ln_g), row(sgu_ln_b), sgu_w_bf, sgu_bt,
            hy_conv_w, row(hy_conv_b), k_all, v_all, w_branch_bf, (w1, w3, w2))
        xs = xsum or xs
        hy = _hyena_call(l, filt, u, x0, hy_bias4, consts)
        x1, hp, afft = _mixer_c_call(l, xs, mp, gh, hy, w_branch_bf, w_out_bf, row(norm2_g), wr_pad)
        idx, gates = _route_call(afft, cap)
        xs = [x1, _moe_call(idx, gates, hp, w1_bf, w3_bf, w2_bf)]
    return _final_call(xs, final_g)
```
